```python
import jax, jax.numpy as jnp
from jax import lax
import numpy as np


D_MODEL = 1024
BATCH = 2
SEQ = 8192
DEPTH = 4
DEC_BATCH = 8
DEC_SEQ = 32
PAST_LEN = 4096

CHUNK = 64
WINDOW = 128
N_HEADS = 8
N_KV_HEADS = 2
HEAD_DIM = 64
ATT_W = N_HEADS * HEAD_DIM
KV_W = N_KV_HEADS * HEAD_DIM
CONV_W = 512
CONV_K = 3
X_HEADS = 4
X_HEAD_DIM = 128
X_W = X_HEADS * X_HEAD_DIM
N_MEM = 256
N_BRANCH = 3
EPS = 1e-6
SPLIT_SIZES = (ATT_W, KV_W, KV_W, ATT_W, CONV_W, CONV_W, CONV_W, CONV_W, X_W, X_W, N_BRANCH * D_MODEL)
IN_W = sum(SPLIT_SIZES)
SPLIT_POINTS = tuple(int(p) for p in np.cumsum(SPLIT_SIZES)[:-1])

kernel_name = 'hybrid_swa_sink_shortconv_memxattn_stream_step'


def rmsnorm(x, g):
    x32 = x.astype(jnp.float32)
    y = x32 * lax.rsqrt(jnp.mean(x32 * x32, axis=-1, keepdims=True) + EPS)
    return (y * g.astype(jnp.float32)).astype(x.dtype)


def alibi_slopes(n):
    return jnp.power(2.0, -8.0 * jnp.arange(1, n + 1, dtype=jnp.float32) / n)


def band_attention(q, kb, vb, qpos, kpos, valid, sink):
    B, N, Q, H, D = q.shape
    G = H // N_KV_HEADS
    qg = q.reshape(B, N, Q, N_KV_HEADS, G, D)
    s = jnp.einsum('bnqkgd,bnskd->bnkgqs', qg, kb).astype(jnp.float32) * (D ** -0.5)
    dist = jnp.abs(qpos[:, :, None] - kpos[:, None, :]).astype(jnp.float32)
    slope = alibi_slopes(H).reshape(N_KV_HEADS, G)[None, None, :, :, None, None]
    s = s - slope * dist[None, :, None, None, :, :]
    s = jnp.where(valid[None, :, None, None, None, :], s, -jnp.inf)
    sink_b = sink.astype(jnp.float32).reshape(N_KV_HEADS, G)[None, None, :, :, None, None]
    m = jnp.maximum(jnp.max(s, axis=-1, keepdims=True), sink_b)
    p = jnp.exp(s - m)
    p = p / (jnp.sum(p, axis=-1, keepdims=True) + jnp.exp(sink_b - m))
    o = jnp.einsum('bnkgqs,bnskd->bnqkgd', p.astype(vb.dtype), vb)
    return o.reshape(B, N, Q, H * D)


def prompt_window_attention(q, k, v, sink):
    B, S = q.shape[:2]
    nc = S // CHUNK
    lb = WINDOW // CHUNK
    pad = ((0, 0), (WINDOW, 0), (0, 0), (0, 0))
    kp = jnp.pad(k, pad).reshape(B, nc + lb, CHUNK, N_KV_HEADS, HEAD_DIM)
    vp = jnp.pad(v, pad).reshape(B, nc + lb, CHUNK, N_KV_HEADS, HEAD_DIM)
    kb = jnp.concatenate([kp[:, j:j + nc] for j in range(lb + 1)], axis=2)
    vb = jnp.concatenate([vp[:, j:j + nc] for j in range(lb + 1)], axis=2)
    qb = q.reshape(B, nc, CHUNK, N_HEADS, HEAD_DIM)
    c0 = jnp.arange(nc)[:, None] * CHUNK
    qpos = c0 + jnp.arange(CHUNK)[None, :]
    kpos = c0 - WINDOW + jnp.arange((lb + 1) * CHUNK)[None, :]
    o = band_attention(qb, kb, vb, qpos, kpos, kpos >= 0, sink)
    return o.reshape(B, S, ATT_W)


def sample_window_attention(q, k, v, ck, cv, sink):
    B, T = q.shape[:2]
    W = ck.shape[1]
    kb = jnp.concatenate([ck, k], axis=1)[:, None]
    vb = jnp.concatenate([cv, v], axis=1)[:, None]
    qpos = (PAST_LEN + jnp.arange(T))[None, :]
    kpos = (PAST_LEN - W + jnp.arange(W + T))[None, :]
    valid = jnp.ones((1, W + T), dtype=bool)
    o = band_attention(q[:, None], kb, vb, qpos, kpos, valid, sink)
    return o.reshape(B, T, ATT_W)


def conv3(up, w, b):
    T = up.shape[1] - (CONV_K - 1)
    return sum(up[:, j:j + T] * w[j] for j in range(CONV_K)) + b


def memory_kv(mem, g, wk, wv):
    B, M = mem.shape[:2]
    mn = rmsnorm(mem, g)
    mk = (mn @ wk).reshape(B, M, X_HEADS, X_HEAD_DIM)
    mv = (mn @ wv).reshape(B, M, X_HEADS, X_HEAD_DIM)
    return mk, mv


def cross_attention(q, mk, mv):
    B, T = q.shape[:2]
    s = jnp.einsum('bthd,bmhd->bhtm', q, mk).astype(jnp.float32) * (X_HEAD_DIM ** -0.5)
    p = jax.nn.softmax(s, axis=-1).astype(mv.dtype)
    return jnp.einsum('bhtm,bmhd->bthd', p, mv).reshape(B, T, X_W)


def split_inputs(x, g, w):
    return jnp.split(rmsnorm(x, g) @ w, SPLIT_POINTS, axis=-1)


def merge_branches(attn, ga, conv, bb, gb, xattn, gx, mg, w_pa, w_pb, w_px, w_out):
    ya = (attn * jax.nn.silu(ga)) @ w_pa
    yb = (bb * conv * jax.nn.silu(gb)) @ w_pb
    yx = (xattn * jax.nn.silu(gx)) @ w_px
    sa, sb, sx = jnp.split(jax.nn.sigmoid(mg), N_BRANCH, axis=-1)
    return (sa * ya + sb * yb + sx * yx) @ w_out


def setup_inputs(seed: int = 0) -> dict:
    key = jax.random.key(seed)
    ks = jax.random.split(key, 24)
    f32 = jnp.float32
    win = min(WINDOW, PAST_LEN)
    nrm = lambda k, shape, s: (jax.random.normal(k, shape, f32) * s).astype(f32)
    return {
        'x_prompt': nrm(ks[0], (BATCH, SEQ, D_MODEL), 1.0),
        'x_sample': nrm(ks[1], (DEC_BATCH, DEC_SEQ, D_MODEL), 1.0),
        'cache_attn_k': nrm(ks[2], (DEPTH, DEC_BATCH, win, N_KV_HEADS, HEAD_DIM), 1.0),
        'cache_attn_v': nrm(ks[3], (DEPTH, DEC_BATCH, win, N_KV_HEADS, HEAD_DIM), 1.0),
        'cache_conv': nrm(ks[4], (DEPTH, DEC_BATCH, CONV_K - 1, CONV_W), 1.0),
        'cache_mem_k': nrm(ks[5], (DEPTH, DEC_BATCH, N_MEM, X_HEADS, X_HEAD_DIM), 1.0),
        'cache_mem_v': nrm(ks[6], (DEPTH, DEC_BATCH, N_MEM, X_HEADS, X_HEAD_DIM), 1.0),
        'mem_prompt': nrm(ks[7], (BATCH, N_MEM, D_MODEL), 1.0),
        'norm_g': 1.0 + nrm(ks[8], (DEPTH, D_MODEL), 0.02),
        'w_in': nrm(ks[9], (DEPTH, D_MODEL, IN_W), D_MODEL ** -0.5),
        'attn_sink': nrm(ks[10], (DEPTH, N_HEADS), 0.5),
        'w_pa': nrm(ks[11], (DEPTH, ATT_W, D_MODEL), ATT_W ** -0.5),
        'conv_w': nrm(ks[12], (DEPTH, CONV_K, CONV_W), CONV_K ** -0.5),
        'conv_b': nrm(ks[13], (DEPTH, CONV_W), 0.01),
        'w_pb': nrm(ks[14], (DEPTH, CONV_W, D_MODEL), CONV_W ** -0.5),
        'mem_norm_g': 1.0 + nrm(ks[15], (DEPTH, D_MODEL), 0.02),
        'w_mk': nrm(ks[16], (DEPTH, D_MODEL, X_W), D_MODEL ** -0.5),
        'w_mv': nrm(ks[17], (DEPTH, D_MODEL, X_W), D_MODEL ** -0.5),
        'w_px': nrm(ks[18], (DEPTH, X_W, D_MODEL), X_W ** -0.5),
        'w_out': nrm(ks[19], (DEPTH, D_MODEL, D_MODEL), D_MODEL ** -0.5),
        'final_g': 1.0 + nrm(ks[20], (D_MODEL,), 0.02),
    }


def reference(x_prompt, x_sample, cache_attn_k, cache_attn_v, cache_conv, cache_mem_k, cache_mem_v,
              mem_prompt, norm_g, w_in, attn_sink, w_pa, conv_w, conv_b, w_pb, mem_norm_g,
              w_mk, w_mv, w_px, w_out, final_g):
    xp = x_prompt
    xs = x_sample
    Bp, S = xp.shape[:2]
    Bs, T = xs.shape[:2]
    win_p = min(WINDOW, S)
    pk, pv, pc, pmk, pmv = [], [], [], [], []
    sk, sv, sc = [], [], []
    for l in range(DEPTH):
        q, k, v, ga, bb, cc, hh, gb, xq, gx, mg = split_inputs(xp, norm_g[l], w_in[l])
        q = q.reshape(Bp, S, N_HEADS, HEAD_DIM)
        k = k.reshape(Bp, S, N_KV_HEADS, HEAD_DIM)
        v = v.reshape(Bp, S, N_KV_HEADS, HEAD_DIM)
        attn = prompt_window_attention(q, k, v, attn_sink[l])
        up = jnp.pad(cc * hh, ((0, 0), (CONV_K - 1, 0), (0, 0)))
        conv = conv3(up, conv_w[l], conv_b[l])
        mk, mv = memory_kv(mem_prompt, mem_norm_g[l], w_mk[l], w_mv[l])
        xattn = cross_attention(xq.reshape(Bp, S, X_HEADS, X_HEAD_DIM), mk, mv)
        xp = xp + merge_branches(attn, ga, conv, bb, gb, xattn, gx, mg, w_pa[l], w_pb[l], w_px[l], w_out[l])
        pk.append(k[:, S - win_p:])
        pv.append(v[:, S - win_p:])
        pc.append(up[:, up.shape[1] - (CONV_K - 1):])
        pmk.append(mk)
        pmv.append(mv)
        q, k, v, ga, bb, cc, hh, gb, xq, gx, mg = split_inputs(xs, norm_g[l], w_in[l])
        q = q.reshape(Bs, T, N_HEADS, HEAD_DIM)
        k = k.reshape(Bs, T, N_KV_HEADS, HEAD_DIM)
        v = v.reshape(Bs, T, N_KV_HEADS, HEAD_DIM)
        attn = sample_window_attention(q, k, v, cache_attn_k[l], cache_attn_v[l], attn_sink[l])
        up = jnp.concatenate([cache_conv[l].astype(cc.dtype), cc * hh], axis=1)
        conv = conv3(up, conv_w[l], conv_b[l])
        xattn = cross_attention(xq.reshape(Bs, T, X_HEADS, X_HEAD_DIM), cache_mem_k[l], cache_mem_v[l])
        xs = xs + merge_branches(attn, ga, conv, bb, gb, xattn, gx, mg, w_pa[l], w_pb[l], w_px[l], w_out[l])
        sk.append(k)
        sv.append(v)
        sc.append(up[:, up.shape[1] - (CONV_K - 1):])
    y_prompt = rmsnorm(xp, final_g)
    y_sample = rmsnorm(xs, final_g)
    return (y_prompt, y_sample,
            jnp.stack(pk), jnp.stack(pv), jnp.stack(pc), jnp.stack(pmk), jnp.stack(pmv),
            jnp.stack(sk), jnp.stack(sv), jnp.stack(sc))
```

```python
import functools

import numpy as np
import jax
import jax.numpy as jnp
from jax import lax
from jax.experimental import pallas as pl
from jax.experimental.pallas import tpu as pltpu

D_MODEL = 1024
DEPTH = 4
CHUNK = 64
WINDOW = 128
N_HEADS = 8
N_KV_HEADS = 2
GROUP = N_HEADS // N_KV_HEADS
HEAD_DIM = 64
ATT_W = N_HEADS * HEAD_DIM
KV_W = N_KV_HEADS * HEAD_DIM
CONV_W = 512
CONV_K = 3
X_HEADS = 4
X_HEAD_DIM = 128
X_W = X_HEADS * X_HEAD_DIM
N_MEM = 256
EPS = 1e-6

OFF_Q = 0
OFF_K = OFF_Q + ATT_W
OFF_V = OFF_K + KV_W
OFF_GA = OFF_V + KV_W
OFF_BB = OFF_GA + ATT_W
OFF_CC = OFF_BB + CONV_W
OFF_HH = OFF_CC + CONV_W
OFF_GB = OFF_HH + CONV_W
OFF_XQ = OFF_GB + CONV_W
OFF_GX = OFF_XQ + X_W
OFF_MG = OFF_GX + X_W
IN_W = OFF_MG + 3 * D_MODEL

LANES = 128
GQ_W = GROUP * HEAD_DIM
BAND = WINDOW + CHUNK
CONV_PAD = 8
PROMPT_TILE = 512
VMEM_LIMIT = 56 * 1024 * 1024

F32 = jnp.float32
BF16 = jnp.bfloat16
_NT = (((1,), (1,)), ((), ()))


def _rms_scale(x):
    return x * lax.rsqrt(jnp.mean(x * x, axis=-1, keepdims=True) + EPS)


def _silu(x):
    return x * jax.nn.sigmoid(x)


def _dot(a, b):
    return jnp.dot(a, b, preferred_element_type=F32)


def _dup_heads(a):
    lane = lax.broadcasted_iota(jnp.int32, a.shape, 1)
    sw = pltpu.roll(a, HEAD_DIM, 1)
    lo = lane < HEAD_DIM
    return jnp.where(lo, a, sw), jnp.where(lo, sw, a)


def _band_attention(qbd, kb, vb, bias, sink_col):
    s = lax.dot_general(qbd, kb, _NT, preferred_element_type=F32) + bias
    m = jnp.maximum(jnp.max(s, axis=-1, keepdims=True), sink_col)
    p = jnp.exp(s - m).astype(BF16)
    oe = _dot(p, vb)
    den = oe[:, LANES:2 * LANES] + jnp.exp(sink_col - m)
    return oe[:, 0:LANES] / den


def _ungroup(on, nq):
    lo = lax.broadcasted_iota(jnp.int32, (nq, LANES), 1) < HEAD_DIM
    return (jnp.where(lo, on[0:nq], on[nq:2 * nq]),
            jnp.where(lo, on[2 * nq:3 * nq], on[3 * nq:4 * nq]))


def _cross_head(xq_h, mk_h, mvx_h):
    s = lax.dot_general(xq_h, mk_h, _NT, preferred_element_type=F32)
    p = jnp.exp(s - jnp.max(s, axis=-1, keepdims=True)).astype(BF16)
    oe = _dot(p, mvx_h)
    return oe[:, 0:X_HEAD_DIM] / oe[:, X_HEAD_DIM:2 * X_HEAD_DIM]


def _merge_out(x, xn, z_terms, win_ref, wp_refs, wout_ref):
    acc = None
    for i, (z, wp_ref) in enumerate(zip(z_terms, wp_refs)):
        gate = jax.nn.sigmoid(_dot(xn, win_ref[:, OFF_MG + i * D_MODEL:OFF_MG + (i + 1) * D_MODEL]))
        term = gate * _dot(z.astype(BF16), wp_ref[...])
        acc = term if acc is None else acc + term
    return x + _dot(acc.astype(BF16), wout_ref[...])


def _memkv_kernel(mem_ref, g_ref, wk_ref, wv_ref, mk_ref, mv_ref, mkb_ref, mvx_ref):
    mn = (_rms_scale(mem_ref[0]) * g_ref[0]).astype(BF16)
    mk = _dot(mn, wk_ref[0])
    mv = _dot(mn, wv_ref[0])
    mk_ref[0, 0] = mk
    mv_ref[0, 0] = mv
    mkb_ref[0, 0] = mk.astype(BF16)
    ones = jnp.ones((N_MEM, X_HEAD_DIM), BF16)
    for h in range(X_HEADS):
        mvx_ref[0, 0, h, :, 0:X_HEAD_DIM] = mv[:, h * X_HEAD_DIM:(h + 1) * X_HEAD_DIM].astype(BF16)
        mvx_ref[0, 0, h, :, X_HEAD_DIM:2 * X_HEAD_DIM] = ones


def _memory_kv(mem, g, wk, wv):
    nb = mem.shape[0]
    return pl.pallas_call(
        _memkv_kernel,
        grid=(DEPTH, nb),
        in_specs=[
            pl.BlockSpec((1, N_MEM, D_MODEL), lambda l, b: (b, 0, 0)),
            pl.BlockSpec((1, 1, D_MODEL), lambda l, b: (l, 0, 0)),
            pl.BlockSpec((1, D_MODEL, X_W), lambda l, b: (l, 0, 0)),
            pl.BlockSpec((1, D_MODEL, X_W), lambda l, b: (l, 0, 0)),
        ],
        out_specs=[
            pl.BlockSpec((1, 1, N_MEM, X_W), lambda l, b: (l, b, 0, 0)),
            pl.BlockSpec((1, 1, N_MEM, X_W), lambda l, b: (l, b, 0, 0)),
            pl.BlockSpec((1, 1, N_MEM, X_W), lambda l, b: (l, b, 0, 0)),
            pl.BlockSpec((1, 1, X_HEADS, N_MEM, 2 * X_HEAD_DIM), lambda l, b: (l, b, 0, 0, 0)),
        ],
        out_shape=[
            jax.ShapeDtypeStruct((DEPTH, nb, N_MEM, X_W), F32),
            jax.ShapeDtypeStruct((DEPTH, nb, N_MEM, X_W), F32),
            jax.ShapeDtypeStruct((DEPTH, nb, N_MEM, X_W), BF16),
            jax.ShapeDtypeStruct((DEPTH, nb, X_HEADS, N_MEM, 2 * X_HEAD_DIM), BF16),
        ],
        name="memory_kv",
    )(mem, g.reshape(DEPTH, 1, D_MODEL), wk, wv)


def _prompt_kernel(x_ref, g_ref, win_ref, wpa_ref, wpb_ref, wpx_ref, wout_ref, cw_ref, cb_ref,
                   sink_ref, bias_ref, bd_ref, mk_ref, mvx_ref, fg_ref,
                   xo_ref, kl_ref, vl_ref, ul_ref,
                   kr_ref, vv_ref, uc_ref, q_ref, attn_ref, *, tile, final):
    t = pl.program_id(1)

    @pl.when(t == 0)
    def _():
        kr_ref[:, 0:WINDOW, :] = jnp.zeros((N_KV_HEADS, WINDOW, GQ_W), BF16)
        vv_ref[:, 0:WINDOW, 0:LANES] = jnp.zeros((N_KV_HEADS, WINDOW, LANES), BF16)
        vv_ref[:, :, LANES:2 * LANES] = jnp.ones((N_KV_HEADS, WINDOW + tile, LANES), BF16)
        uc_ref[0:CONV_PAD, :] = jnp.zeros((CONV_PAD, CONV_W), F32)

    x = x_ref[0]
    xn = (_rms_scale(x) * g_ref[...]).astype(BF16)

    p1 = _dot(xn, win_ref[:, OFF_Q:OFF_BB])
    q_ref[...] = (p1[:, OFF_Q:OFF_K] * (HEAD_DIM ** -0.5)).astype(BF16)
    k = p1[:, OFF_K:OFF_V]
    v = p1[:, OFF_V:OFF_GA]
    ga = p1[:, OFF_GA:OFF_BB]
    kl_ref[0] = k[tile - WINDOW:tile]
    vl_ref[0] = v[tile - WINDOW:tile]
    for j, (kj, vj) in enumerate(zip(_dup_heads(k), _dup_heads(v))):
        kj = kj.astype(BF16)
        kr_ref[j, WINDOW:WINDOW + tile, 0:LANES] = kj
        kr_ref[j, WINDOW:WINDOW + tile, LANES:2 * LANES] = kj
        vv_ref[j, WINDOW:WINDOW + tile, 0:LANES] = vj.astype(BF16)

    def chunk_body(c, carry):
        r0 = pl.multiple_of(c * CHUNK, CHUNK)
        qc = q_ref[pl.ds(r0, CHUNK), :]
        variant = jnp.minimum(c + jnp.where(t == 0, 0, 2), 2)
        for j in range(N_KV_HEADS):
            qg = qc[:, j * GQ_W:(j + 1) * GQ_W]
            qbd = jnp.concatenate([qg] * GROUP, axis=0) * bd_ref[...]
            on = _band_attention(qbd, kr_ref[j, pl.ds(r0, BAND), :], vv_ref[j, pl.ds(r0, BAND), :],
                                 bias_ref[variant, j], sink_ref[j])
            lo, hi = _ungroup(on, CHUNK)
            attn_ref[pl.ds(r0, CHUNK), j * GQ_W:j * GQ_W + LANES] = lo
            attn_ref[pl.ds(r0, CHUNK), j * GQ_W + LANES:(j + 1) * GQ_W] = hi
        return carry

    lax.fori_loop(0, tile // CHUNK, chunk_body, 0)
    kr_ref[:, 0:WINDOW, :] = kr_ref[:, tile:tile + WINDOW, :]
    vv_ref[:, 0:WINDOW, 0:LANES] = vv_ref[:, tile:tile + WINDOW, 0:LANES]
    za = attn_ref[...] * _silu(ga)

    p2 = _dot(xn, win_ref[:, OFF_BB:OFF_XQ])
    bb = p2[:, 0:CONV_W]
    u = p2[:, CONV_W:2 * CONV_W] * p2[:, 2 * CONV_W:3 * CONV_W]
    gb = p2[:, 3 * CONV_W:4 * CONV_W]
    uc_ref[CONV_PAD:CONV_PAD + tile, :] = u
    ul_ref[0] = u[tile - (CONV_K - 1):tile]
    conv = (uc_ref[CONV_PAD - 2:CONV_PAD - 2 + tile, :] * cw_ref[0:1, :]
            + uc_ref[CONV_PAD - 1:CONV_PAD - 1 + tile, :] * cw_ref[1:2, :]
            + u * cw_ref[2:3, :]) + cb_ref[...]
    uc_ref[0:CONV_PAD, :] = uc_ref[tile:tile + CONV_PAD, :]
    zb = bb * conv * _silu(gb)

    p3 = _dot(xn, win_ref[:, OFF_XQ:OFF_MG])
    xq = (p3[:, 0:X_W] * (X_HEAD_DIM ** -0.5)).astype(BF16)
    gx = p3[:, X_W:2 * X_W]
    xattn = jnp.concatenate(
        [_cross_head(xq[:, h * X_HEAD_DIM:(h + 1) * X_HEAD_DIM],
                     mk_ref[0, :, h * X_HEAD_DIM:(h + 1) * X_HEAD_DIM], mvx_ref[0, h])
         for h in range(X_HEADS)], axis=1)
    zx = xattn * _silu(gx)

    out = _merge_out(x, xn, (za, zb, zx), win_ref, (wpa_ref, wpb_ref, wpx_ref), wout_ref)
    if final:
        out = _rms_scale(out) * fg_ref[...]
    xo_ref[0] = out


def _const_spec(shape):
    nd = len(shape)
    return pl.BlockSpec(shape, lambda b, t: (0,) * nd, pipeline_mode=pl.Buffered(1))


def _prompt_layer(x, g, win, wpa, wpb, wpx, wout, cw, cb, sink_col, bias, bd, mkb, mvx, fg, *, final):
    nb, seq, _ = x.shape
    tile = PROMPT_TILE
    assert seq % tile == 0 and tile % WINDOW == 0
    kern = functools.partial(_prompt_kernel, tile=tile, final=final)
    return pl.pallas_call(
        kern,
        grid=(nb, seq // tile),
        in_specs=[
            pl.BlockSpec((1, tile, D_MODEL), lambda b, t: (b, t, 0)),
            _const_spec((1, D_MODEL)),
            _const_spec((D_MODEL, IN_W)),
            _const_spec((ATT_W, D_MODEL)),
            _const_spec((CONV_W, D_MODEL)),
            _const_spec((X_W, D_MODEL)),
            _const_spec((D_MODEL, D_MODEL)),
            _const_spec((CONV_K, CONV_W)),
            _const_spec((1, CONV_W)),
            _const_spec((N_KV_HEADS, GROUP * CHUNK, 1)),
            _const_spec((3, N_KV_HEADS, GROUP * CHUNK, BAND)),
            _const_spec((GROUP * CHUNK, GQ_W)),
            pl.BlockSpec((1, N_MEM, X_W), lambda b, t: (b, 0, 0)),
            pl.BlockSpec((1, X_HEADS, N_MEM, 2 * X_HEAD_DIM), lambda b, t: (b, 0, 0, 0)),
            _const_spec((1, D_MODEL)),
        ],
        out_specs=[
            pl.BlockSpec((1, tile, D_MODEL), lambda b, t: (b, t, 0)),
            pl.BlockSpec((1, WINDOW, KV_W), lambda b, t: (b, 0, 0)),
            pl.BlockSpec((1, WINDOW, KV_W), lambda b, t: (b, 0, 0)),
            pl.BlockSpec((1, CONV_K - 1, CONV_W), lambda b, t: (b, 0, 0)),
        ],
        out_shape=[
            jax.ShapeDtypeStruct((nb, seq, D_MODEL), F32),
            jax.ShapeDtypeStruct((nb, WINDOW, KV_W), F32),
            jax.ShapeDtypeStruct((nb, WINDOW, KV_W), F32),
            jax.ShapeDtypeStruct((nb, CONV_K - 1, CONV_W), F32),
        ],
        scratch_shapes=[
            pltpu.VMEM((N_KV_HEADS, WINDOW + tile, GQ_W), BF16),
            pltpu.VMEM((N_KV_HEADS, WINDOW + tile, 2 * LANES), BF16),
            pltpu.VMEM((CONV_PAD + tile, CONV_W), F32),
            pltpu.VMEM((tile, ATT_W), BF16),
            pltpu.VMEM((tile, ATT_W), F32),
        ],
        compiler_params=pltpu.CompilerParams(
            dimension_semantics=("arbitrary", "arbitrary"), vmem_limit_bytes=VMEM_LIMIT),
        name="prompt_layer",
    )(x, g, win, wpa, wpb, wpx, wout, cw, cb, sink_col, bias, bd, mkb, mvx, fg)


def _sample_kernel(x_ref, g_ref, win_ref, wpa_ref, wpb_ref, wpx_ref, wout_ref, cw_ref, cb_ref,
                   sink_ref, bias_ref, bd_ref, ck_ref, cv_ref, cc_ref, cmk_ref, cmv_ref, fg_ref,
                   xo_ref, ko_ref, vo_ref, uo_ref,
                   uc_ref, attn_ref, conv_ref, xat_ref, *, nb, nt, final):
    x = x_ref[...]
    xn = (_rms_scale(x) * g_ref[...]).astype(BF16)

    p1 = _dot(xn, win_ref[:, OFF_Q:OFF_BB])
    q = (p1[:, OFF_Q:OFF_K] * (HEAD_DIM ** -0.5)).astype(BF16)
    k = p1[:, OFF_K:OFF_V]
    v = p1[:, OFF_V:OFF_GA]
    ga = p1[:, OFF_GA:OFF_BB]
    ko_ref[...] = k
    vo_ref[...] = v
    ones = jnp.ones((WINDOW + nt, LANES), BF16)
    for b in range(nb):
        rows = slice(b * nt, (b + 1) * nt)
        kall = _dup_heads(jnp.concatenate([ck_ref[b], k[rows]], axis=0))
        vall = _dup_heads(jnp.concatenate([cv_ref[b], v[rows]], axis=0))
        for j in range(N_KV_HEADS):
            kj = kall[j].astype(BF16)
            vj = vall[j].astype(BF16)
            qg = q[rows, j * GQ_W:(j + 1) * GQ_W]
            qbd = jnp.concatenate([qg] * GROUP, axis=0) * bd_ref[...]
            on = _band_attention(qbd, jnp.concatenate([kj, kj], axis=1),
                                 jnp.concatenate([vj, ones], axis=1), bias_ref[j], sink_ref[j])
            lo, hi = _ungroup(on, nt)
            attn_ref[rows, j * GQ_W:j * GQ_W + LANES] = lo
            attn_ref[rows, j * GQ_W + LANES:(j + 1) * GQ_W] = hi
    za = attn_ref[...] * _silu(ga)

    p2 = _dot(xn, win_ref[:, OFF_BB:OFF_XQ])
    bb = p2[:, 0:CONV_W]
    u = p2[:, CONV_W:2 * CONV_W] * p2[:, 2 * CONV_W:3 * CONV_W]
    gb = p2[:, 3 * CONV_W:4 * CONV_W]
    for b in range(nb):
        rows = slice(b * nt, (b + 1) * nt)
        ub = u[rows]
        uc_ref[b, CONV_PAD - (CONV_K - 1):CONV_PAD, :] = cc_ref[b]
        uc_ref[b, CONV_PAD:CONV_PAD + nt, :] = ub
        uo_ref[b] = ub[nt - (CONV_K - 1):nt]
        conv_ref[rows, :] = (uc_ref[b, CONV_PAD - 2:CONV_PAD - 2 + nt, :] * cw_ref[0:1, :]
                             + uc_ref[b, CONV_PAD - 1:CONV_PAD - 1 + nt, :] * cw_ref[1:2, :]
                             + ub * cw_ref[2:3, :]) + cb_ref[...]
    zb = bb * conv_ref[...] * _silu(gb)

    p3 = _dot(xn, win_ref[:, OFF_XQ:OFF_MG])
    xq = (p3[:, 0:X_W] * (X_HEAD_DIM ** -0.5)).astype(BF16)
    gx = p3[:, X_W:2 * X_W]
    mones = jnp.ones((N_MEM, X_HEAD_DIM), BF16)
    for b in range(nb):
        rows = slice(b * nt, (b + 1) * nt)
        for h in range(X_HEADS):
            cols = slice(h * X_HEAD_DIM, (h + 1) * X_HEAD_DIM)
            mvx = jnp.concatenate([cmv_ref[b, :, cols].astype(BF16), mones], axis=1)
            xat_ref[rows, cols] = _cross_head(xq[rows, cols], cmk_ref[b, :, cols].astype(BF16), mvx)
    zx = xat_ref[...] * _silu(gx)

    out = _merge_out(x, xn, (za, zb, zx), win_ref, (wpa_ref, wpb_ref, wpx_ref), wout_ref)
    if final:
        out = _rms_scale(out) * fg_ref[...]
    xo_ref[...] = out


def _sample_layer(x, g, win, wpa, wpb, wpx, wout, cw, cb, sink_col, bias, bd, ck, cv, cc, cmk, cmv, fg,
                  *, nb, nt, final):
    rows = nb * nt
    kern = functools.partial(_sample_kernel, nb=nb, nt=nt, final=final)
    return pl.pallas_call(
        kern,
        out_shape=[
            jax.ShapeDtypeStruct((rows, D_MODEL), F32),
            jax.ShapeDtypeStruct((rows, KV_W), F32),
            jax.ShapeDtypeStruct((rows, KV_W), F32),
            jax.ShapeDtypeStruct((nb, CONV_K - 1, CONV_W), F32),
        ],
        scratch_shapes=[
            pltpu.VMEM((nb, CONV_PAD + nt, CONV_W), F32),
            pltpu.VMEM((rows, ATT_W), F32),
            pltpu.VMEM((rows, CONV_W), F32),
            pltpu.VMEM((rows, X_W), F32),
        ],
        compiler_params=pltpu.CompilerParams(vmem_limit_bytes=VMEM_LIMIT),
        name="sample_layer",
    )(x, g, win, wpa, wpb, wpx, wout, cw, cb, sink_col, bias, bd, ck, cv, cc, cmk, cmv, fg)


def _alibi_slopes():
    return np.power(2.0, -8.0 * np.arange(1, N_HEADS + 1, dtype=np.float32) / N_HEADS).astype(np.float32)


def _alibi_bias(nq, nk, masked):
    dist = np.abs(np.arange(nq)[:, None] + WINDOW - np.arange(nk)[None, :]).astype(np.float32)
    slope = _alibi_slopes().reshape(N_KV_HEADS, GROUP)
    bias = -(slope[:, :, None, None] * dist[None, None])
    bias = np.where(np.arange(nk)[None, None, None, :] < masked, -np.inf, bias)
    return bias.reshape(N_KV_HEADS, GROUP * nq, nk).astype(np.float32)


def _block_diag_mask(nq):
    rows = np.arange(GROUP * nq)[:, None] // nq
    cols = np.arange(GQ_W)[None, :] // HEAD_DIM
    return (rows == cols).astype(np.float32)


def kernel(x_prompt, x_sample, cache_attn_k, cache_attn_v, cache_conv, cache_mem_k, cache_mem_v,
           mem_prompt, norm_g, w_in, attn_sink, w_pa, conv_w, conv_b, w_pb, mem_norm_g,
           w_mk, w_mv, w_px, w_out, final_g):
    bp, seq, _ = x_prompt.shape
    bs, nt, _ = x_sample.shape
    win_len = cache_attn_k.shape[2]
    assert win_len == WINDOW and nt <= CHUNK and seq >= WINDOW

    w_in_b, w_pa_b, w_pb_b, w_px_b, w_out_b = (w.astype(BF16) for w in (w_in, w_pa, w_pb, w_px, w_out))
    w_mk_b, w_mv_b = w_mk.astype(BF16), w_mv.astype(BF16)

    bias_p = jnp.asarray(np.stack([_alibi_bias(CHUNK, BAND, m) for m in (WINDOW, CHUNK, 0)]))
    bias_s = jnp.asarray(_alibi_bias(nt, WINDOW + nt, 0))
    bd_p = jnp.asarray(_block_diag_mask(CHUNK), BF16)
    bd_s = jnp.asarray(_block_diag_mask(nt), BF16)
    sink_g = attn_sink.reshape(DEPTH, N_KV_HEADS, GROUP, 1)
    sink_p = jnp.broadcast_to(sink_g, (DEPTH, N_KV_HEADS, GROUP, CHUNK)).reshape(DEPTH, N_KV_HEADS, GROUP * CHUNK, 1)
    sink_s = jnp.broadcast_to(sink_g, (DEPTH, N_KV_HEADS, GROUP, nt)).reshape(DEPTH, N_KV_HEADS, GROUP * nt, 1)
    fg = final_g.reshape(1, D_MODEL)

    mk, mv, mkb, mvx = _memory_kv(mem_prompt, mem_norm_g, w_mk_b, w_mv_b)

    xp = x_prompt
    xs = x_sample.reshape(bs * nt, D_MODEL)
    ck = cache_attn_k.reshape(DEPTH, bs, WINDOW, KV_W)
    cv = cache_attn_v.reshape(DEPTH, bs, WINDOW, KV_W)
    cmk = cache_mem_k.reshape(DEPTH, bs, N_MEM, X_W)
    cmv = cache_mem_v.reshape(DEPTH, bs, N_MEM, X_W)
    pk, pv, pc, sk, sv, sc = [], [], [], [], [], []
    for l in range(DEPTH):
        final = l == DEPTH - 1
        shared = (norm_g[l].reshape(1, D_MODEL), w_in_b[l], w_pa_b[l], w_pb_b[l], w_px_b[l], w_out_b[l],
                  conv_w[l], conv_b[l].reshape(1, CONV_W))
        xp, kl, vl, ul = _prompt_layer(xp, *shared, sink_p[l], bias_p, bd_p, mkb[l], mvx[l], fg, final=final)
        pk.append(kl)
        pv.append(vl)
        pc.append(ul)
        xs, ko, vo, uo = _sample_layer(xs, *shared, sink_s[l], bias_s, bd_s, ck[l], cv[l], cache_conv[l],
                                       cmk[l], cmv[l], fg, nb=bs, nt=nt, final=final)
        sk.append(ko)
        sv.append(vo)
        sc.append(uo)

    kv_p = (DEPTH, bp, WINDOW, N_KV_HEADS, HEAD_DIM)
    kv_s = (DEPTH, bs, nt, N_KV_HEADS, HEAD_DIM)
    mem_shape = (DEPTH, bp, N_MEM, X_HEADS, X_HEAD_DIM)
    return (xp, xs.reshape(bs, nt, D_MODEL),
            jnp.stack(pk).reshape(kv_p), jnp.stack(pv).reshape(kv_p), jnp.stack(pc),
            mk.reshape(mem_shape), mv.reshape(mem_shape),
            jnp.stack(sk).reshape(kv_s), jnp.stack(sv).reshape(kv_s), jnp.stack(sc))
```

```python
import functools

import numpy as np
import jax
import jax.numpy as jnp
from jax import lax
from jax.experimental import pallas as pl
from jax.experimental.pallas import tpu as pltpu

D_MODEL = 1024
DEPTH = 4
CHUNK = 64
WINDOW = 128
N_HEADS = 8
N_KV_HEADS = 2
GROUP = N_HEADS // N_KV_HEADS
HEAD_DIM = 64
ATT_W = N_HEADS * HEAD_DIM
KV_W = N_KV_HEADS * HEAD_DIM
CONV_W = 512
CONV_K = 3
X_HEADS = 4
X_HEAD_DIM = 128
X_W = X_HEADS * X_HEAD_DIM
N_MEM = 256
EPS = 1e-6

OFF_Q = 0
OFF_K = OFF_Q + ATT_W
OFF_V = OFF_K + KV_W
OFF_GA = OFF_V + KV_W
OFF_BB = OFF_GA + ATT_W
OFF_CC = OFF_BB + CONV_W
OFF_HH = OFF_CC + CONV_W
OFF_GB = OFF_HH + CONV_W
OFF_XQ = OFF_GB + CONV_W
OFF_GX = OFF_XQ + X_W
OFF_MG = OFF_GX + X_W
IN_W = OFF_MG + 3 * D_MODEL

LANES = 128
GQ_W = GROUP * HEAD_DIM
BAND = WINDOW + CHUNK
CONV_PAD = 8
PROMPT_TILE = 512
VMEM_LIMIT = 56 * 1024 * 1024

F32 = jnp.float32
BF16 = jnp.bfloat16
_NT = (((1,), (1,)), ((), ()))


def _rms_scale(x):
    return x * lax.rsqrt(jnp.mean(x * x, axis=-1, keepdims=True) + EPS)


def _silu(x):
    return x * jax.nn.sigmoid(x)


def _dot(a, b):
    return jnp.dot(a, b, preferred_element_type=F32)


def _dup_heads(a):
    lane = lax.broadcasted_iota(jnp.int32, a.shape, 1)
    sw = pltpu.roll(a, HEAD_DIM, 1)
    lo = lane < HEAD_DIM
    return jnp.where(lo, a, sw), jnp.where(lo, sw, a)


def _band_attention(qbd, kb, vb, bias, sink_col):
    s = lax.dot_general(qbd, kb, _NT, preferred_element_type=F32) + bias
    m = jnp.maximum(jnp.max(s, axis=-1, keepdims=True), sink_col)
    p = jnp.exp(s - m).astype(BF16)
    oe = _dot(p, vb)
    den = oe[:, LANES:2 * LANES] + jnp.exp(sink_col - m)
    return oe[:, 0:LANES] / den


def _ungroup(on, nq):
    lo = lax.broadcasted_iota(jnp.int32, (nq, LANES), 1) < HEAD_DIM
    return (jnp.where(lo, on[0:nq], on[nq:2 * nq]),
            jnp.where(lo, on[2 * nq:3 * nq], on[3 * nq:4 * nq]))


def _cross_head(xq_h, mk_h, mvx_h):
    s = lax.dot_general(xq_h, mk_h, _NT, preferred_element_type=F32)
    p = jnp.exp(s - jnp.max(s, axis=-1, keepdims=True)).astype(BF16)
    oe = _dot(p, mvx_h)
    return oe[:, 0:X_HEAD_DIM] / oe[:, X_HEAD_DIM:2 * X_HEAD_DIM]


def _merge_out(x, xn, z_terms, win_ref, wp_refs, wout_ref):
    acc = None
    for i, (z, wp_ref) in enumerate(zip(z_terms, wp_refs)):
        gate = jax.nn.sigmoid(_dot(xn, win_ref[:, OFF_MG + i * D_MODEL:OFF_MG + (i + 1) * D_MODEL]))
        term = gate * _dot(z.astype(BF16), wp_ref[...])
        acc = term if acc is None else acc + term
    return x + _dot(acc.astype(BF16), wout_ref[...])


def _memkv_kernel(mem_ref, g_ref, wk_ref, wv_ref, mk_ref, mv_ref, mkb_ref, mvx_ref):
    mn = (_rms_scale(mem_ref[0]) * g_ref[0]).astype(BF16)
    mk = _dot(mn, wk_ref[0])
    mv = _dot(mn, wv_ref[0])
    mk_ref[0, 0] = mk
    mv_ref[0, 0] = mv
    mkb_ref[0, 0] = mk.astype(BF16)
    ones = jnp.ones((N_MEM, X_HEAD_DIM), BF16)
    for h in range(X_HEADS):
        mvx_ref[0, 0, h, :, 0:X_HEAD_DIM] = mv[:, h * X_HEAD_DIM:(h + 1) * X_HEAD_DIM].astype(BF16)
        mvx_ref[0, 0, h, :, X_HEAD_DIM:2 * X_HEAD_DIM] = ones


def _memory_kv(mem, g, wk, wv):
    nb = mem.shape[0]
    return pl.pallas_call(
        _memkv_kernel,
        grid=(DEPTH, nb),
        in_specs=[
            pl.BlockSpec((1, N_MEM, D_MODEL), lambda l, b: (b, 0, 0)),
            pl.BlockSpec((1, 1, D_MODEL), lambda l, b: (l, 0, 0)),
            pl.BlockSpec((1, D_MODEL, X_W), lambda l, b: (l, 0, 0)),
            pl.BlockSpec((1, D_MODEL, X_W), lambda l, b: (l, 0, 0)),
        ],
        out_specs=[
            pl.BlockSpec((1, 1, N_MEM, X_W), lambda l, b: (l, b, 0, 0)),
            pl.BlockSpec((1, 1, N_MEM, X_W), lambda l, b: (l, b, 0, 0)),
            pl.BlockSpec((1, 1, N_MEM, X_W), lambda l, b: (l, b, 0, 0)),
            pl.BlockSpec((1, 1, X_HEADS, N_MEM, 2 * X_HEAD_DIM), lambda l, b: (l, b, 0, 0, 0)),
        ],
        out_shape=[
            jax.ShapeDtypeStruct((DEPTH, nb, N_MEM, X_W), F32),
            jax.ShapeDtypeStruct((DEPTH, nb, N_MEM, X_W), F32),
            jax.ShapeDtypeStruct((DEPTH, nb, N_MEM, X_W), BF16),
            jax.ShapeDtypeStruct((DEPTH, nb, X_HEADS, N_MEM, 2 * X_HEAD_DIM), BF16),
        ],
        name="memory_kv",
    )(mem, g.reshape(DEPTH, 1, D_MODEL), wk, wv)


def _prompt_kernel(x_ref, g_ref, win_ref, wpa_ref, wpb_ref, wpx_ref, wout_ref, cw_ref, cb_ref,
                   sink_ref, bias_ref, bd_ref, mk_ref, mvx_ref, fg_ref,
                   xo_ref, kl_ref, vl_ref, ul_ref,
                   kr_ref, vv_ref, uc_ref, q_ref, attn_ref, *, tile, final):
    t = pl.program_id(1)

    @pl.when(t == 0)
    def _():
        kr_ref[:, 0:WINDOW, :] = jnp.zeros((N_KV_HEADS, WINDOW, GQ_W), BF16)
        vv_ref[:, 0:WINDOW, 0:LANES] = jnp.zeros((N_KV_HEADS, WINDOW, LANES), BF16)
        vv_ref[:, :, LANES:2 * LANES] = jnp.ones((N_KV_HEADS, WINDOW + tile, LANES), BF16)
        uc_ref[0:CONV_PAD, :] = jnp.zeros((CONV_PAD, CONV_W), F32)

    x = x_ref[0]
    xn = (_rms_scale(x) * g_ref[...]).astype(BF16)

    p1 = _dot(xn, win_ref[:, OFF_Q:OFF_BB])
    q_ref[...] = (p1[:, OFF_Q:OFF_K] * (HEAD_DIM ** -0.5)).astype(BF16)
    k = p1[:, OFF_K:OFF_V]
    v = p1[:, OFF_V:OFF_GA]
    ga = p1[:, OFF_GA:OFF_BB]
    kl_ref[0] = k[tile - WINDOW:tile]
    vl_ref[0] = v[tile - WINDOW:tile]
    for j, (kj, vj) in enumerate(zip(_dup_heads(k), _dup_heads(v))):
        kj = kj.astype(BF16)
        kr_ref[j, WINDOW:WINDOW + tile, 0:LANES] = kj
        kr_ref[j, WINDOW:WINDOW + tile, LANES:2 * LANES] = kj
        vv_ref[j, WINDOW:WINDOW + tile, 0:LANES] = vj.astype(BF16)

    for c in range(tile // CHUNK):
        r0 = c * CHUNK
        qc = q_ref[r0:r0 + CHUNK, :]
        variant = jnp.where(t == 0, c, 2) if c < 2 else 2
        for j in range(N_KV_HEADS):
            qg = qc[:, j * GQ_W:(j + 1) * GQ_W]
            qbd = jnp.concatenate([qg] * GROUP, axis=0) * bd_ref[...]
            on = _band_attention(qbd, kr_ref[j, r0:r0 + BAND, :], vv_ref[j, r0:r0 + BAND, :],
                                 bias_ref[variant, j], sink_ref[j])
            lo, hi = _ungroup(on, CHUNK)
            attn_ref[r0:r0 + CHUNK, j * GQ_W:j * GQ_W + LANES] = lo
            attn_ref[r0:r0 + CHUNK, j * GQ_W + LANES:(j + 1) * GQ_W] = hi
    kr_ref[:, 0:WINDOW, :] = kr_ref[:, tile:tile + WINDOW, :]
    vv_ref[:, 0:WINDOW, 0:LANES] = vv_ref[:, tile:tile + WINDOW, 0:LANES]
    za = attn_ref[...] * _silu(ga)

    p2 = _dot(xn, win_ref[:, OFF_BB:OFF_XQ])
    bb = p2[:, 0:CONV_W]
    u = p2[:, CONV_W:2 * CONV_W] * p2[:, 2 * CONV_W:3 * CONV_W]
    gb = p2[:, 3 * CONV_W:4 * CONV_W]
    uc_ref[CONV_PAD:CONV_PAD + tile, :] = u
    ul_ref[0] = u[tile - (CONV_K - 1):tile]
    conv = (uc_ref[CONV_PAD - 2:CONV_PAD - 2 + tile, :] * cw_ref[0:1, :]
            + uc_ref[CONV_PAD - 1:CONV_PAD - 1 + tile, :] * cw_ref[1:2, :]
            + u * cw_ref[2:3, :]) + cb_ref[...]
    uc_ref[0:CONV_PAD, :] = uc_ref[tile:tile + CONV_PAD, :]
    zb = bb * conv * _silu(gb)

    p3 = _dot(xn, win_ref[:, OFF_XQ:OFF_MG])
    xq = (p3[:, 0:X_W] * (X_HEAD_DIM ** -0.5)).astype(BF16)
    gx = p3[:, X_W:2 * X_W]
    xattn = jnp.concatenate(
        [_cross_head(xq[:, h * X_HEAD_DIM:(h + 1) * X_HEAD_DIM],
                     mk_ref[0, :, h * X_HEAD_DIM:(h + 1) * X_HEAD_DIM], mvx_ref[0, h])
         for h in range(X_HEADS)], axis=1)
    zx = xattn * _silu(gx)

    out = _merge_out(x, xn, (za, zb, zx), win_ref, (wpa_ref, wpb_ref, wpx_ref), wout_ref)
    if final:
        out = _rms_scale(out) * fg_ref[...]
    xo_ref[0] = out


def _const_spec(shape):
    nd = len(shape)
    return pl.BlockSpec(shape, lambda b, t: (0,) * nd, pipeline_mode=pl.Buffered(1))


def _prompt_layer(x, g, win, wpa, wpb, wpx, wout, cw, cb, sink_col, bias, bd, mkb, mvx, fg, *, final):
    nb, seq, _ = x.shape
    tile = PROMPT_TILE
    assert seq % tile == 0 and tile % WINDOW == 0
    kern = functools.partial(_prompt_kernel, tile=tile, final=final)
    return pl.pallas_call(
        kern,
        grid=(nb, seq // tile),
        in_specs=[
            pl.BlockSpec((1, tile, D_MODEL), lambda b, t: (b, t, 0)),
            _const_spec((1, D_MODEL)),
            _const_spec((D_MODEL, IN_W)),
            _const_spec((ATT_W, D_MODEL)),
            _const_spec((CONV_W, D_MODEL)),
            _const_spec((X_W, D_MODEL)),
            _const_spec((D_MODEL, D_MODEL)),
            _const_spec((CONV_K, CONV_W)),
            _const_spec((1, CONV_W)),
            _const_spec((N_KV_HEADS, GROUP * CHUNK, 1)),
            _const_spec((3, N_KV_HEADS, GROUP * CHUNK, BAND)),
            _const_spec((GROUP * CHUNK, GQ_W)),
            pl.BlockSpec((1, N_MEM, X_W), lambda b, t: (b, 0, 0)),
            pl.BlockSpec((1, X_HEADS, N_MEM, 2 * X_HEAD_DIM), lambda b, t: (b, 0, 0, 0)),
            _const_spec((1, D_MODEL)),
        ],
        out_specs=[
            pl.BlockSpec((1, tile, D_MODEL), lambda b, t: (b, t, 0)),
            pl.BlockSpec((1, WINDOW, KV_W), lambda b, t: (b, 0, 0)),
            pl.BlockSpec((1, WINDOW, KV_W), lambda b, t: (b, 0, 0)),
            pl.BlockSpec((1, CONV_K - 1, CONV_W), lambda b, t: (b, 0, 0)),
        ],
        out_shape=[
            jax.ShapeDtypeStruct((nb, seq, D_MODEL), F32),
            jax.ShapeDtypeStruct((nb, WINDOW, KV_W), F32),
            jax.ShapeDtypeStruct((nb, WINDOW, KV_W), F32),
            jax.ShapeDtypeStruct((nb, CONV_K - 1, CONV_W), F32),
        ],
        scratch_shapes=[
            pltpu.VMEM((N_KV_HEADS, WINDOW + tile, GQ_W), BF16),
            pltpu.VMEM((N_KV_HEADS, WINDOW + tile, 2 * LANES), BF16),
            pltpu.VMEM((CONV_PAD + tile, CONV_W), F32),
            pltpu.VMEM((tile, ATT_W), BF16),
            pltpu.VMEM((tile, ATT_W), F32),
        ],
        compiler_params=pltpu.CompilerParams(
            dimension_semantics=("arbitrary", "arbitrary"), vmem_limit_bytes=VMEM_LIMIT),
        name="prompt_layer",
    )(x, g, win, wpa, wpb, wpx, wout, cw, cb, sink_col, bias, bd, mkb, mvx, fg)


def _sample_kernel(x_ref, g_ref, win_ref, wpa_ref, wpb_ref, wpx_ref, wout_ref, cw_ref, cb_ref,
                   sink_ref, bias_ref, bd_ref, ck_ref, cv_ref, cc_ref, cmk_ref, cmv_ref, fg_ref,
                   xo_ref, ko_ref, vo_ref, uo_ref,
                   uc_ref, attn_ref, conv_ref, xat_ref, *, nb, nt, final):
    x = x_ref[...]
    xn = (_rms_scale(x) * g_ref[...]).astype(BF16)

    p1 = _dot(xn, win_ref[:, OFF_Q:OFF_BB])
    q = (p1[:, OFF_Q:OFF_K] * (HEAD_DIM ** -0.5)).astype(BF16)
    k = p1[:, OFF_K:OFF_V]
    v = p1[:, OFF_V:OFF_GA]
    ga = p1[:, OFF_GA:OFF_BB]
    ko_ref[...] = k
    vo_ref[...] = v
    ones = jnp.ones((WINDOW + nt, LANES), BF16)
    for b in range(nb):
        rows = slice(b * nt, (b + 1) * nt)
        kall = _dup_heads(jnp.concatenate([ck_ref[b], k[rows]], axis=0))
        vall = _dup_heads(jnp.concatenate([cv_ref[b], v[rows]], axis=0))
        for j in range(N_KV_HEADS):
            kj = kall[j].astype(BF16)
            vj = vall[j].astype(BF16)
            qg = q[rows, j * GQ_W:(j + 1) * GQ_W]
            qbd = jnp.concatenate([qg] * GROUP, axis=0) * bd_ref[...]
            on = _band_attention(qbd, jnp.concatenate([kj, kj], axis=1),
                                 jnp.concatenate([vj, ones], axis=1), bias_ref[j], sink_ref[j])
            lo, hi = _ungroup(on, nt)
            attn_ref[rows, j * GQ_W:j * GQ_W + LANES] = lo
            attn_ref[rows, j * GQ_W + LANES:(j + 1) * GQ_W] = hi
    za = attn_ref[...] * _silu(ga)

    p2 = _dot(xn, win_ref[:, OFF_BB:OFF_XQ])
    bb = p2[:, 0:CONV_W]
    u = p2[:, CONV_W:2 * CONV_W] * p2[:, 2 * CONV_W:3 * CONV_W]
    gb = p2[:, 3 * CONV_W:4 * CONV_W]
    for b in range(nb):
        rows = slice(b * nt, (b + 1) * nt)
        ub = u[rows]
        uc_ref[b, CONV_PAD - (CONV_K - 1):CONV_PAD, :] = cc_ref[b]
        uc_ref[b, CONV_PAD:CONV_PAD + nt, :] = ub
        uo_ref[b] = ub[nt - (CONV_K - 1):nt]
        conv_ref[rows, :] = (uc_ref[b, CONV_PAD - 2:CONV_PAD - 2 + nt, :] * cw_ref[0:1, :]
                             + uc_ref[b, CONV_PAD - 1:CONV_PAD - 1 + nt, :] * cw_ref[1:2, :]
                             + ub * cw_ref[2:3, :]) + cb_ref[...]
    zb = bb * conv_ref[...] * _silu(gb)

    p3 = _dot(xn, win_ref[:, OFF_XQ:OFF_MG])
    xq = (p3[:, 0:X_W] * (X_HEAD_DIM ** -0.5)).astype(BF16)
    gx = p3[:, X_W:2 * X_W]
    mones = jnp.ones((N_MEM, X_HEAD_DIM), BF16)
    for b in range(nb):
        rows = slice(b * nt, (b + 1) * nt)
        for h in range(X_HEADS):
            cols = slice(h * X_HEAD_DIM, (h + 1) * X_HEAD_DIM)
            mvx = jnp.concatenate([cmv_ref[b, :, cols].astype(BF16), mones], axis=1)
            xat_ref[rows, cols] = _cross_head(xq[rows, cols], cmk_ref[b, :, cols].astype(BF16), mvx)
    zx = xat_ref[...] * _silu(gx)

    out = _merge_out(x, xn, (za, zb, zx), win_ref, (wpa_ref, wpb_ref, wpx_ref), wout_ref)
    if final:
        out = _rms_scale(out) * fg_ref[...]
    xo_ref[...] = out


def _sample_layer(x, g, win, wpa, wpb, wpx, wout, cw, cb, sink_col, bias, bd, ck, cv, cc, cmk, cmv, fg,
                  *, nb, nt, final):
    rows = nb * nt
    kern = functools.partial(_sample_kernel, nb=nb, nt=nt, final=final)
    return pl.pallas_call(
        kern,
        out_shape=[
            jax.ShapeDtypeStruct((rows, D_MODEL), F32),
            jax.ShapeDtypeStruct((rows, KV_W), F32),
            jax.ShapeDtypeStruct((rows, KV_W), F32),
            jax.ShapeDtypeStruct((nb, CONV_K - 1, CONV_W), F32),
        ],
        scratch_shapes=[
            pltpu.VMEM((nb, CONV_PAD + nt, CONV_W), F32),
            pltpu.VMEM((rows, ATT_W), F32),
            pltpu.VMEM((rows, CONV_W), F32),
            pltpu.VMEM((rows, X_W), F32),
        ],
        compiler_params=pltpu.CompilerParams(vmem_limit_bytes=VMEM_LIMIT),
        name="sample_layer",
    )(x, g, win, wpa, wpb, wpx, wout, cw, cb, sink_col, bias, bd, ck, cv, cc, cmk, cmv, fg)


def _alibi_slopes():
    return np.power(2.0, -8.0 * np.arange(1, N_HEADS + 1, dtype=np.float32) / N_HEADS).astype(np.float32)


def _alibi_bias(nq, nk, masked):
    dist = np.abs(np.arange(nq)[:, None] + WINDOW - np.arange(nk)[None, :]).astype(np.float32)
    slope = _alibi_slopes().reshape(N_KV_HEADS, GROUP)
    bias = -(slope[:, :, None, None] * dist[None, None])
    bias = np.where(np.arange(nk)[None, None, None, :] < masked, -np.inf, bias)
    return bias.reshape(N_KV_HEADS, GROUP * nq, nk).astype(np.float32)


def _block_diag_mask(nq):
    rows = np.arange(GROUP * nq)[:, None] // nq
    cols = np.arange(GQ_W)[None, :] // HEAD_DIM
    return (rows == cols).astype(np.float32)


def kernel(x_prompt, x_sample, cache_attn_k, cache_attn_v, cache_conv, cache_mem_k, cache_mem_v,
           mem_prompt, norm_g, w_in, attn_sink, w_pa, conv_w, conv_b, w_pb, mem_norm_g,
           w_mk, w_mv, w_px, w_out, final_g):
    bp, seq, _ = x_prompt.shape
    bs, nt, _ = x_sample.shape
    win_len = cache_attn_k.shape[2]
    assert win_len == WINDOW and nt <= CHUNK and seq >= WINDOW

    w_in_b, w_pa_b, w_pb_b, w_px_b, w_out_b = (w.astype(BF16) for w in (w_in, w_pa, w_pb, w_px, w_out))
    w_mk_b, w_mv_b = w_mk.astype(BF16), w_mv.astype(BF16)

    bias_p = jnp.asarray(np.stack([_alibi_bias(CHUNK, BAND, m) for m in (WINDOW, CHUNK, 0)]))
    bias_s = jnp.asarray(_alibi_bias(nt, WINDOW + nt, 0))
    bd_p = jnp.asarray(_block_diag_mask(CHUNK), BF16)
    bd_s = jnp.asarray(_block_diag_mask(nt), BF16)
    sink_g = attn_sink.reshape(DEPTH, N_KV_HEADS, GROUP, 1)
    sink_p = jnp.broadcast_to(sink_g, (DEPTH, N_KV_HEADS, GROUP, CHUNK)).reshape(DEPTH, N_KV_HEADS, GROUP * CHUNK, 1)
    sink_s = jnp.broadcast_to(sink_g, (DEPTH, N_KV_HEADS, GROUP, nt)).reshape(DEPTH, N_KV_HEADS, GROUP * nt, 1)
    fg = final_g.reshape(1, D_MODEL)

    mk, mv, mkb, mvx = _memory_kv(mem_prompt, mem_norm_g, w_mk_b, w_mv_b)

    xp = x_prompt
    xs = x_sample.reshape(bs * nt, D_MODEL)
    ck = cache_attn_k.reshape(DEPTH, bs, WINDOW, KV_W)
    cv = cache_attn_v.reshape(DEPTH, bs, WINDOW, KV_W)
    cmk = cache_mem_k.reshape(DEPTH, bs, N_MEM, X_W)
    cmv = cache_mem_v.reshape(DEPTH, bs, N_MEM, X_W)
    pk, pv, pc, sk, sv, sc = [], [], [], [], [], []
    for l in range(DEPTH):
        final = l == DEPTH - 1
        shared = (norm_g[l].reshape(1, D_MODEL), w_in_b[l], w_pa_b[l], w_pb_b[l], w_px_b[l], w_out_b[l],
                  conv_w[l], conv_b[l].reshape(1, CONV_W))
        xp, kl, vl, ul = _prompt_layer(xp, *shared, sink_p[l], bias_p, bd_p, mkb[l], mvx[l], fg, final=final)
        pk.append(kl)
        pv.append(vl)
        pc.append(ul)
        xs, ko, vo, uo = _sample_layer(xs, *shared, sink_s[l], bias_s, bd_s, ck[l], cv[l], cache_conv[l],
                                       cmk[l], cmv[l], fg, nb=bs, nt=nt, final=final)
        sk.append(ko)
        sv.append(vo)
        sc.append(uo)

    kv_p = (DEPTH, bp, WINDOW, N_KV_HEADS, HEAD_DIM)
    kv_s = (DEPTH, bs, nt, N_KV_HEADS, HEAD_DIM)
    mem_shape = (DEPTH, bp, N_MEM, X_HEADS, X_HEAD_DIM)
    return (xp, xs.reshape(bs, nt, D_MODEL),
            jnp.stack(pk).reshape(kv_p), jnp.stack(pv).reshape(kv_p), jnp.stack(pc),
            mk.reshape(mem_shape), mv.reshape(mem_shape),
            jnp.stack(sk).reshape(kv_s), jnp.stack(sv).reshape(kv_s), jnp.stack(sc))
```

```python
import functools

import numpy as np
import jax
import jax.numpy as jnp
from jax import lax
from jax.experimental import pallas as pl
from jax.experimental.pallas import tpu as pltpu

D_MODEL = 1024
DEPTH = 4
CHUNK = 64
WINDOW = 128
N_HEADS = 8
N_KV_HEADS = 2
GROUP = N_HEADS // N_KV_HEADS
HEAD_DIM = 64
ATT_W = N_HEADS * HEAD_DIM
KV_W = N_KV_HEADS * HEAD_DIM
CONV_W = 512
CONV_K = 3
X_HEADS = 4
X_HEAD_DIM = 128
X_W = X_HEADS * X_HEAD_DIM
N_MEM = 256
EPS = 1e-6

OFF_Q = 0
OFF_K = OFF_Q + ATT_W
OFF_V = OFF_K + KV_W
OFF_GA = OFF_V + KV_W
OFF_BB = OFF_GA + ATT_W
OFF_CC = OFF_BB + CONV_W
OFF_HH = OFF_CC + CONV_W
OFF_GB = OFF_HH + CONV_W
OFF_XQ = OFF_GB + CONV_W
OFF_GX = OFF_XQ + X_W
OFF_MG = OFF_GX + X_W
IN_W = OFF_MG + 3 * D_MODEL

LANES = 128
GQ_W = GROUP * HEAD_DIM
BAND = WINDOW + CHUNK
CONV_PAD = 8
PROMPT_TILE = 512
VMEM_LIMIT = 56 * 1024 * 1024

F32 = jnp.float32
BF16 = jnp.bfloat16
_NT = (((1,), (1,)), ((), ()))


def _rms_scale(x):
    return x * lax.rsqrt(jnp.mean(x * x, axis=-1, keepdims=True) + EPS)


def _silu(x):
    return x * jax.nn.sigmoid(x)


def _dot(a, b):
    return jnp.dot(a, b, preferred_element_type=F32)


def _dup_heads(a):
    lane = lax.broadcasted_iota(jnp.int32, a.shape, 1)
    sw = pltpu.roll(a, HEAD_DIM, 1)
    lo = lane < HEAD_DIM
    return jnp.where(lo, a, sw), jnp.where(lo, sw, a)


def _band_attention(qbd, kb, vb, bias, sink_col):
    s = lax.dot_general(qbd, kb, _NT, preferred_element_type=F32) + bias
    m = jnp.maximum(jnp.max(s, axis=-1, keepdims=True), sink_col)
    p = jnp.exp(s - m).astype(BF16)
    oe = _dot(p, vb)
    den = oe[:, LANES:2 * LANES] + jnp.exp(sink_col - m)
    return oe[:, 0:LANES] / den


def _ungroup(on, nq):
    lo = lax.broadcasted_iota(jnp.int32, (nq, LANES), 1) < HEAD_DIM
    return (jnp.where(lo, on[0:nq], on[nq:2 * nq]),
            jnp.where(lo, on[2 * nq:3 * nq], on[3 * nq:4 * nq]))


def _cross_head(xq_h, mk_h, mvx_h):
    s = lax.dot_general(xq_h, mk_h, _NT, preferred_element_type=F32)
    p = jnp.exp(s - jnp.max(s, axis=-1, keepdims=True)).astype(BF16)
    oe = _dot(p, mvx_h)
    return oe[:, 0:X_HEAD_DIM] / oe[:, X_HEAD_DIM:2 * X_HEAD_DIM]


def _merge_out(x, xn, z_terms, win_ref, wp_refs, wout_ref):
    acc = None
    for i, (z, wp_ref) in enumerate(zip(z_terms, wp_refs)):
        gate = jax.nn.sigmoid(_dot(xn, win_ref[:, OFF_MG + i * D_MODEL:OFF_MG + (i + 1) * D_MODEL]))
        term = gate * _dot(z.astype(BF16), wp_ref[...])
        acc = term if acc is None else acc + term
    return x + _dot(acc.astype(BF16), wout_ref[...])


def _memkv_kernel(mem_ref, g_ref, wk_ref, wv_ref, mk_ref, mv_ref, mkb_ref, mvx_ref):
    mn = (_rms_scale(mem_ref[0]) * g_ref[0]).astype(BF16)
    mk = _dot(mn, wk_ref[0].astype(BF16))
    mv = _dot(mn, wv_ref[0].astype(BF16))
    mkb_ref[0, 0] = mk.astype(BF16)
    ones = jnp.ones((N_MEM, X_HEAD_DIM), BF16)
    for h in range(X_HEADS):
        cols = slice(h * X_HEAD_DIM, (h + 1) * X_HEAD_DIM)
        mk_ref[0, 0, pl.ds(h, N_MEM, stride=X_HEADS), :] = mk[:, cols]
        mv_ref[0, 0, pl.ds(h, N_MEM, stride=X_HEADS), :] = mv[:, cols]
        mvx_ref[0, 0, h, :, 0:X_HEAD_DIM] = mv[:, cols].astype(BF16)
        mvx_ref[0, 0, h, :, X_HEAD_DIM:2 * X_HEAD_DIM] = ones


def _memory_kv(mem, g, wk, wv):
    nb = mem.shape[0]
    return pl.pallas_call(
        _memkv_kernel,
        grid=(DEPTH, nb),
        in_specs=[
            pl.BlockSpec((1, N_MEM, D_MODEL), lambda l, b: (b, 0, 0)),
            pl.BlockSpec((1, 1, D_MODEL), lambda l, b: (l, 0, 0)),
            pl.BlockSpec((1, D_MODEL, X_W), lambda l, b: (l, 0, 0)),
            pl.BlockSpec((1, D_MODEL, X_W), lambda l, b: (l, 0, 0)),
        ],
        out_specs=[
            pl.BlockSpec((1, 1, N_MEM * X_HEADS, X_HEAD_DIM), lambda l, b: (l, b, 0, 0)),
            pl.BlockSpec((1, 1, N_MEM * X_HEADS, X_HEAD_DIM), lambda l, b: (l, b, 0, 0)),
            pl.BlockSpec((1, 1, N_MEM, X_W), lambda l, b: (l, b, 0, 0)),
            pl.BlockSpec((1, 1, X_HEADS, N_MEM, 2 * X_HEAD_DIM), lambda l, b: (l, b, 0, 0, 0)),
        ],
        out_shape=[
            jax.ShapeDtypeStruct((DEPTH, nb, N_MEM * X_HEADS, X_HEAD_DIM), F32),
            jax.ShapeDtypeStruct((DEPTH, nb, N_MEM * X_HEADS, X_HEAD_DIM), F32),
            jax.ShapeDtypeStruct((DEPTH, nb, N_MEM, X_W), BF16),
            jax.ShapeDtypeStruct((DEPTH, nb, X_HEADS, N_MEM, 2 * X_HEAD_DIM), BF16),
        ],
        name="memory_kv",
    )(mem, g.reshape(DEPTH, 1, D_MODEL), wk, wv)


def _prompt_kernel(layer_ref, x_ref, g_ref, win_ref, wpa_ref, wpb_ref, wpx_ref, wout_ref, cw_ref, cb_ref,
                   sink_ref, bias_ref, bd_ref, mk_ref, mvx_ref, fg_ref,
                   xo_ref, kl_ref, vl_ref, ul_ref,
                   kr_ref, vv_ref, uc_ref, q_ref, attn_ref, *, tile, final):
    t = pl.program_id(1)

    @pl.when(t == 0)
    def _():
        kr_ref[:, 0:WINDOW, :] = jnp.zeros((N_KV_HEADS, WINDOW, GQ_W), BF16)
        vv_ref[:, 0:WINDOW, 0:LANES] = jnp.zeros((N_KV_HEADS, WINDOW, LANES), BF16)
        vv_ref[:, :, LANES:2 * LANES] = jnp.ones((N_KV_HEADS, WINDOW + tile, LANES), BF16)
        uc_ref[0:CONV_PAD, :] = jnp.zeros((CONV_PAD, CONV_W), F32)

    x = x_ref[0]
    xn = (_rms_scale(x) * g_ref[...]).astype(BF16)

    p1 = _dot(xn, win_ref[:, OFF_Q:OFF_BB])
    q_ref[...] = (p1[:, OFF_Q:OFF_K] * (HEAD_DIM ** -0.5)).astype(BF16)
    k = p1[:, OFF_K:OFF_V]
    v = p1[:, OFF_V:OFF_GA]
    ga = p1[:, OFF_GA:OFF_BB]
    kl_ref[0] = k[tile - WINDOW:tile]
    vl_ref[0] = v[tile - WINDOW:tile]
    for j, (kj, vj) in enumerate(zip(_dup_heads(k), _dup_heads(v))):
        kj = kj.astype(BF16)
        kr_ref[j, WINDOW:WINDOW + tile, 0:LANES] = kj
        kr_ref[j, WINDOW:WINDOW + tile, LANES:2 * LANES] = kj
        vv_ref[j, WINDOW:WINDOW + tile, 0:LANES] = vj.astype(BF16)

    def proj(off, width):
        return _dot(xn, win_ref[:, off:off + width])

    fillers = [functools.partial(proj, off, CONV_W) for off in (OFF_BB, OFF_CC, OFF_HH, OFF_GB)]
    fillers += [functools.partial(proj, off, X_W) for off in (OFF_XQ, OFF_GX)]
    fillers += [lambda i=i: jax.nn.sigmoid(proj(OFF_MG + i * D_MODEL, D_MODEL)) for i in (1, 2)]
    filled = []
    n_chunks = tile // CHUNK

    for c in range(n_chunks):
        r0 = c * CHUNK
        qc = q_ref[r0:r0 + CHUNK, :]
        variant = jnp.where(t == 0, c, 2) if c < 2 else 2
        for j in range(N_KV_HEADS):
            qg = qc[:, j * GQ_W:(j + 1) * GQ_W]
            qbd = jnp.concatenate([qg] * GROUP, axis=0) * bd_ref[...]
            on = _band_attention(qbd, kr_ref[j, r0:r0 + BAND, :], vv_ref[j, r0:r0 + BAND, :],
                                 bias_ref[variant, j], sink_ref[j])
            lo, hi = _ungroup(on, CHUNK)
            attn_ref[r0:r0 + CHUNK, j * GQ_W:j * GQ_W + LANES] = lo
            attn_ref[r0:r0 + CHUNK, j * GQ_W + LANES:(j + 1) * GQ_W] = hi
        while len(filled) * n_chunks < (c + 1) * len(fillers):
            filled.append(fillers[len(filled)]())
    bb, cc, hh, gb, xq, gx, gate_b, gate_x = filled
    kr_ref[:, 0:WINDOW, :] = kr_ref[:, tile:tile + WINDOW, :]
    vv_ref[:, 0:WINDOW, 0:LANES] = vv_ref[:, tile:tile + WINDOW, 0:LANES]

    u = cc * hh
    uc_ref[CONV_PAD:CONV_PAD + tile, :] = u
    ul_ref[0] = u[tile - (CONV_K - 1):tile]
    conv = (uc_ref[CONV_PAD - 2:CONV_PAD - 2 + tile, :] * cw_ref[0:1, :]
            + uc_ref[CONV_PAD - 1:CONV_PAD - 1 + tile, :] * cw_ref[1:2, :]
            + u * cw_ref[2:3, :]) + cb_ref[...]
    uc_ref[0:CONV_PAD, :] = uc_ref[tile:tile + CONV_PAD, :]
    zb = (bb * conv * _silu(gb)).astype(BF16)
    za = (attn_ref[...] * _silu(ga)).astype(BF16)

    xq = (xq * (X_HEAD_DIM ** -0.5)).astype(BF16)
    merge_work = [lambda: _dot(za, wpa_ref[...]),
                  lambda: jax.nn.sigmoid(proj(OFF_MG, D_MODEL)),
                  lambda: _dot(zb, wpb_ref[...])]
    merged = []
    heads = []
    for h in range(X_HEADS):
        heads.append(_cross_head(xq[:, h * X_HEAD_DIM:(h + 1) * X_HEAD_DIM],
                                 mk_ref[0, :, h * X_HEAD_DIM:(h + 1) * X_HEAD_DIM], mvx_ref[0, h]))
        if h < len(merge_work):
            merged.append(merge_work[h]())
    ya, gate_a, yb = merged
    zx = (jnp.concatenate(heads, axis=1) * _silu(gx)).astype(BF16)
    acc = gate_a * ya + gate_b * yb + gate_x * _dot(zx, wpx_ref[...])
    out = x + _dot(acc.astype(BF16), wout_ref[...])
    if final:
        out = _rms_scale(out) * fg_ref[...]
    xo_ref[0] = out


def _const_spec(shape):
    nd = len(shape)
    return pl.BlockSpec(shape, lambda *_: (0,) * nd, pipeline_mode=pl.Buffered(1))


def _layer_spec(shape):
    nd = len(shape)
    return pl.BlockSpec((None,) + tuple(shape), lambda *a: (a[-1][0],) + (0,) * nd, pipeline_mode=pl.Buffered(1))


def _prompt_layer(layer, x, g, win, wpa, wpb, wpx, wout, cw, cb, sink_col, bias, bd, mkb, mvx, fg, *, final):
    nb, seq, _ = x.shape
    tile = PROMPT_TILE
    assert seq % tile == 0 and tile % WINDOW == 0
    kern = functools.partial(_prompt_kernel, tile=tile, final=final)
    grid_spec = pltpu.PrefetchScalarGridSpec(
        num_scalar_prefetch=1,
        grid=(nb, seq // tile),
        in_specs=[
            pl.BlockSpec((1, tile, D_MODEL), lambda b, t, l: (b, t, 0)),
            _layer_spec((1, D_MODEL)),
            _layer_spec((D_MODEL, IN_W)),
            _layer_spec((ATT_W, D_MODEL)),
            _layer_spec((CONV_W, D_MODEL)),
            _layer_spec((X_W, D_MODEL)),
            _layer_spec((D_MODEL, D_MODEL)),
            _layer_spec((CONV_K, CONV_W)),
            _layer_spec((1, CONV_W)),
            _layer_spec((N_KV_HEADS, GROUP * CHUNK, 1)),
            _const_spec((3, N_KV_HEADS, GROUP * CHUNK, BAND)),
            _const_spec((GROUP * CHUNK, GQ_W)),
            pl.BlockSpec((None, 1, N_MEM, X_W), lambda b, t, l: (l[0], b, 0, 0)),
            pl.BlockSpec((None, 1, X_HEADS, N_MEM, 2 * X_HEAD_DIM), lambda b, t, l: (l[0], b, 0, 0, 0)),
            _const_spec((1, D_MODEL)),
        ],
        out_specs=[
            pl.BlockSpec((1, tile, D_MODEL), lambda b, t, l: (b, t, 0)),
            pl.BlockSpec((1, WINDOW, KV_W), lambda b, t, l: (b, 0, 0)),
            pl.BlockSpec((1, WINDOW, KV_W), lambda b, t, l: (b, 0, 0)),
            pl.BlockSpec((1, CONV_K - 1, CONV_W), lambda b, t, l: (b, 0, 0)),
        ],
        scratch_shapes=[
            pltpu.VMEM((N_KV_HEADS, WINDOW + tile, GQ_W), BF16),
            pltpu.VMEM((N_KV_HEADS, WINDOW + tile, 2 * LANES), BF16),
            pltpu.VMEM((CONV_PAD + tile, CONV_W), F32),
            pltpu.VMEM((tile, ATT_W), BF16),
            pltpu.VMEM((tile, ATT_W), F32),
        ],
    )
    return pl.pallas_call(
        kern,
        grid_spec=grid_spec,
        out_shape=[
            jax.ShapeDtypeStruct((nb, seq, D_MODEL), F32),
            jax.ShapeDtypeStruct((nb, WINDOW, KV_W), F32),
            jax.ShapeDtypeStruct((nb, WINDOW, KV_W), F32),
            jax.ShapeDtypeStruct((nb, CONV_K - 1, CONV_W), F32),
        ],
        compiler_params=pltpu.CompilerParams(
            dimension_semantics=("arbitrary", "arbitrary"), vmem_limit_bytes=VMEM_LIMIT),
        name="prompt_layer",
    )(layer, x, g, win, wpa, wpb, wpx, wout, cw, cb, sink_col, bias, bd, mkb, mvx, fg)


def _sample_kernel(layer_ref, x_ref, g_ref, win_ref, wpa_ref, wpb_ref, wpx_ref, wout_ref, cw_ref, cb_ref,
                   sink_ref, bias_ref, bd_ref, ck_ref, cv_ref, cc_ref, cmk_ref, cmv_ref, fg_ref,
                   xo_ref, ko_ref, vo_ref, uo_ref,
                   uc_ref, attn_ref, conv_ref, xat_ref, *, nb, nt, final):
    x = x_ref[...]
    xn = (_rms_scale(x) * g_ref[...]).astype(BF16)

    p1 = _dot(xn, win_ref[:, OFF_Q:OFF_BB])
    q = (p1[:, OFF_Q:OFF_K] * (HEAD_DIM ** -0.5)).astype(BF16)
    k = p1[:, OFF_K:OFF_V]
    v = p1[:, OFF_V:OFF_GA]
    ga = p1[:, OFF_GA:OFF_BB]
    ko_ref[...] = k
    vo_ref[...] = v
    ones = jnp.ones((WINDOW + nt, LANES), BF16)
    for b in range(nb):
        rows = slice(b * nt, (b + 1) * nt)
        kall = _dup_heads(jnp.concatenate([ck_ref[b], k[rows]], axis=0))
        vall = _dup_heads(jnp.concatenate([cv_ref[b], v[rows]], axis=0))
        for j in range(N_KV_HEADS):
            kj = kall[j].astype(BF16)
            vj = vall[j].astype(BF16)
            qg = q[rows, j * GQ_W:(j + 1) * GQ_W]
            qbd = jnp.concatenate([qg] * GROUP, axis=0) * bd_ref[...]
            on = _band_attention(qbd, jnp.concatenate([kj, kj], axis=1),
                                 jnp.concatenate([vj, ones], axis=1), bias_ref[j], sink_ref[j])
            lo, hi = _ungroup(on, nt)
            attn_ref[rows, j * GQ_W:j * GQ_W + LANES] = lo
            attn_ref[rows, j * GQ_W + LANES:(j + 1) * GQ_W] = hi
    za = attn_ref[...] * _silu(ga)

    p2 = _dot(xn, win_ref[:, OFF_BB:OFF_XQ])
    bb = p2[:, 0:CONV_W]
    u = p2[:, CONV_W:2 * CONV_W] * p2[:, 2 * CONV_W:3 * CONV_W]
    gb = p2[:, 3 * CONV_W:4 * CONV_W]
    for b in range(nb):
        rows = slice(b * nt, (b + 1) * nt)
        ub = u[rows]
        uc_ref[b, CONV_PAD - (CONV_K - 1):CONV_PAD, :] = cc_ref[b]
        uc_ref[b, CONV_PAD:CONV_PAD + nt, :] = ub
        uo_ref[b] = ub[nt - (CONV_K - 1):nt]
        conv_ref[rows, :] = (uc_ref[b, CONV_PAD - 2:CONV_PAD - 2 + nt, :] * cw_ref[0:1, :]
                             + uc_ref[b, CONV_PAD - 1:CONV_PAD - 1 + nt, :] * cw_ref[1:2, :]
                             + ub * cw_ref[2:3, :]) + cb_ref[...]
    zb = bb * conv_ref[...] * _silu(gb)

    p3 = _dot(xn, win_ref[:, OFF_XQ:OFF_MG])
    xq = (p3[:, 0:X_W] * (X_HEAD_DIM ** -0.5)).astype(BF16)
    gx = p3[:, X_W:2 * X_W]
    mones = jnp.ones((N_MEM, X_HEAD_DIM), BF16)
    for b in range(nb):
        rows = slice(b * nt, (b + 1) * nt)
        for h in range(X_HEADS):
            cols = slice(h * X_HEAD_DIM, (h + 1) * X_HEAD_DIM)
            head_rows = pl.ds(h, N_MEM, stride=X_HEADS)
            mvx = jnp.concatenate([cmv_ref[b, head_rows, :].astype(BF16), mones], axis=1)
            xat_ref[rows, cols] = _cross_head(xq[rows, cols], cmk_ref[b, head_rows, :].astype(BF16), mvx)
    zx = xat_ref[...] * _silu(gx)

    out = _merge_out(x, xn, (za, zb, zx), win_ref, (wpa_ref, wpb_ref, wpx_ref), wout_ref)
    if final:
        out = _rms_scale(out) * fg_ref[...]
    xo_ref[...] = out


def _sample_layer(layer, x, g, win, wpa, wpb, wpx, wout, cw, cb, sink_col, bias, bd, ck, cv, cc, cmk, cmv, fg,
                  *, nb, nt, final):
    rows = nb * nt
    kern = functools.partial(_sample_kernel, nb=nb, nt=nt, final=final)
    grid_spec = pltpu.PrefetchScalarGridSpec(
        num_scalar_prefetch=1,
        grid=(1,),
        in_specs=[
            _const_spec((rows, D_MODEL)),
            _layer_spec((1, D_MODEL)),
            _layer_spec((D_MODEL, IN_W)),
            _layer_spec((ATT_W, D_MODEL)),
            _layer_spec((CONV_W, D_MODEL)),
            _layer_spec((X_W, D_MODEL)),
            _layer_spec((D_MODEL, D_MODEL)),
            _layer_spec((CONV_K, CONV_W)),
            _layer_spec((1, CONV_W)),
            _layer_spec((N_KV_HEADS, GROUP * nt, 1)),
            _const_spec((N_KV_HEADS, GROUP * nt, WINDOW + nt)),
            _const_spec((GROUP * nt, GQ_W)),
            _layer_spec((nb, WINDOW, KV_W)),
            _layer_spec((nb, WINDOW, KV_W)),
            _layer_spec((nb, CONV_K - 1, CONV_W)),
            _layer_spec((nb, N_MEM * X_HEADS, X_HEAD_DIM)),
            _layer_spec((nb, N_MEM * X_HEADS, X_HEAD_DIM)),
            _const_spec((1, D_MODEL)),
        ],
        out_specs=[
            pl.BlockSpec((rows, D_MODEL), lambda i, l: (0, 0)),
            pl.BlockSpec((rows, KV_W), lambda i, l: (0, 0)),
            pl.BlockSpec((rows, KV_W), lambda i, l: (0, 0)),
            pl.BlockSpec((nb, CONV_K - 1, CONV_W), lambda i, l: (0, 0, 0)),
        ],
        scratch_shapes=[
            pltpu.VMEM((nb, CONV_PAD + nt, CONV_W), F32),
            pltpu.VMEM((rows, ATT_W), F32),
            pltpu.VMEM((rows, CONV_W), F32),
            pltpu.VMEM((rows, X_W), F32),
        ],
    )
    return pl.pallas_call(
        kern,
        grid_spec=grid_spec,
        out_shape=[
            jax.ShapeDtypeStruct((rows, D_MODEL), F32),
            jax.ShapeDtypeStruct((rows, KV_W), F32),
            jax.ShapeDtypeStruct((rows, KV_W), F32),
            jax.ShapeDtypeStruct((nb, CONV_K - 1, CONV_W), F32),
        ],
        compiler_params=pltpu.CompilerParams(
            dimension_semantics=("arbitrary",), vmem_limit_bytes=VMEM_LIMIT),
        name="sample_layer",
    )(layer, x, g, win, wpa, wpb, wpx, wout, cw, cb, sink_col, bias, bd, ck, cv, cc, cmk, cmv, fg)


def _alibi_slopes():
    return np.power(2.0, -8.0 * np.arange(1, N_HEADS + 1, dtype=np.float32) / N_HEADS).astype(np.float32)


def _alibi_bias(nq, nk, masked):
    dist = np.abs(np.arange(nq)[:, None] + WINDOW - np.arange(nk)[None, :]).astype(np.float32)
    slope = _alibi_slopes().reshape(N_KV_HEADS, GROUP)
    bias = -(slope[:, :, None, None] * dist[None, None])
    bias = np.where(np.arange(nk)[None, None, None, :] < masked, -np.inf, bias)
    return bias.reshape(N_KV_HEADS, GROUP * nq, nk).astype(np.float32)


def _block_diag_mask(nq):
    rows = np.arange(GROUP * nq)[:, None] // nq
    cols = np.arange(GQ_W)[None, :] // HEAD_DIM
    return (rows == cols).astype(np.float32)


def kernel(x_prompt, x_sample, cache_attn_k, cache_attn_v, cache_conv, cache_mem_k, cache_mem_v,
           mem_prompt, norm_g, w_in, attn_sink, w_pa, conv_w, conv_b, w_pb, mem_norm_g,
           w_mk, w_mv, w_px, w_out, final_g):
    bp, seq, _ = x_prompt.shape
    bs, nt, _ = x_sample.shape
    win_len = cache_attn_k.shape[2]
    assert win_len == WINDOW and nt <= CHUNK and seq >= WINDOW

    w_in_b, w_pa_b, w_pb_b, w_px_b, w_out_b = (w.astype(BF16) for w in (w_in, w_pa, w_pb, w_px, w_out))

    bias_p = jnp.asarray(np.stack([_alibi_bias(CHUNK, BAND, m) for m in (WINDOW, CHUNK, 0)]))
    bias_s = jnp.asarray(_alibi_bias(nt, WINDOW + nt, 0))
    bd_p = jnp.asarray(_block_diag_mask(CHUNK), BF16)
    bd_s = jnp.asarray(_block_diag_mask(nt), BF16)
    sink_g = attn_sink.reshape(DEPTH, N_KV_HEADS, GROUP, 1)
    sink_p = jnp.broadcast_to(sink_g, (DEPTH, N_KV_HEADS, GROUP, CHUNK)).reshape(DEPTH, N_KV_HEADS, GROUP * CHUNK, 1)
    sink_s = jnp.broadcast_to(sink_g, (DEPTH, N_KV_HEADS, GROUP, nt)).reshape(DEPTH, N_KV_HEADS, GROUP * nt, 1)
    fg = final_g.reshape(1, D_MODEL)

    mk, mv, mkb, mvx = _memory_kv(mem_prompt, mem_norm_g, w_mk, w_mv)

    xp = x_prompt
    xs = x_sample.reshape(bs * nt, D_MODEL)
    ck = cache_attn_k.reshape(DEPTH, bs, WINDOW, KV_W)
    cv = cache_attn_v.reshape(DEPTH, bs, WINDOW, KV_W)
    shared = (norm_g.reshape(DEPTH, 1, D_MODEL), w_in_b, w_pa_b, w_pb_b, w_px_b, w_out_b,
              conv_w, conv_b.reshape(DEPTH, 1, CONV_W))
    cmk = cache_mem_k.reshape(DEPTH, bs, N_MEM * X_HEADS, X_HEAD_DIM)
    cmv = cache_mem_v.reshape(DEPTH, bs, N_MEM * X_HEADS, X_HEAD_DIM)
    pk, pv, pc, sk, sv, sc = [], [], [], [], [], []
    for l in range(DEPTH):
        final = l == DEPTH - 1
        layer = jnp.full((1,), l, jnp.int32)
        xp, kl, vl, ul = _prompt_layer(layer, xp, *shared, sink_p, bias_p, bd_p, mkb, mvx, fg, final=final)
        pk.append(kl)
        pv.append(vl)
        pc.append(ul)
        xs, ko, vo, uo = _sample_layer(layer, xs, *shared, sink_s, bias_s, bd_s, ck, cv, cache_conv,
                                       cmk, cmv, fg, nb=bs, nt=nt, final=final)
        sk.append(ko)
        sv.append(vo)
        sc.append(uo)

    kv_p = (DEPTH, bp, WINDOW, N_KV_HEADS, HEAD_DIM)
    kv_s = (DEPTH, bs, nt, N_KV_HEADS, HEAD_DIM)
    mem_shape = (DEPTH, bp, N_MEM, X_HEADS, X_HEAD_DIM)
    return (xp, xs.reshape(bs, nt, D_MODEL),
            jnp.stack(pk).reshape(kv_p), jnp.stack(pv).reshape(kv_p), jnp.stack(pc),
            mk.reshape(mem_shape), mv.reshape(mem_shape),
            jnp.stack(sk).reshape(kv_s), jnp.stack(sv).reshape(kv_s), jnp.stack(sc))
```

```python
import functools

import numpy as np
import jax
import jax.numpy as jnp
from jax import lax
from jax.experimental import pallas as pl
from jax.experimental.pallas import tpu as pltpu

D_MODEL = 1024
DEPTH = 4
CHUNK = 64
WINDOW = 128
N_HEADS = 8
N_KV_HEADS = 2
GROUP = N_HEADS // N_KV_HEADS
HEAD_DIM = 64
ATT_W = N_HEADS * HEAD_DIM
KV_W = N_KV_HEADS * HEAD_DIM
CONV_W = 512
CONV_K = 3
X_HEADS = 4
X_HEAD_DIM = 128
X_W = X_HEADS * X_HEAD_DIM
N_MEM = 256
EPS = 1e-6

OFF_Q = 0
OFF_K = OFF_Q + ATT_W
OFF_V = OFF_K + KV_W
OFF_GA = OFF_V + KV_W
OFF_BB = OFF_GA + ATT_W
OFF_CC = OFF_BB + CONV_W
OFF_HH = OFF_CC + CONV_W
OFF_GB = OFF_HH + CONV_W
OFF_XQ = OFF_GB + CONV_W
OFF_GX = OFF_XQ + X_W
OFF_MG = OFF_GX + X_W
IN_W = OFF_MG + 3 * D_MODEL

LANES = 128
GQ_W = GROUP * HEAD_DIM
BAND = WINDOW + CHUNK
KEY_SLOTS = 256
CONV_PAD = 8
PROMPT_TILE = 512
VMEM_LIMIT = 56 * 1024 * 1024

F32 = jnp.float32
BF16 = jnp.bfloat16
_NT = (((1,), (1,)), ((), ()))


def _rms_scale(x):
    return x * lax.rsqrt(jnp.mean(x * x, axis=-1, keepdims=True) + EPS)


def _silu(x):
    return x * jax.nn.sigmoid(x)


def _dot(a, b):
    return jnp.dot(a, b, preferred_element_type=F32)


def _dup_heads(a):
    lane = lax.broadcasted_iota(jnp.int32, a.shape, 1)
    sw = pltpu.roll(a, HEAD_DIM, 1)
    lo = lane < HEAD_DIM
    return jnp.where(lo, a, sw), jnp.where(lo, sw, a)


def _key_tails(nk):
    shape = (KEY_SLOTS - nk, 2 * LANES)
    row = lax.broadcasted_iota(jnp.int32, shape, 0)
    col = lax.broadcasted_iota(jnp.int32, shape, 1)
    vtail = jnp.where((row == 0) & (col >= LANES), 1.0, 0.0).astype(BF16)
    return jnp.zeros(shape, BF16), vtail


def _band_attention(qbd, kb, vb, bias):
    s = lax.dot_general(qbd, kb, _NT, preferred_element_type=F32) + bias
    p = jnp.exp(s - jnp.max(s, axis=-1, keepdims=True)).astype(BF16)
    oe = _dot(p, vb)
    return oe[:, 0:LANES] / oe[:, LANES:2 * LANES]


def _ungroup(on, nq):
    lo = lax.broadcasted_iota(jnp.int32, (nq, LANES), 1) < HEAD_DIM
    return (jnp.where(lo, on[0:nq], on[nq:2 * nq]),
            jnp.where(lo, on[2 * nq:3 * nq], on[3 * nq:4 * nq]))


def _cross_head(xq_h, mk_h, mvx_h):
    s = lax.dot_general(xq_h, mk_h, _NT, preferred_element_type=F32)
    p = jnp.exp(s - jnp.max(s, axis=-1, keepdims=True)).astype(BF16)
    oe = _dot(p, mvx_h)
    return oe[:, 0:X_HEAD_DIM] / oe[:, X_HEAD_DIM:2 * X_HEAD_DIM]


def _merge_out(x, xn, z_terms, win_ref, wp_refs, wout_ref):
    acc = None
    for i, (z, wp_ref) in enumerate(zip(z_terms, wp_refs)):
        gate = jax.nn.sigmoid(_dot(xn, win_ref[:, OFF_MG + i * D_MODEL:OFF_MG + (i + 1) * D_MODEL]))
        term = gate * _dot(z.astype(BF16), wp_ref[...])
        acc = term if acc is None else acc + term
    return x + _dot(acc.astype(BF16), wout_ref[...])


def _memkv_kernel(mem_ref, g_ref, wk_ref, wv_ref, mk_ref, mv_ref, mkb_ref, mvx_ref):
    mn = (_rms_scale(mem_ref[0]) * g_ref[0]).astype(BF16)
    mk = _dot(mn, wk_ref[0].astype(BF16))
    mv = _dot(mn, wv_ref[0].astype(BF16))
    mkb_ref[0, 0] = mk.astype(BF16)
    ones = jnp.ones((N_MEM, X_HEAD_DIM), BF16)
    for h in range(X_HEADS):
        cols = slice(h * X_HEAD_DIM, (h + 1) * X_HEAD_DIM)
        mk_ref[0, 0, pl.ds(h, N_MEM, stride=X_HEADS), :] = mk[:, cols]
        mv_ref[0, 0, pl.ds(h, N_MEM, stride=X_HEADS), :] = mv[:, cols]
        mvx_ref[0, 0, h, :, 0:X_HEAD_DIM] = mv[:, cols].astype(BF16)
        mvx_ref[0, 0, h, :, X_HEAD_DIM:2 * X_HEAD_DIM] = ones


def _memory_kv(mem, g, wk, wv):
    nb = mem.shape[0]
    return pl.pallas_call(
        _memkv_kernel,
        grid=(DEPTH, nb),
        in_specs=[
            pl.BlockSpec((1, N_MEM, D_MODEL), lambda l, b: (b, 0, 0)),
            pl.BlockSpec((1, 1, D_MODEL), lambda l, b: (l, 0, 0)),
            pl.BlockSpec((1, D_MODEL, X_W), lambda l, b: (l, 0, 0)),
            pl.BlockSpec((1, D_MODEL, X_W), lambda l, b: (l, 0, 0)),
        ],
        out_specs=[
            pl.BlockSpec((1, 1, N_MEM * X_HEADS, X_HEAD_DIM), lambda l, b: (l, b, 0, 0)),
            pl.BlockSpec((1, 1, N_MEM * X_HEADS, X_HEAD_DIM), lambda l, b: (l, b, 0, 0)),
            pl.BlockSpec((1, 1, N_MEM, X_W), lambda l, b: (l, b, 0, 0)),
            pl.BlockSpec((1, 1, X_HEADS, N_MEM, 2 * X_HEAD_DIM), lambda l, b: (l, b, 0, 0, 0)),
        ],
        out_shape=[
            jax.ShapeDtypeStruct((DEPTH, nb, N_MEM * X_HEADS, X_HEAD_DIM), F32),
            jax.ShapeDtypeStruct((DEPTH, nb, N_MEM * X_HEADS, X_HEAD_DIM), F32),
            jax.ShapeDtypeStruct((DEPTH, nb, N_MEM, X_W), BF16),
            jax.ShapeDtypeStruct((DEPTH, nb, X_HEADS, N_MEM, 2 * X_HEAD_DIM), BF16),
        ],
        name="memory_kv",
    )(mem, g.reshape(DEPTH, 1, D_MODEL), wk, wv)


def _prompt_kernel(layer_ref, x_ref, g_ref, win_ref, wpa_ref, wpb_ref, wpx_ref, wout_ref, cw_ref, cb_ref,
                   bias_ref, bd_ref, mk_ref, mvx_ref, fg_ref,
                   xo_ref, kl_ref, vl_ref, ul_ref,
                   kr_ref, vv_ref, uc_ref, q_ref, attn_ref, *, tile, final):
    t = pl.program_id(1)

    @pl.when(t == 0)
    def _():
        kr_ref[:, 0:WINDOW, :] = jnp.zeros((N_KV_HEADS, WINDOW, GQ_W), BF16)
        vv_ref[:, 0:WINDOW, 0:LANES] = jnp.zeros((N_KV_HEADS, WINDOW, LANES), BF16)
        vv_ref[:, :, LANES:2 * LANES] = jnp.ones((N_KV_HEADS, WINDOW + tile, LANES), BF16)
        uc_ref[0:CONV_PAD, :] = jnp.zeros((CONV_PAD, CONV_W), F32)

    x = x_ref[0]
    xn = (_rms_scale(x) * g_ref[...]).astype(BF16)

    def proj(off, width):
        return _dot(xn, win_ref[:, off:off + width])

    q_ref[...] = (proj(OFF_Q, ATT_W) * (HEAD_DIM ** -0.5)).astype(BF16)
    half = tile // 2
    w_kv = win_ref[:, OFF_K:OFF_GA]
    kv = jnp.concatenate([_dot(xn[0:half], w_kv), _dot(xn[half:tile], w_kv)], axis=0)
    k = kv[:, 0:KV_W]
    v = kv[:, KV_W:2 * KV_W]
    ga = proj(OFF_GA, ATT_W)
    kl_ref[0] = k[tile - WINDOW:tile]
    vl_ref[0] = v[tile - WINDOW:tile]
    for j, (kj, vj) in enumerate(zip(_dup_heads(k), _dup_heads(v))):
        kj = kj.astype(BF16)
        kr_ref[j, WINDOW:WINDOW + tile, 0:LANES] = kj
        kr_ref[j, WINDOW:WINDOW + tile, LANES:2 * LANES] = kj
        vv_ref[j, WINDOW:WINDOW + tile, 0:LANES] = vj.astype(BF16)

    fillers = [functools.partial(proj, off, CONV_W) for off in (OFF_BB, OFF_CC, OFF_HH, OFF_GB)]
    fillers += [functools.partial(proj, off, X_W) for off in (OFF_XQ, OFF_GX)]
    fillers += [lambda i=i: jax.nn.sigmoid(proj(OFF_MG + i * D_MODEL, D_MODEL)) for i in (1, 2)]
    filled = []
    n_chunks = tile // CHUNK
    ktail, vtail = _key_tails(BAND)

    for c in range(n_chunks):
        r0 = c * CHUNK
        qc = q_ref[r0:r0 + CHUNK, :]
        variant = jnp.where(t == 0, c, 2) if c < 2 else 2
        for j in range(N_KV_HEADS):
            qg = qc[:, j * GQ_W:(j + 1) * GQ_W]
            qbd = jnp.concatenate([qg] * GROUP, axis=0) * bd_ref[...]
            on = _band_attention(qbd, jnp.concatenate([kr_ref[j, r0:r0 + BAND, :], ktail], axis=0),
                                 jnp.concatenate([vv_ref[j, r0:r0 + BAND, :], vtail], axis=0),
                                 bias_ref[variant, j])
            lo, hi = _ungroup(on, CHUNK)
            attn_ref[r0:r0 + CHUNK, j * GQ_W:j * GQ_W + LANES] = lo
            attn_ref[r0:r0 + CHUNK, j * GQ_W + LANES:(j + 1) * GQ_W] = hi
        while len(filled) * n_chunks < (c + 1) * len(fillers):
            filled.append(fillers[len(filled)]())
    bb, cc, hh, gb, xq, gx, gate_b, gate_x = filled
    kr_ref[:, 0:WINDOW, :] = kr_ref[:, tile:tile + WINDOW, :]
    vv_ref[:, 0:WINDOW, 0:LANES] = vv_ref[:, tile:tile + WINDOW, 0:LANES]

    u = cc * hh
    uc_ref[CONV_PAD:CONV_PAD + tile, :] = u
    ul_ref[0] = u[tile - (CONV_K - 1):tile]
    conv = (uc_ref[CONV_PAD - 2:CONV_PAD - 2 + tile, :] * cw_ref[0:1, :]
            + uc_ref[CONV_PAD - 1:CONV_PAD - 1 + tile, :] * cw_ref[1:2, :]
            + u * cw_ref[2:3, :]) + cb_ref[...]
    uc_ref[0:CONV_PAD, :] = uc_ref[tile:tile + CONV_PAD, :]
    zb = (bb * conv * _silu(gb)).astype(BF16)
    za = (attn_ref[...] * _silu(ga)).astype(BF16)

    xq = (xq * (X_HEAD_DIM ** -0.5)).astype(BF16)
    merge_work = [lambda: _dot(za, wpa_ref[...]),
                  lambda: jax.nn.sigmoid(proj(OFF_MG, D_MODEL)),
                  lambda: _dot(zb, wpb_ref[...])]
    merged = []
    heads = []
    for h in range(X_HEADS):
        heads.append(_cross_head(xq[:, h * X_HEAD_DIM:(h + 1) * X_HEAD_DIM],
                                 mk_ref[0, :, h * X_HEAD_DIM:(h + 1) * X_HEAD_DIM], mvx_ref[0, h]))
        if h < len(merge_work):
            merged.append(merge_work[h]())
    ya, gate_a, yb = merged
    zx = (jnp.concatenate(heads, axis=1) * _silu(gx)).astype(BF16)
    acc = gate_a * ya + gate_b * yb + gate_x * _dot(zx, wpx_ref[...])
    out = x + _dot(acc.astype(BF16), wout_ref[...])
    if final:
        out = _rms_scale(out) * fg_ref[...]
    xo_ref[0] = out


def _const_spec(shape):
    nd = len(shape)
    return pl.BlockSpec(shape, lambda *_: (0,) * nd, pipeline_mode=pl.Buffered(1))


def _layer_spec(shape):
    nd = len(shape)
    return pl.BlockSpec((None,) + tuple(shape), lambda *a: (a[-1][0],) + (0,) * nd, pipeline_mode=pl.Buffered(1))


def _prompt_layer(layer, x, g, win, wpa, wpb, wpx, wout, cw, cb, bias, bd, mkb, mvx, fg, *, final):
    nb, seq, _ = x.shape
    tile = PROMPT_TILE
    assert seq % tile == 0 and tile % WINDOW == 0
    kern = functools.partial(_prompt_kernel, tile=tile, final=final)
    grid_spec = pltpu.PrefetchScalarGridSpec(
        num_scalar_prefetch=1,
        grid=(nb, seq // tile),
        in_specs=[
            pl.BlockSpec((1, tile, D_MODEL), lambda b, t, l: (b, t, 0)),
            _layer_spec((1, D_MODEL)),
            _layer_spec((D_MODEL, IN_W)),
            _layer_spec((ATT_W, D_MODEL)),
            _layer_spec((CONV_W, D_MODEL)),
            _layer_spec((X_W, D_MODEL)),
            _layer_spec((D_MODEL, D_MODEL)),
            _layer_spec((CONV_K, CONV_W)),
            _layer_spec((1, CONV_W)),
            _layer_spec((3, N_KV_HEADS, GROUP * CHUNK, KEY_SLOTS)),
            _const_spec((GROUP * CHUNK, GQ_W)),
            pl.BlockSpec((None, 1, N_MEM, X_W), lambda b, t, l: (l[0], b, 0, 0)),
            pl.BlockSpec((None, 1, X_HEADS, N_MEM, 2 * X_HEAD_DIM), lambda b, t, l: (l[0], b, 0, 0, 0)),
            _const_spec((1, D_MODEL)),
        ],
        out_specs=[
            pl.BlockSpec((1, tile, D_MODEL), lambda b, t, l: (b, t, 0)),
            pl.BlockSpec((1, WINDOW, KV_W), lambda b, t, l: (b, 0, 0)),
            pl.BlockSpec((1, WINDOW, KV_W), lambda b, t, l: (b, 0, 0)),
            pl.BlockSpec((1, CONV_K - 1, CONV_W), lambda b, t, l: (b, 0, 0)),
        ],
        scratch_shapes=[
            pltpu.VMEM((N_KV_HEADS, WINDOW + tile, GQ_W), BF16),
            pltpu.VMEM((N_KV_HEADS, WINDOW + tile, 2 * LANES), BF16),
            pltpu.VMEM((CONV_PAD + tile, CONV_W), F32),
            pltpu.VMEM((tile, ATT_W), BF16),
            pltpu.VMEM((tile, ATT_W), F32),
        ],
    )
    return pl.pallas_call(
        kern,
        grid_spec=grid_spec,
        out_shape=[
            jax.ShapeDtypeStruct((nb, seq, D_MODEL), F32),
            jax.ShapeDtypeStruct((nb, WINDOW, KV_W), F32),
            jax.ShapeDtypeStruct((nb, WINDOW, KV_W), F32),
            jax.ShapeDtypeStruct((nb, CONV_K - 1, CONV_W), F32),
        ],
        compiler_params=pltpu.CompilerParams(
            dimension_semantics=("arbitrary", "arbitrary"), vmem_limit_bytes=VMEM_LIMIT),
        name="prompt_layer",
    )(layer, x, g, win, wpa, wpb, wpx, wout, cw, cb, bias, bd, mkb, mvx, fg)


def _sample_kernel(layer_ref, x_ref, g_ref, win_ref, wpa_ref, wpb_ref, wpx_ref, wout_ref, cw_ref, cb_ref,
                   bias_ref, bd_ref, ck_ref, cv_ref, cc_ref, cmk_ref, cmv_ref, fg_ref,
                   xo_ref, ko_ref, vo_ref, uo_ref,
                   uc_ref, attn_ref, conv_ref, xat_ref, *, nb, nt, final):
    x = x_ref[...]
    xn = (_rms_scale(x) * g_ref[...]).astype(BF16)

    p1 = _dot(xn, win_ref[:, OFF_Q:OFF_BB])
    q = (p1[:, OFF_Q:OFF_K] * (HEAD_DIM ** -0.5)).astype(BF16)
    k = p1[:, OFF_K:OFF_V]
    v = p1[:, OFF_V:OFF_GA]
    ga = p1[:, OFF_GA:OFF_BB]
    ko_ref[...] = k
    vo_ref[...] = v
    ones = jnp.ones((WINDOW + nt, LANES), BF16)
    ktail, vtail = _key_tails(WINDOW + nt)
    for b in range(nb):
        rows = slice(b * nt, (b + 1) * nt)
        kall = _dup_heads(jnp.concatenate([ck_ref[b], k[rows]], axis=0))
        vall = _dup_heads(jnp.concatenate([cv_ref[b], v[rows]], axis=0))
        for j in range(N_KV_HEADS):
            kj = kall[j].astype(BF16)
            vj = vall[j].astype(BF16)
            qg = q[rows, j * GQ_W:(j + 1) * GQ_W]
            qbd = jnp.concatenate([qg] * GROUP, axis=0) * bd_ref[...]
            on = _band_attention(qbd, jnp.concatenate([jnp.concatenate([kj, kj], axis=1), ktail], axis=0),
                                 jnp.concatenate([jnp.concatenate([vj, ones], axis=1), vtail], axis=0),
                                 bias_ref[j])
            lo, hi = _ungroup(on, nt)
            attn_ref[rows, j * GQ_W:j * GQ_W + LANES] = lo
            attn_ref[rows, j * GQ_W + LANES:(j + 1) * GQ_W] = hi
    za = attn_ref[...] * _silu(ga)

    p2 = _dot(xn, win_ref[:, OFF_BB:OFF_XQ])
    bb = p2[:, 0:CONV_W]
    u = p2[:, CONV_W:2 * CONV_W] * p2[:, 2 * CONV_W:3 * CONV_W]
    gb = p2[:, 3 * CONV_W:4 * CONV_W]
    for b in range(nb):
        rows = slice(b * nt, (b + 1) * nt)
        ub = u[rows]
        uc_ref[b, CONV_PAD - (CONV_K - 1):CONV_PAD, :] = cc_ref[b]
        uc_ref[b, CONV_PAD:CONV_PAD + nt, :] = ub
        uo_ref[b] = ub[nt - (CONV_K - 1):nt]
        conv_ref[rows, :] = (uc_ref[b, CONV_PAD - 2:CONV_PAD - 2 + nt, :] * cw_ref[0:1, :]
                             + uc_ref[b, CONV_PAD - 1:CONV_PAD - 1 + nt, :] * cw_ref[1:2, :]
                             + ub * cw_ref[2:3, :]) + cb_ref[...]
    zb = bb * conv_ref[...] * _silu(gb)

    p3 = _dot(xn, win_ref[:, OFF_XQ:OFF_MG])
    xq = (p3[:, 0:X_W] * (X_HEAD_DIM ** -0.5)).astype(BF16)
    gx = p3[:, X_W:2 * X_W]
    mones = jnp.ones((N_MEM, X_HEAD_DIM), BF16)
    for b in range(nb):
        rows = slice(b * nt, (b + 1) * nt)
        for h in range(X_HEADS):
            cols = slice(h * X_HEAD_DIM, (h + 1) * X_HEAD_DIM)
            head_rows = pl.ds(h, N_MEM, stride=X_HEADS)
            mvx = jnp.concatenate([cmv_ref[b, head_rows, :].astype(BF16), mones], axis=1)
            xat_ref[rows, cols] = _cross_head(xq[rows, cols], cmk_ref[b, head_rows, :].astype(BF16), mvx)
    zx = xat_ref[...] * _silu(gx)

    out = _merge_out(x, xn, (za, zb, zx), win_ref, (wpa_ref, wpb_ref, wpx_ref), wout_ref)
    if final:
        out = _rms_scale(out) * fg_ref[...]
    xo_ref[...] = out


def _sample_layer(layer, x, g, win, wpa, wpb, wpx, wout, cw, cb, bias, bd, ck, cv, cc, cmk, cmv, fg,
                  *, nb, nt, final):
    rows = nb * nt
    kern = functools.partial(_sample_kernel, nb=nb, nt=nt, final=final)
    grid_spec = pltpu.PrefetchScalarGridSpec(
        num_scalar_prefetch=1,
        grid=(1,),
        in_specs=[
            _const_spec((rows, D_MODEL)),
            _layer_spec((1, D_MODEL)),
            _layer_spec((D_MODEL, IN_W)),
            _layer_spec((ATT_W, D_MODEL)),
            _layer_spec((CONV_W, D_MODEL)),
            _layer_spec((X_W, D_MODEL)),
            _layer_spec((D_MODEL, D_MODEL)),
            _layer_spec((CONV_K, CONV_W)),
            _layer_spec((1, CONV_W)),
            _layer_spec((N_KV_HEADS, GROUP * nt, KEY_SLOTS)),
            _const_spec((GROUP * nt, GQ_W)),
            _layer_spec((nb, WINDOW, KV_W)),
            _layer_spec((nb, WINDOW, KV_W)),
            _layer_spec((nb, CONV_K - 1, CONV_W)),
            _layer_spec((nb, N_MEM * X_HEADS, X_HEAD_DIM)),
            _layer_spec((nb, N_MEM * X_HEADS, X_HEAD_DIM)),
            _const_spec((1, D_MODEL)),
        ],
        out_specs=[
            pl.BlockSpec((rows, D_MODEL), lambda i, l: (0, 0)),
            pl.BlockSpec((rows, KV_W), lambda i, l: (0, 0)),
            pl.BlockSpec((rows, KV_W), lambda i, l: (0, 0)),
            pl.BlockSpec((nb, CONV_K - 1, CONV_W), lambda i, l: (0, 0, 0)),
        ],
        scratch_shapes=[
            pltpu.VMEM((nb, CONV_PAD + nt, CONV_W), F32),
            pltpu.VMEM((rows, ATT_W), F32),
            pltpu.VMEM((rows, CONV_W), F32),
            pltpu.VMEM((rows, X_W), F32),
        ],
    )
    return pl.pallas_call(
        kern,
        grid_spec=grid_spec,
        out_shape=[
            jax.ShapeDtypeStruct((rows, D_MODEL), F32),
            jax.ShapeDtypeStruct((rows, KV_W), F32),
            jax.ShapeDtypeStruct((rows, KV_W), F32),
            jax.ShapeDtypeStruct((nb, CONV_K - 1, CONV_W), F32),
        ],
        compiler_params=pltpu.CompilerParams(
            dimension_semantics=("arbitrary",), vmem_limit_bytes=VMEM_LIMIT),
        name="sample_layer",
    )(layer, x, g, win, wpa, wpb, wpx, wout, cw, cb, bias, bd, ck, cv, cc, cmk, cmv, fg)


def _alibi_slopes():
    return np.power(2.0, -8.0 * np.arange(1, N_HEADS + 1, dtype=np.float32) / N_HEADS).astype(np.float32)


def _alibi_bias(nq, nk, masked):
    dist = np.abs(np.arange(nq)[:, None] + WINDOW - np.arange(nk)[None, :]).astype(np.float32)
    slope = _alibi_slopes().reshape(N_KV_HEADS, GROUP)
    bias = -(slope[:, :, None, None] * dist[None, None])
    bias = np.where(np.arange(nk)[None, None, None, :] < masked, -np.inf, bias)
    return bias.reshape(N_KV_HEADS, GROUP * nq, nk).astype(np.float32)


def _bias_with_sink(alibi, attn_sink, nq):
    nk = alibi.shape[-1]
    lead = alibi.shape[:-3]
    sink = jnp.repeat(attn_sink.reshape(DEPTH, N_KV_HEADS, GROUP), nq, axis=2)[..., None]
    sink = jnp.broadcast_to(sink.reshape((DEPTH,) + (1,) * len(lead) + sink.shape[1:]),
                            (DEPTH,) + alibi.shape[:-1] + (1,))
    real = jnp.broadcast_to(jnp.asarray(alibi), (DEPTH,) + alibi.shape)
    unused = jnp.full((DEPTH,) + alibi.shape[:-1] + (KEY_SLOTS - nk - 1,), -jnp.inf, F32)
    return jnp.concatenate([real, sink, unused], axis=-1)


def _block_diag_mask(nq):
    rows = np.arange(GROUP * nq)[:, None] // nq
    cols = np.arange(GQ_W)[None, :] // HEAD_DIM
    return (rows == cols).astype(np.float32)


def kernel(x_prompt, x_sample, cache_attn_k, cache_attn_v, cache_conv, cache_mem_k, cache_mem_v,
           mem_prompt, norm_g, w_in, attn_sink, w_pa, conv_w, conv_b, w_pb, mem_norm_g,
           w_mk, w_mv, w_px, w_out, final_g):
    bp, seq, _ = x_prompt.shape
    bs, nt, _ = x_sample.shape
    win_len = cache_attn_k.shape[2]
    assert win_len == WINDOW and nt <= CHUNK and seq >= WINDOW

    w_in_b, w_pa_b, w_pb_b, w_px_b, w_out_b = (w.astype(BF16) for w in (w_in, w_pa, w_pb, w_px, w_out))

    bias_p = _bias_with_sink(np.stack([_alibi_bias(CHUNK, BAND, m) for m in (WINDOW, CHUNK, 0)]), attn_sink, CHUNK)
    bias_s = _bias_with_sink(_alibi_bias(nt, WINDOW + nt, 0), attn_sink, nt)
    bd_p = jnp.asarray(_block_diag_mask(CHUNK), BF16)
    bd_s = jnp.asarray(_block_diag_mask(nt), BF16)
    fg = final_g.reshape(1, D_MODEL)

    mk, mv, mkb, mvx = _memory_kv(mem_prompt, mem_norm_g, w_mk, w_mv)

    xp = x_prompt
    xs = x_sample.reshape(bs * nt, D_MODEL)
    ck = cache_attn_k.reshape(DEPTH, bs, WINDOW, KV_W)
    cv = cache_attn_v.reshape(DEPTH, bs, WINDOW, KV_W)
    shared = (norm_g.reshape(DEPTH, 1, D_MODEL), w_in_b, w_pa_b, w_pb_b, w_px_b, w_out_b,
              conv_w, conv_b.reshape(DEPTH, 1, CONV_W))
    cmk = cache_mem_k.reshape(DEPTH, bs, N_MEM * X_HEADS, X_HEAD_DIM)
    cmv = cache_mem_v.reshape(DEPTH, bs, N_MEM * X_HEADS, X_HEAD_DIM)
    pk, pv, pc, sk, sv, sc = [], [], [], [], [], []
    for l in range(DEPTH):
        final = l == DEPTH - 1
        layer = jnp.full((1,), l, jnp.int32)
        xp, kl, vl, ul = _prompt_layer(layer, xp, *shared, bias_p, bd_p, mkb, mvx, fg, final=final)
        pk.append(kl)
        pv.append(vl)
        pc.append(ul)
        xs, ko, vo, uo = _sample_layer(layer, xs, *shared, bias_s, bd_s, ck, cv, cache_conv,
                                       cmk, cmv, fg, nb=bs, nt=nt, final=final)
        sk.append(ko)
        sv.append(vo)
        sc.append(uo)

    kv_p = (DEPTH, bp, WINDOW, N_KV_HEADS, HEAD_DIM)
    kv_s = (DEPTH, bs, nt, N_KV_HEADS, HEAD_DIM)
    mem_shape = (DEPTH, bp, N_MEM, X_HEADS, X_HEAD_DIM)
    return (xp, xs.reshape(bs, nt, D_MODEL),
            jnp.stack(pk).reshape(kv_p), jnp.stack(pv).reshape(kv_p), jnp.stack(pc),
            mk.reshape(mem_shape), mv.reshape(mem_shape),
            jnp.stack(sk).reshape(kv_s), jnp.stack(sv).reshape(kv_s), jnp.stack(sc))
```

```python
import functools

import numpy as np
import jax
import jax.numpy as jnp
from jax import lax
from jax.experimental import pallas as pl
from jax.experimental.pallas import tpu as pltpu

D_MODEL = 1024
DEPTH = 4
CHUNK = 64
WINDOW = 128
N_HEADS = 8
N_KV_HEADS = 2
GROUP = N_HEADS // N_KV_HEADS
HEAD_DIM = 64
ATT_W = N_HEADS * HEAD_DIM
KV_W = N_KV_HEADS * HEAD_DIM
CONV_W = 512
CONV_K = 3
X_HEADS = 4
X_HEAD_DIM = 128
X_W = X_HEADS * X_HEAD_DIM
N_MEM = 256
EPS = 1e-6

OFF_Q = 0
OFF_K = OFF_Q + ATT_W
OFF_V = OFF_K + KV_W
OFF_GA = OFF_V + KV_W
OFF_BB = OFF_GA + ATT_W
OFF_CC = OFF_BB + CONV_W
OFF_HH = OFF_CC + CONV_W
OFF_GB = OFF_HH + CONV_W
OFF_XQ = OFF_GB + CONV_W
OFF_GX = OFF_XQ + X_W
OFF_MG = OFF_GX + X_W
IN_W = OFF_MG + 3 * D_MODEL

LANES = 128
GQ_W = GROUP * HEAD_DIM
BAND = WINDOW + CHUNK
KEY_SLOTS = 256
CONV_PAD = 8
PROMPT_TILE = 512
Q_ROWS = 128
VMEM_LIMIT = 60 * 1024 * 1024

F32 = jnp.float32
BF16 = jnp.bfloat16
_NT = (((1,), (1,)), ((), ()))


def _rms_scale(x):
    return x * lax.rsqrt(jnp.mean(x * x, axis=-1, keepdims=True) + EPS)


def _silu(x):
    return x * jax.nn.sigmoid(x)


def _dot(a, b):
    return jnp.dot(a, b, preferred_element_type=F32)


def _dup_heads(a):
    lane = lax.broadcasted_iota(jnp.int32, a.shape, 1)
    sw = pltpu.roll(a, HEAD_DIM, 1)
    lo = lane < HEAD_DIM
    return jnp.where(lo, a, sw), jnp.where(lo, sw, a)


def _key_tails(nk):
    shape = (KEY_SLOTS - nk, 2 * LANES)
    row = lax.broadcasted_iota(jnp.int32, shape, 0)
    col = lax.broadcasted_iota(jnp.int32, shape, 1)
    vtail = jnp.where((row == 0) & (col >= LANES), 1.0, 0.0).astype(BF16)
    return jnp.zeros(shape, BF16), vtail


def _band_attention(qbd, kb, vb, bias):
    s = lax.dot_general(qbd, kb, _NT, preferred_element_type=F32) + bias
    p = jnp.exp(s - jnp.max(s, axis=-1, keepdims=True)).astype(BF16)
    oe = _dot(p, vb)
    return oe[:, 0:LANES] / oe[:, LANES:2 * LANES]


def _ungroup(on, nq):
    lo = lax.broadcasted_iota(jnp.int32, (nq, LANES), 1) < HEAD_DIM
    return (jnp.where(lo, on[0:nq], on[nq:2 * nq]),
            jnp.where(lo, on[2 * nq:3 * nq], on[3 * nq:4 * nq]))


def _cross_head(xq_h, mk_h, mvx_h):
    s = lax.dot_general(xq_h, mk_h, _NT, preferred_element_type=F32)
    p = jnp.exp(s - jnp.max(s, axis=-1, keepdims=True)).astype(BF16)
    oe = _dot(p, mvx_h)
    return oe[:, 0:X_HEAD_DIM] / oe[:, X_HEAD_DIM:2 * X_HEAD_DIM]


def _merge_out(x, xn, z_terms, win_ref, wp_refs, wout_ref):
    acc = None
    for i, (z, wp_ref) in enumerate(zip(z_terms, wp_refs)):
        gate = jax.nn.sigmoid(_dot(xn, win_ref[:, OFF_MG + i * D_MODEL:OFF_MG + (i + 1) * D_MODEL]))
        term = gate * _dot(z.astype(BF16), wp_ref[...])
        acc = term if acc is None else acc + term
    return x + _dot(acc.astype(BF16), wout_ref[...])


def _memkv_kernel(mem_ref, g_ref, wk_ref, wv_ref, mk_ref, mv_ref, mkb_ref, mvx_ref):
    mn = (_rms_scale(mem_ref[0]) * g_ref[0]).astype(BF16)
    mk = _dot(mn, wk_ref[0].astype(BF16))
    mv = _dot(mn, wv_ref[0].astype(BF16))
    mkb_ref[0, 0] = mk.astype(BF16)
    ones = jnp.ones((N_MEM, X_HEAD_DIM), BF16)
    for h in range(X_HEADS):
        cols = slice(h * X_HEAD_DIM, (h + 1) * X_HEAD_DIM)
        mk_ref[0, 0, pl.ds(h, N_MEM, stride=X_HEADS), :] = mk[:, cols]
        mv_ref[0, 0, pl.ds(h, N_MEM, stride=X_HEADS), :] = mv[:, cols]
        mvx_ref[0, 0, h, :, 0:X_HEAD_DIM] = mv[:, cols].astype(BF16)
        mvx_ref[0, 0, h, :, X_HEAD_DIM:2 * X_HEAD_DIM] = ones


def _memory_kv(mem, g, wk, wv):
    nb = mem.shape[0]
    return pl.pallas_call(
        _memkv_kernel,
        grid=(DEPTH, nb),
        in_specs=[
            pl.BlockSpec((1, N_MEM, D_MODEL), lambda l, b: (b, 0, 0)),
            pl.BlockSpec((1, 1, D_MODEL), lambda l, b: (l, 0, 0)),
            pl.BlockSpec((1, D_MODEL, X_W), lambda l, b: (l, 0, 0)),
            pl.BlockSpec((1, D_MODEL, X_W), lambda l, b: (l, 0, 0)),
        ],
        out_specs=[
            pl.BlockSpec((1, 1, N_MEM * X_HEADS, X_HEAD_DIM), lambda l, b: (l, b, 0, 0)),
            pl.BlockSpec((1, 1, N_MEM * X_HEADS, X_HEAD_DIM), lambda l, b: (l, b, 0, 0)),
            pl.BlockSpec((1, 1, N_MEM, X_W), lambda l, b: (l, b, 0, 0)),
            pl.BlockSpec((1, 1, X_HEADS, N_MEM, 2 * X_HEAD_DIM), lambda l, b: (l, b, 0, 0, 0)),
        ],
        out_shape=[
            jax.ShapeDtypeStruct((DEPTH, nb, N_MEM * X_HEADS, X_HEAD_DIM), F32),
            jax.ShapeDtypeStruct((DEPTH, nb, N_MEM * X_HEADS, X_HEAD_DIM), F32),
            jax.ShapeDtypeStruct((DEPTH, nb, N_MEM, X_W), BF16),
            jax.ShapeDtypeStruct((DEPTH, nb, X_HEADS, N_MEM, 2 * X_HEAD_DIM), BF16),
        ],
        name="memory_kv",
    )(mem, g.reshape(DEPTH, 1, D_MODEL), wk, wv)


def _prompt_tile(t, x_ref, g_ref, win_ref, wpa_ref, wpb_ref, wpx_ref, wout_ref, cw_ref, cb_ref,
                 bias_ref, bd_ref, mk_ref, mvx_ref, fg_ref,
                 xo_ref, kl_ref, vl_ref, ul_ref,
                 kr_ref, vv_ref, uc_ref, q_ref, attn_ref, *, tile, final):
    @pl.when(t == 0)
    def _():
        kr_ref[:, 0:WINDOW, :] = jnp.zeros((N_KV_HEADS, WINDOW, GQ_W), BF16)
        vv_ref[:, 0:WINDOW, 0:LANES] = jnp.zeros((N_KV_HEADS, WINDOW, LANES), BF16)
        vv_ref[:, :, LANES:2 * LANES] = jnp.ones((N_KV_HEADS, WINDOW + tile, LANES), BF16)
        uc_ref[0:CONV_PAD, :] = jnp.zeros((CONV_PAD, CONV_W), F32)

    x = x_ref[0]
    xn = (_rms_scale(x) * g_ref[...]).astype(BF16)

    def proj(off, width):
        return _dot(xn, win_ref[:, off:off + width])

    w_q = win_ref[:, OFF_Q:OFF_K]
    for r in range(0, tile, Q_ROWS):
        q_ref[r:r + Q_ROWS, :] = (_dot(xn[r:r + Q_ROWS], w_q) * (HEAD_DIM ** -0.5)).astype(BF16)
    half = tile // 2
    w_kv = win_ref[:, OFF_K:OFF_GA]
    kv = jnp.concatenate([_dot(xn[0:half], w_kv), _dot(xn[half:tile], w_kv)], axis=0)
    k = kv[:, 0:KV_W]
    v = kv[:, KV_W:2 * KV_W]
    ga = proj(OFF_GA, ATT_W)
    kl_ref[0] = k[tile - WINDOW:tile]
    vl_ref[0] = v[tile - WINDOW:tile]
    for j, (kj, vj) in enumerate(zip(_dup_heads(k), _dup_heads(v))):
        kj = kj.astype(BF16)
        kr_ref[j, WINDOW:WINDOW + tile, 0:LANES] = kj
        kr_ref[j, WINDOW:WINDOW + tile, LANES:2 * LANES] = kj
        vv_ref[j, WINDOW:WINDOW + tile, 0:LANES] = vj.astype(BF16)

    fillers = [functools.partial(proj, off, CONV_W) for off in (OFF_BB, OFF_CC, OFF_HH, OFF_GB)]
    fillers += [functools.partial(proj, off, X_W) for off in (OFF_XQ, OFF_GX)]
    fillers += [lambda i=i: jax.nn.sigmoid(proj(OFF_MG + i * D_MODEL, D_MODEL)) for i in (1, 2)]
    filled = []
    n_chunks = tile // CHUNK
    ktail, vtail = _key_tails(BAND)

    for c in range(n_chunks):
        r0 = c * CHUNK
        qc = q_ref[r0:r0 + CHUNK, :]
        variant = jnp.where(t == 0, c, 2) if c < 2 else 2
        for j in range(N_KV_HEADS):
            qg = qc[:, j * GQ_W:(j + 1) * GQ_W]
            qbd = jnp.concatenate([qg] * GROUP, axis=0) * bd_ref[...]
            on = _band_attention(qbd, jnp.concatenate([kr_ref[j, r0:r0 + BAND, :], ktail], axis=0),
                                 jnp.concatenate([vv_ref[j, r0:r0 + BAND, :], vtail], axis=0),
                                 bias_ref[variant, j])
            lo, hi = _ungroup(on, CHUNK)
            attn_ref[r0:r0 + CHUNK, j * GQ_W:j * GQ_W + LANES] = lo
            attn_ref[r0:r0 + CHUNK, j * GQ_W + LANES:(j + 1) * GQ_W] = hi
        while len(filled) * n_chunks < (c + 1) * len(fillers):
            filled.append(fillers[len(filled)]())
    bb, cc, hh, gb, xq, gx, gate_b, gate_x = filled
    kr_ref[:, 0:WINDOW, :] = kr_ref[:, tile:tile + WINDOW, :]
    vv_ref[:, 0:WINDOW, 0:LANES] = vv_ref[:, tile:tile + WINDOW, 0:LANES]

    u = cc * hh
    uc_ref[CONV_PAD:CONV_PAD + tile, :] = u
    ul_ref[0] = u[tile - (CONV_K - 1):tile]
    conv = (uc_ref[CONV_PAD - 2:CONV_PAD - 2 + tile, :] * cw_ref[0:1, :]
            + uc_ref[CONV_PAD - 1:CONV_PAD - 1 + tile, :] * cw_ref[1:2, :]
            + u * cw_ref[2:3, :]) + cb_ref[...]
    uc_ref[0:CONV_PAD, :] = uc_ref[tile:tile + CONV_PAD, :]
    zb = (bb * conv * _silu(gb)).astype(BF16)
    za = (attn_ref[...] * _silu(ga)).astype(BF16)

    xq = (xq * (X_HEAD_DIM ** -0.5)).astype(BF16)
    merge_work = [lambda: _dot(za, wpa_ref[...]),
                  lambda: jax.nn.sigmoid(proj(OFF_MG, D_MODEL)),
                  lambda: _dot(zb, wpb_ref[...])]
    merged = []
    heads = []
    for h in range(X_HEADS):
        heads.append(_cross_head(xq[:, h * X_HEAD_DIM:(h + 1) * X_HEAD_DIM],
                                 mk_ref[0, :, h * X_HEAD_DIM:(h + 1) * X_HEAD_DIM], mvx_ref[0, h]))
        if h < len(merge_work):
            merged.append(merge_work[h]())
    ya, gate_a, yb = merged
    zx = (jnp.concatenate(heads, axis=1) * _silu(gx)).astype(BF16)
    acc = gate_a * ya + gate_b * yb + gate_x * _dot(zx, wpx_ref[...])
    out = x + _dot(acc.astype(BF16), wout_ref[...])
    if final:
        out = _rms_scale(out) * fg_ref[...]
    xo_ref[0] = out


def _const_spec(shape):
    nd = len(shape)
    return pl.BlockSpec(shape, lambda *_: (0,) * nd, pipeline_mode=pl.Buffered(1))


def _layer_spec(shape):
    nd = len(shape)
    return pl.BlockSpec((None,) + tuple(shape), lambda *a: (a[-1][0],) + (0,) * nd, pipeline_mode=pl.Buffered(1))


def _sample_step(x_ref, g_ref, win_ref, wpa_ref, wpb_ref, wpx_ref, wout_ref, cw_ref, cb_ref,
                 bias_ref, bd_ref, ck_ref, cv_ref, cc_ref, cmk_ref, cmv_ref, fg_ref,
                 xo_ref, ko_ref, vo_ref, uo_ref,
                 uc_ref, attn_ref, conv_ref, xat_ref, *, nb, nt, final):
    x = x_ref[...]
    xn = (_rms_scale(x) * g_ref[...]).astype(BF16)

    p1 = _dot(xn, win_ref[:, OFF_Q:OFF_BB])
    q = (p1[:, OFF_Q:OFF_K] * (HEAD_DIM ** -0.5)).astype(BF16)
    k = p1[:, OFF_K:OFF_V]
    v = p1[:, OFF_V:OFF_GA]
    ga = p1[:, OFF_GA:OFF_BB]
    ko_ref[...] = k
    vo_ref[...] = v
    ones = jnp.ones((WINDOW + nt, LANES), BF16)
    ktail, vtail = _key_tails(WINDOW + nt)
    for b in range(nb):
        rows = slice(b * nt, (b + 1) * nt)
        kall = _dup_heads(jnp.concatenate([ck_ref[b], k[rows]], axis=0))
        vall = _dup_heads(jnp.concatenate([cv_ref[b], v[rows]], axis=0))
        for j in range(N_KV_HEADS):
            kj = kall[j].astype(BF16)
            vj = vall[j].astype(BF16)
            qg = q[rows, j * GQ_W:(j + 1) * GQ_W]
            qbd = jnp.concatenate([qg] * GROUP, axis=0) * bd_ref[...]
            on = _band_attention(qbd, jnp.concatenate([jnp.concatenate([kj, kj], axis=1), ktail], axis=0),
                                 jnp.concatenate([jnp.concatenate([vj, ones], axis=1), vtail], axis=0),
                                 bias_ref[j])
            lo, hi = _ungroup(on, nt)
            attn_ref[rows, j * GQ_W:j * GQ_W + LANES] = lo
            attn_ref[rows, j * GQ_W + LANES:(j + 1) * GQ_W] = hi
    za = attn_ref[...] * _silu(ga)

    p2 = _dot(xn, win_ref[:, OFF_BB:OFF_XQ])
    bb = p2[:, 0:CONV_W]
    u = p2[:, CONV_W:2 * CONV_W] * p2[:, 2 * CONV_W:3 * CONV_W]
    gb = p2[:, 3 * CONV_W:4 * CONV_W]
    for b in range(nb):
        rows = slice(b * nt, (b + 1) * nt)
        ub = u[rows]
        uc_ref[b, CONV_PAD - (CONV_K - 1):CONV_PAD, :] = cc_ref[b]
        uc_ref[b, CONV_PAD:CONV_PAD + nt, :] = ub
        uo_ref[b] = ub[nt - (CONV_K - 1):nt]
        conv_ref[rows, :] = (uc_ref[b, CONV_PAD - 2:CONV_PAD - 2 + nt, :] * cw_ref[0:1, :]
                             + uc_ref[b, CONV_PAD - 1:CONV_PAD - 1 + nt, :] * cw_ref[1:2, :]
                             + ub * cw_ref[2:3, :]) + cb_ref[...]
    zb = bb * conv_ref[...] * _silu(gb)

    p3 = _dot(xn, win_ref[:, OFF_XQ:OFF_MG])
    xq = (p3[:, 0:X_W] * (X_HEAD_DIM ** -0.5)).astype(BF16)
    gx = p3[:, X_W:2 * X_W]
    mones = jnp.ones((N_MEM, X_HEAD_DIM), BF16)
    for b in range(nb):
        rows = slice(b * nt, (b + 1) * nt)
        for h in range(X_HEADS):
            cols = slice(h * X_HEAD_DIM, (h + 1) * X_HEAD_DIM)
            head_rows = pl.ds(h, N_MEM, stride=X_HEADS)
            mvx = jnp.concatenate([cmv_ref[b, head_rows, :].astype(BF16), mones], axis=1)
            xat_ref[rows, cols] = _cross_head(xq[rows, cols], cmk_ref[b, head_rows, :].astype(BF16), mvx)
    zx = xat_ref[...] * _silu(gx)

    out = _merge_out(x, xn, (za, zb, zx), win_ref, (wpa_ref, wpb_ref, wpx_ref), wout_ref)
    if final:
        out = _rms_scale(out) * fg_ref[...]
    xo_ref[...] = out


def _layer_kernel(layer_ref, x_ref, g_ref, win_ref, wpa_ref, wpb_ref, wpx_ref, wout_ref, cw_ref, cb_ref, fg_ref,
                  bias_ref, bd_ref, mk_ref, mvx_ref,
                  xs_ref, sbias_ref, sbd_ref, ck_ref, cv_ref, cc_ref, cmk_ref, cmv_ref,
                  xo_ref, kl_ref, vl_ref, ul_ref, xso_ref, ko_ref, vo_ref, uo_ref,
                  kr_ref, vv_ref, uc_ref, q_ref, attn_ref, suc_ref, sattn_ref, sconv_ref, sxat_ref,
                  *, tile, tiles_per_seq, prompt_steps, nb, nt, final):
    i = pl.program_id(0)
    weights = (g_ref, win_ref, wpa_ref, wpb_ref, wpx_ref, wout_ref, cw_ref, cb_ref)

    @pl.when(i < prompt_steps)
    def _():
        _prompt_tile(lax.rem(i, tiles_per_seq), x_ref, *weights, bias_ref, bd_ref, mk_ref, mvx_ref, fg_ref,
                     xo_ref, kl_ref, vl_ref, ul_ref, kr_ref, vv_ref, uc_ref, q_ref, attn_ref,
                     tile=tile, final=final)

    @pl.when(i == prompt_steps)
    def _():
        _sample_step(xs_ref, *weights, sbias_ref, sbd_ref, ck_ref, cv_ref, cc_ref, cmk_ref, cmv_ref, fg_ref,
                     xso_ref, ko_ref, vo_ref, uo_ref, suc_ref, sattn_ref, sconv_ref, sxat_ref,
                     nb=nb, nt=nt, final=final)


def _layer(layer, x, xs, g, win, wpa, wpb, wpx, wout, cw, cb, fg, bias, bd, mkb, mvx,
           sbias, sbd, ck, cv, cc, cmk, cmv, *, nt, final):
    nb, seq, _ = x.shape
    rows = xs.shape[0]
    nbs = rows // nt
    tile = PROMPT_TILE
    assert seq % tile == 0 and tile % WINDOW == 0
    tps = seq // tile
    prompt_steps = nb * tps
    kern = functools.partial(_layer_kernel, tile=tile, tiles_per_seq=tps, prompt_steps=prompt_steps,
                             nb=nbs, nt=nt, final=final)

    def seq_of(i):
        return jnp.minimum(i, prompt_steps - 1) // tps

    def tile_of(i):
        return jnp.minimum(i, prompt_steps - 1) % tps

    grid_spec = pltpu.PrefetchScalarGridSpec(
        num_scalar_prefetch=1,
        grid=(prompt_steps + 1,),
        in_specs=[
            pl.BlockSpec((1, tile, D_MODEL), lambda i, l: (seq_of(i), tile_of(i), 0)),
            _layer_spec((1, D_MODEL)),
            _layer_spec((D_MODEL, IN_W)),
            _layer_spec((ATT_W, D_MODEL)),
            _layer_spec((CONV_W, D_MODEL)),
            _layer_spec((X_W, D_MODEL)),
            _layer_spec((D_MODEL, D_MODEL)),
            _layer_spec((CONV_K, CONV_W)),
            _layer_spec((1, CONV_W)),
            _const_spec((1, D_MODEL)),
            _layer_spec((3, N_KV_HEADS, GROUP * CHUNK, KEY_SLOTS)),
            _const_spec((GROUP * CHUNK, GQ_W)),
            pl.BlockSpec((None, 1, N_MEM, X_W), lambda i, l: (l[0], seq_of(i), 0, 0)),
            pl.BlockSpec((None, 1, X_HEADS, N_MEM, 2 * X_HEAD_DIM), lambda i, l: (l[0], seq_of(i), 0, 0, 0)),
            _const_spec((rows, D_MODEL)),
            _layer_spec((N_KV_HEADS, GROUP * nt, KEY_SLOTS)),
            _const_spec((GROUP * nt, GQ_W)),
            _layer_spec((nbs, WINDOW, KV_W)),
            _layer_spec((nbs, WINDOW, KV_W)),
            _layer_spec((nbs, CONV_K - 1, CONV_W)),
            _layer_spec((nbs, N_MEM * X_HEADS, X_HEAD_DIM)),
            _layer_spec((nbs, N_MEM * X_HEADS, X_HEAD_DIM)),
        ],
        out_specs=[
            pl.BlockSpec((1, tile, D_MODEL), lambda i, l: (seq_of(i), tile_of(i), 0)),
            pl.BlockSpec((1, WINDOW, KV_W), lambda i, l: (seq_of(i), 0, 0)),
            pl.BlockSpec((1, WINDOW, KV_W), lambda i, l: (seq_of(i), 0, 0)),
            pl.BlockSpec((1, CONV_K - 1, CONV_W), lambda i, l: (seq_of(i), 0, 0)),
            pl.BlockSpec((rows, D_MODEL), lambda i, l: (0, 0)),
            pl.BlockSpec((rows, KV_W), lambda i, l: (0, 0)),
            pl.BlockSpec((rows, KV_W), lambda i, l: (0, 0)),
            pl.BlockSpec((nbs, CONV_K - 1, CONV_W), lambda i, l: (0, 0, 0)),
        ],
        scratch_shapes=[
            pltpu.VMEM((N_KV_HEADS, WINDOW + tile, GQ_W), BF16),
            pltpu.VMEM((N_KV_HEADS, WINDOW + tile, 2 * LANES), BF16),
            pltpu.VMEM((CONV_PAD + tile, CONV_W), F32),
            pltpu.VMEM((tile, ATT_W), BF16),
            pltpu.VMEM((tile, ATT_W), F32),
            pltpu.VMEM((nbs, CONV_PAD + nt, CONV_W), F32),
            pltpu.VMEM((rows, ATT_W), F32),
            pltpu.VMEM((rows, CONV_W), F32),
            pltpu.VMEM((rows, X_W), F32),
        ],
    )
    return pl.pallas_call(
        kern,
        grid_spec=grid_spec,
        out_shape=[
            jax.ShapeDtypeStruct((nb, seq, D_MODEL), F32),
            jax.ShapeDtypeStruct((nb, WINDOW, KV_W), F32),
            jax.ShapeDtypeStruct((nb, WINDOW, KV_W), F32),
            jax.ShapeDtypeStruct((nb, CONV_K - 1, CONV_W), F32),
            jax.ShapeDtypeStruct((rows, D_MODEL), F32),
            jax.ShapeDtypeStruct((rows, KV_W), F32),
            jax.ShapeDtypeStruct((rows, KV_W), F32),
            jax.ShapeDtypeStruct((nbs, CONV_K - 1, CONV_W), F32),
        ],
        compiler_params=pltpu.CompilerParams(
            dimension_semantics=("arbitrary",), vmem_limit_bytes=VMEM_LIMIT),
        name="layer",
    )(layer, x, g, win, wpa, wpb, wpx, wout, cw, cb, fg, bias, bd, mkb, mvx,
      xs, sbias, sbd, ck, cv, cc, cmk, cmv)


def _alibi_slopes():
    return np.power(2.0, -8.0 * np.arange(1, N_HEADS + 1, dtype=np.float32) / N_HEADS).astype(np.float32)


def _alibi_bias(nq, nk, masked):
    dist = np.abs(np.arange(nq)[:, None] + WINDOW - np.arange(nk)[None, :]).astype(np.float32)
    slope = _alibi_slopes().reshape(N_KV_HEADS, GROUP)
    bias = -(slope[:, :, None, None] * dist[None, None])
    bias = np.where(np.arange(nk)[None, None, None, :] < masked, -np.inf, bias)
    return bias.reshape(N_KV_HEADS, GROUP * nq, nk).astype(np.float32)


def _bias_with_sink(alibi, attn_sink, nq):
    nk = alibi.shape[-1]
    lead = alibi.shape[:-3]
    sink = jnp.repeat(attn_sink.reshape(DEPTH, N_KV_HEADS, GROUP), nq, axis=2)[..., None]
    sink = jnp.broadcast_to(sink.reshape((DEPTH,) + (1,) * len(lead) + sink.shape[1:]),
                            (DEPTH,) + alibi.shape[:-1] + (1,))
    real = jnp.broadcast_to(jnp.asarray(alibi), (DEPTH,) + alibi.shape)
    unused = jnp.full((DEPTH,) + alibi.shape[:-1] + (KEY_SLOTS - nk - 1,), -jnp.inf, F32)
    return jnp.concatenate([real, sink, unused], axis=-1)


def _block_diag_mask(nq):
    rows = np.arange(GROUP * nq)[:, None] // nq
    cols = np.arange(GQ_W)[None, :] // HEAD_DIM
    return (rows == cols).astype(np.float32)


def kernel(x_prompt, x_sample, cache_attn_k, cache_attn_v, cache_conv, cache_mem_k, cache_mem_v,
           mem_prompt, norm_g, w_in, attn_sink, w_pa, conv_w, conv_b, w_pb, mem_norm_g,
           w_mk, w_mv, w_px, w_out, final_g):
    bp, seq, _ = x_prompt.shape
    bs, nt, _ = x_sample.shape
    win_len = cache_attn_k.shape[2]
    assert win_len == WINDOW and nt <= CHUNK and seq >= WINDOW

    w_in_b, w_pa_b, w_pb_b, w_px_b, w_out_b = (w.astype(BF16) for w in (w_in, w_pa, w_pb, w_px, w_out))

    bias_p = _bias_with_sink(np.stack([_alibi_bias(CHUNK, BAND, m) for m in (WINDOW, CHUNK, 0)]), attn_sink, CHUNK)
    bias_s = _bias_with_sink(_alibi_bias(nt, WINDOW + nt, 0), attn_sink, nt)
    bd_p = jnp.asarray(_block_diag_mask(CHUNK), BF16)
    bd_s = jnp.asarray(_block_diag_mask(nt), BF16)
    fg = final_g.reshape(1, D_MODEL)

    mk, mv, mkb, mvx = _memory_kv(mem_prompt, mem_norm_g, w_mk, w_mv)

    xp = x_prompt
    xs = x_sample.reshape(bs * nt, D_MODEL)
    ck = cache_attn_k.reshape(DEPTH, bs, WINDOW, KV_W)
    cv = cache_attn_v.reshape(DEPTH, bs, WINDOW, KV_W)
    shared = (norm_g.reshape(DEPTH, 1, D_MODEL), w_in_b, w_pa_b, w_pb_b, w_px_b, w_out_b,
              conv_w, conv_b.reshape(DEPTH, 1, CONV_W), fg)
    cmk = cache_mem_k.reshape(DEPTH, bs, N_MEM * X_HEADS, X_HEAD_DIM)
    cmv = cache_mem_v.reshape(DEPTH, bs, N_MEM * X_HEADS, X_HEAD_DIM)
    pk, pv, pc, sk, sv, sc = [], [], [], [], [], []
    for l in range(DEPTH):
        layer = jnp.full((1,), l, jnp.int32)
        xp, kl, vl, ul, xs, ko, vo, uo = _layer(
            layer, xp, xs, *shared, bias_p, bd_p, mkb, mvx, bias_s, bd_s, ck, cv, cache_conv, cmk, cmv,
            nt=nt, final=l == DEPTH - 1)
        pk.append(kl)
        pv.append(vl)
        pc.append(ul)
        sk.append(ko)
        sv.append(vo)
        sc.append(uo)

    kv_p = (DEPTH, bp, WINDOW, N_KV_HEADS, HEAD_DIM)
    kv_s = (DEPTH, bs, nt, N_KV_HEADS, HEAD_DIM)
    mem_shape = (DEPTH, bp, N_MEM, X_HEADS, X_HEAD_DIM)
    return (xp, xs.reshape(bs, nt, D_MODEL),
            jnp.stack(pk).reshape(kv_p), jnp.stack(pv).reshape(kv_p), jnp.stack(pc),
            mk.reshape(mem_shape), mv.reshape(mem_shape),
            jnp.stack(sk).reshape(kv_s), jnp.stack(sv).reshape(kv_s), jnp.stack(sc))
```

```python
import functools

import numpy as np
import jax
import jax.numpy as jnp
from jax import lax
from jax.experimental import pallas as pl
from jax.experimental.pallas import tpu as pltpu

D_MODEL = 1024
DEPTH = 4
CHUNK = 64
WINDOW = 128
N_HEADS = 8
N_KV_HEADS = 2
GROUP = N_HEADS // N_KV_HEADS
HEAD_DIM = 64
ATT_W = N_HEADS * HEAD_DIM
KV_W = N_KV_HEADS * HEAD_DIM
CONV_W = 512
CONV_K = 3
X_HEADS = 4
X_HEAD_DIM = 128
X_W = X_HEADS * X_HEAD_DIM
N_MEM = 256
EPS = 1e-6

OFF_Q = 0
OFF_K = OFF_Q + ATT_W
OFF_V = OFF_K + KV_W
OFF_GA = OFF_V + KV_W
OFF_BB = OFF_GA + ATT_W
OFF_CC = OFF_BB + CONV_W
OFF_HH = OFF_CC + CONV_W
OFF_GB = OFF_HH + CONV_W
OFF_XQ = OFF_GB + CONV_W
OFF_GX = OFF_XQ + X_W
OFF_MG = OFF_GX + X_W
IN_W = OFF_MG + 3 * D_MODEL

LANES = 128
GQ_W = GROUP * HEAD_DIM
BAND = WINDOW + CHUNK
KEY_SLOTS = 256
CONV_PAD = 8
PROMPT_TILE = 512
Q_ROWS = 128
CAST_STEPS = 16
VMEM_LIMIT = 60 * 1024 * 1024

F32 = jnp.float32
BF16 = jnp.bfloat16
_NT = (((1,), (1,)), ((), ()))


def _rms_scale(x):
    return x * lax.rsqrt(jnp.mean(x * x, axis=-1, keepdims=True) + EPS)


def _silu(x):
    return x * jax.nn.sigmoid(x)


def _dot(a, b):
    return jnp.dot(a, b, preferred_element_type=F32)


def _dup_heads(a):
    lane = lax.broadcasted_iota(jnp.int32, a.shape, 1)
    sw = pltpu.roll(a, HEAD_DIM, 1)
    lo = lane < HEAD_DIM
    return jnp.where(lo, a, sw), jnp.where(lo, sw, a)


def _key_tails(nk):
    shape = (KEY_SLOTS - nk, 2 * LANES)
    row = lax.broadcasted_iota(jnp.int32, shape, 0)
    col = lax.broadcasted_iota(jnp.int32, shape, 1)
    vtail = jnp.where((row == 0) & (col >= LANES), 1.0, 0.0).astype(BF16)
    return jnp.zeros(shape, BF16), vtail


def _band_attention(qbd, kb, vb, bias):
    s = lax.dot_general(qbd, kb, _NT, preferred_element_type=F32) + bias
    p = jnp.exp(s - jnp.max(s, axis=-1, keepdims=True)).astype(BF16)
    oe = _dot(p, vb)
    return oe[:, 0:LANES] / oe[:, LANES:2 * LANES]


def _ungroup(on, nq):
    lo = lax.broadcasted_iota(jnp.int32, (nq, LANES), 1) < HEAD_DIM
    return (jnp.where(lo, on[0:nq], on[nq:2 * nq]),
            jnp.where(lo, on[2 * nq:3 * nq], on[3 * nq:4 * nq]))


def _cross_head(xq_h, mk_h, mvx_h):
    s = lax.dot_general(xq_h, mk_h, _NT, preferred_element_type=F32)
    p = jnp.exp(s - jnp.max(s, axis=-1, keepdims=True)).astype(BF16)
    oe = _dot(p, mvx_h)
    return oe[:, 0:X_HEAD_DIM] / oe[:, X_HEAD_DIM:2 * X_HEAD_DIM]


def _merge_out(x, xn, z_terms, win_ref, wp_refs, wout_ref):
    acc = None
    for i, (z, wp_ref) in enumerate(zip(z_terms, wp_refs)):
        gate = jax.nn.sigmoid(_dot(xn, win_ref[:, OFF_MG + i * D_MODEL:OFF_MG + (i + 1) * D_MODEL]))
        term = gate * _dot(z.astype(BF16), wp_ref[...])
        acc = term if acc is None else acc + term
    return x + _dot(acc.astype(BF16), wout_ref[...])


def _memkv_kernel(mem_ref, g_ref, wk_ref, wv_ref, mk_ref, mv_ref, mkb_ref, mvx_ref):
    mn = (_rms_scale(mem_ref[0]) * g_ref[0]).astype(BF16)
    mk = _dot(mn, wk_ref[0].astype(BF16))
    mv = _dot(mn, wv_ref[0].astype(BF16))
    mkb_ref[0, 0] = mk.astype(BF16)
    ones = jnp.ones((N_MEM, X_HEAD_DIM), BF16)
    for h in range(X_HEADS):
        cols = slice(h * X_HEAD_DIM, (h + 1) * X_HEAD_DIM)
        mk_ref[0, 0, pl.ds(h, N_MEM, stride=X_HEADS), :] = mk[:, cols]
        mv_ref[0, 0, pl.ds(h, N_MEM, stride=X_HEADS), :] = mv[:, cols]
        mvx_ref[0, 0, h, :, 0:X_HEAD_DIM] = mv[:, cols].astype(BF16)
        mvx_ref[0, 0, h, :, X_HEAD_DIM:2 * X_HEAD_DIM] = ones


def _memory_kv(mem, g, wk, wv):
    nb = mem.shape[0]
    return pl.pallas_call(
        _memkv_kernel,
        grid=(DEPTH, nb),
        in_specs=[
            pl.BlockSpec((1, N_MEM, D_MODEL), lambda l, b: (b, 0, 0)),
            pl.BlockSpec((1, 1, D_MODEL), lambda l, b: (l, 0, 0)),
            pl.BlockSpec((1, D_MODEL, X_W), lambda l, b: (l, 0, 0)),
            pl.BlockSpec((1, D_MODEL, X_W), lambda l, b: (l, 0, 0)),
        ],
        out_specs=[
            pl.BlockSpec((1, 1, N_MEM * X_HEADS, X_HEAD_DIM), lambda l, b: (l, b, 0, 0)),
            pl.BlockSpec((1, 1, N_MEM * X_HEADS, X_HEAD_DIM), lambda l, b: (l, b, 0, 0)),
            pl.BlockSpec((1, 1, N_MEM, X_W), lambda l, b: (l, b, 0, 0)),
            pl.BlockSpec((1, 1, X_HEADS, N_MEM, 2 * X_HEAD_DIM), lambda l, b: (l, b, 0, 0, 0)),
        ],
        out_shape=[
            jax.ShapeDtypeStruct((DEPTH, nb, N_MEM * X_HEADS, X_HEAD_DIM), F32),
            jax.ShapeDtypeStruct((DEPTH, nb, N_MEM * X_HEADS, X_HEAD_DIM), F32),
            jax.ShapeDtypeStruct((DEPTH, nb, N_MEM, X_W), BF16),
            jax.ShapeDtypeStruct((DEPTH, nb, X_HEADS, N_MEM, 2 * X_HEAD_DIM), BF16),
        ],
        name="memory_kv",
    )(mem, g.reshape(DEPTH, 1, D_MODEL), wk, wv)


def _prompt_tile(t, x_ref, g_ref, win_ref, wpa_ref, wpb_ref, wpx_ref, wout_ref, cw_ref, cb_ref,
                 bias_ref, bd_ref, mk_ref, mvx_ref, fg_ref,
                 xo_ref, kl_ref, vl_ref, ul_ref,
                 kr_ref, vv_ref, uc_ref, q_ref, attn_ref, *, tile, final):
    @pl.when(t == 0)
    def _():
        kr_ref[:, 0:WINDOW, :] = jnp.zeros((N_KV_HEADS, WINDOW, GQ_W), BF16)
        vv_ref[:, 0:WINDOW, 0:LANES] = jnp.zeros((N_KV_HEADS, WINDOW, LANES), BF16)
        vv_ref[:, :, LANES:2 * LANES] = jnp.ones((N_KV_HEADS, WINDOW + tile, LANES), BF16)
        uc_ref[0:CONV_PAD, :] = jnp.zeros((CONV_PAD, CONV_W), F32)

    x = x_ref[0]
    xn = (_rms_scale(x) * g_ref[...]).astype(BF16)

    def proj(off, width):
        return _dot(xn, win_ref[:, off:off + width])

    w_q = win_ref[:, OFF_Q:OFF_K]
    for r in range(0, tile, Q_ROWS):
        q_ref[r:r + Q_ROWS, :] = (_dot(xn[r:r + Q_ROWS], w_q) * (HEAD_DIM ** -0.5)).astype(BF16)
    half = tile // 2
    w_kv = win_ref[:, OFF_K:OFF_GA]
    kv = jnp.concatenate([_dot(xn[0:half], w_kv), _dot(xn[half:tile], w_kv)], axis=0)
    k = kv[:, 0:KV_W]
    v = kv[:, KV_W:2 * KV_W]
    ga = proj(OFF_GA, ATT_W)
    kl_ref[0] = k[tile - WINDOW:tile]
    vl_ref[0] = v[tile - WINDOW:tile]
    for j, (kj, vj) in enumerate(zip(_dup_heads(k), _dup_heads(v))):
        kj = kj.astype(BF16)
        kr_ref[j, WINDOW:WINDOW + tile, 0:LANES] = kj
        kr_ref[j, WINDOW:WINDOW + tile, LANES:2 * LANES] = kj
        vv_ref[j, WINDOW:WINDOW + tile, 0:LANES] = vj.astype(BF16)

    fillers = [functools.partial(proj, off, CONV_W) for off in (OFF_BB, OFF_CC, OFF_HH, OFF_GB)]
    fillers += [functools.partial(proj, off, X_W) for off in (OFF_XQ, OFF_GX)]
    fillers += [lambda i=i: jax.nn.sigmoid(proj(OFF_MG + i * D_MODEL, D_MODEL)) for i in (1, 2)]
    filled = []
    n_chunks = tile // CHUNK
    ktail, vtail = _key_tails(BAND)

    for c in range(n_chunks):
        r0 = c * CHUNK
        qc = q_ref[r0:r0 + CHUNK, :]
        variant = jnp.where(t == 0, c, 2) if c < 2 else 2
        for j in range(N_KV_HEADS):
            qg = qc[:, j * GQ_W:(j + 1) * GQ_W]
            qbd = jnp.concatenate([qg] * GROUP, axis=0) * bd_ref[...]
            on = _band_attention(qbd, jnp.concatenate([kr_ref[j, r0:r0 + BAND, :], ktail], axis=0),
                                 jnp.concatenate([vv_ref[j, r0:r0 + BAND, :], vtail], axis=0),
                                 bias_ref[variant, j])
            lo, hi = _ungroup(on, CHUNK)
            attn_ref[r0:r0 + CHUNK, j * GQ_W:j * GQ_W + LANES] = lo
            attn_ref[r0:r0 + CHUNK, j * GQ_W + LANES:(j + 1) * GQ_W] = hi
        while len(filled) * n_chunks < (c + 1) * len(fillers):
            filled.append(fillers[len(filled)]())
    bb, cc, hh, gb, xq, gx, gate_b, gate_x = filled
    kr_ref[:, 0:WINDOW, :] = kr_ref[:, tile:tile + WINDOW, :]
    vv_ref[:, 0:WINDOW, 0:LANES] = vv_ref[:, tile:tile + WINDOW, 0:LANES]

    u = cc * hh
    uc_ref[CONV_PAD:CONV_PAD + tile, :] = u
    ul_ref[0] = u[tile - (CONV_K - 1):tile]
    conv = (uc_ref[CONV_PAD - 2:CONV_PAD - 2 + tile, :] * cw_ref[0:1, :]
            + uc_ref[CONV_PAD - 1:CONV_PAD - 1 + tile, :] * cw_ref[1:2, :]
            + u * cw_ref[2:3, :]) + cb_ref[...]
    uc_ref[0:CONV_PAD, :] = uc_ref[tile:tile + CONV_PAD, :]
    zb = (bb * conv * _silu(gb)).astype(BF16)
    za = (attn_ref[...] * _silu(ga)).astype(BF16)

    xq = (xq * (X_HEAD_DIM ** -0.5)).astype(BF16)
    merge_work = [lambda: _dot(za, wpa_ref[...]),
                  lambda: jax.nn.sigmoid(proj(OFF_MG, D_MODEL)),
                  lambda: _dot(zb, wpb_ref[...])]
    merged = []
    heads = []
    for h in range(X_HEADS):
        heads.append(_cross_head(xq[:, h * X_HEAD_DIM:(h + 1) * X_HEAD_DIM],
                                 mk_ref[0, :, h * X_HEAD_DIM:(h + 1) * X_HEAD_DIM], mvx_ref[0, h]))
        if h < len(merge_work):
            merged.append(merge_work[h]())
    ya, gate_a, yb = merged
    zx = (jnp.concatenate(heads, axis=1) * _silu(gx)).astype(BF16)
    acc = gate_a * ya + gate_b * yb + gate_x * _dot(zx, wpx_ref[...])
    out = x + _dot(acc.astype(BF16), wout_ref[...])
    if final:
        out = _rms_scale(out) * fg_ref[...]
    xo_ref[0] = out


def _const_spec(shape):
    nd = len(shape)
    return pl.BlockSpec(shape, lambda *_: (0,) * nd, pipeline_mode=pl.Buffered(1))


def _layer_spec(shape):
    nd = len(shape)
    return pl.BlockSpec((None,) + tuple(shape), lambda *a: (a[-1][0],) + (0,) * nd, pipeline_mode=pl.Buffered(1))


def _sample_step(x_ref, g_ref, win_ref, wpa_ref, wpb_ref, wpx_ref, wout_ref, cw_ref, cb_ref,
                 bias_ref, bd_ref, ck_ref, cv_ref, cc_ref, cmk_ref, cmv_ref, fg_ref,
                 xo_ref, ko_ref, vo_ref, uo_ref,
                 uc_ref, attn_ref, conv_ref, xat_ref, *, nb, nt, final):
    x = x_ref[...]
    xn = (_rms_scale(x) * g_ref[...]).astype(BF16)

    p1 = _dot(xn, win_ref[:, OFF_Q:OFF_BB])
    q = (p1[:, OFF_Q:OFF_K] * (HEAD_DIM ** -0.5)).astype(BF16)
    k = p1[:, OFF_K:OFF_V]
    v = p1[:, OFF_V:OFF_GA]
    ga = p1[:, OFF_GA:OFF_BB]
    ko_ref[...] = k
    vo_ref[...] = v
    ones = jnp.ones((WINDOW + nt, LANES), BF16)
    ktail, vtail = _key_tails(WINDOW + nt)
    for b in range(nb):
        rows = slice(b * nt, (b + 1) * nt)
        kall = _dup_heads(jnp.concatenate([ck_ref[b], k[rows]], axis=0))
        vall = _dup_heads(jnp.concatenate([cv_ref[b], v[rows]], axis=0))
        for j in range(N_KV_HEADS):
            kj = kall[j].astype(BF16)
            vj = vall[j].astype(BF16)
            qg = q[rows, j * GQ_W:(j + 1) * GQ_W]
            qbd = jnp.concatenate([qg] * GROUP, axis=0) * bd_ref[...]
            on = _band_attention(qbd, jnp.concatenate([jnp.concatenate([kj, kj], axis=1), ktail], axis=0),
                                 jnp.concatenate([jnp.concatenate([vj, ones], axis=1), vtail], axis=0),
                                 bias_ref[j])
            lo, hi = _ungroup(on, nt)
            attn_ref[rows, j * GQ_W:j * GQ_W + LANES] = lo
            attn_ref[rows, j * GQ_W + LANES:(j + 1) * GQ_W] = hi
    za = attn_ref[...] * _silu(ga)

    p2 = _dot(xn, win_ref[:, OFF_BB:OFF_XQ])
    bb = p2[:, 0:CONV_W]
    u = p2[:, CONV_W:2 * CONV_W] * p2[:, 2 * CONV_W:3 * CONV_W]
    gb = p2[:, 3 * CONV_W:4 * CONV_W]
    for b in range(nb):
        rows = slice(b * nt, (b + 1) * nt)
        ub = u[rows]
        uc_ref[b, CONV_PAD - (CONV_K - 1):CONV_PAD, :] = cc_ref[b]
        uc_ref[b, CONV_PAD:CONV_PAD + nt, :] = ub
        uo_ref[b] = ub[nt - (CONV_K - 1):nt]
        conv_ref[rows, :] = (uc_ref[b, CONV_PAD - 2:CONV_PAD - 2 + nt, :] * cw_ref[0:1, :]
                             + uc_ref[b, CONV_PAD - 1:CONV_PAD - 1 + nt, :] * cw_ref[1:2, :]
                             + ub * cw_ref[2:3, :]) + cb_ref[...]
    zb = bb * conv_ref[...] * _silu(gb)

    p3 = _dot(xn, win_ref[:, OFF_XQ:OFF_MG])
    xq = (p3[:, 0:X_W] * (X_HEAD_DIM ** -0.5)).astype(BF16)
    gx = p3[:, X_W:2 * X_W]
    mones = jnp.ones((N_MEM, X_HEAD_DIM), BF16)
    for b in range(nb):
        rows = slice(b * nt, (b + 1) * nt)
        for h in range(X_HEADS):
            cols = slice(h * X_HEAD_DIM, (h + 1) * X_HEAD_DIM)
            head_rows = pl.ds(h, N_MEM, stride=X_HEADS)
            mvx = jnp.concatenate([cmv_ref[b, head_rows, :].astype(BF16), mones], axis=1)
            xat_ref[rows, cols] = _cross_head(xq[rows, cols], cmk_ref[b, head_rows, :].astype(BF16), mvx)
    zx = xat_ref[...] * _silu(gx)

    out = _merge_out(x, xn, (za, zb, zx), win_ref, (wpa_ref, wpb_ref, wpx_ref), wout_ref)
    if final:
        out = _rms_scale(out) * fg_ref[...]
    xo_ref[...] = out


def _prompt_layer_kernel(layer_ref, x_ref, g_ref, win32_ref, wpa32_ref, wpb32_ref, wpx32_ref, wout32_ref,
                         cw_ref, cb_ref, fg_ref, bias_ref, bd_ref, mk_ref, mvx_ref,
                         xo_ref, kl_ref, vl_ref, ul_ref, wino_ref, wpao_ref, wpbo_ref, wpxo_ref, wouto_ref,
                         win_ref, wpa_ref, wpb_ref, wpx_ref, wout_ref,
                         kr_ref, vv_ref, uc_ref, q_ref, attn_ref, *, tile, tiles_per_seq, final):
    i = pl.program_id(0)

    @pl.when(i < CAST_STEPS)
    def _():
        for src, dst, out in ((win32_ref, win_ref, wino_ref), (wpa32_ref, wpa_ref, wpao_ref),
                              (wpb32_ref, wpb_ref, wpbo_ref), (wpx32_ref, wpx_ref, wpxo_ref),
                              (wout32_ref, wout_ref, wouto_ref)):
            rows = src.shape[0]
            chunk = src[...].astype(BF16)
            dst[pl.ds(pl.multiple_of(i * rows, rows), rows), :] = chunk
            out[...] = chunk

    @pl.when(i >= CAST_STEPS)
    def _():
        _prompt_tile(lax.rem(i - CAST_STEPS, tiles_per_seq), x_ref, g_ref, win_ref, wpa_ref, wpb_ref, wpx_ref,
                     wout_ref, cw_ref, cb_ref, bias_ref, bd_ref, mk_ref, mvx_ref, fg_ref,
                     xo_ref, kl_ref, vl_ref, ul_ref, kr_ref, vv_ref, uc_ref, q_ref, attn_ref,
                     tile=tile, final=final)


def _sample_layer_kernel(layer_ref, *refs, nb, nt, final):
    _sample_step(*refs, nb=nb, nt=nt, final=final)


def _chunk_spec(rows, cols):
    assert rows % CAST_STEPS == 0
    return pl.BlockSpec((None, rows // CAST_STEPS, cols),
                        lambda i, l: (l[0], jnp.minimum(i, CAST_STEPS - 1), 0))


def _chunk_out_spec(rows, cols):
    return pl.BlockSpec((rows // CAST_STEPS, cols), lambda i, l: (jnp.minimum(i, CAST_STEPS - 1), 0))


_WEIGHT_SHAPES = ((D_MODEL, IN_W), (ATT_W, D_MODEL), (CONV_W, D_MODEL), (X_W, D_MODEL), (D_MODEL, D_MODEL))


def _prompt_layer(layer, x, g, win, wpa, wpb, wpx, wout, cw, cb, fg, bias, bd, mkb, mvx, *, final):
    nb, seq, _ = x.shape
    tile = PROMPT_TILE
    assert seq % tile == 0 and tile % WINDOW == 0
    tps = seq // tile
    kern = functools.partial(_prompt_layer_kernel, tile=tile, tiles_per_seq=tps, final=final)

    def seq_of(i):
        return jnp.maximum(i - CAST_STEPS, 0) // tps

    def tile_of(i):
        return jnp.maximum(i - CAST_STEPS, 0) % tps

    grid_spec = pltpu.PrefetchScalarGridSpec(
        num_scalar_prefetch=1,
        grid=(CAST_STEPS + nb * tps,),
        in_specs=[
            pl.BlockSpec((1, tile, D_MODEL), lambda i, l: (seq_of(i), tile_of(i), 0)),
            _layer_spec((1, D_MODEL)),
            *[_chunk_spec(r, c) for r, c in _WEIGHT_SHAPES],
            _layer_spec((CONV_K, CONV_W)),
            _layer_spec((1, CONV_W)),
            _const_spec((1, D_MODEL)),
            _layer_spec((3, N_KV_HEADS, GROUP * CHUNK, KEY_SLOTS)),
            _const_spec((GROUP * CHUNK, GQ_W)),
            pl.BlockSpec((None, 1, N_MEM, X_W), lambda i, l: (l[0], seq_of(i), 0, 0)),
            pl.BlockSpec((None, 1, X_HEADS, N_MEM, 2 * X_HEAD_DIM), lambda i, l: (l[0], seq_of(i), 0, 0, 0)),
        ],
        out_specs=[
            pl.BlockSpec((1, tile, D_MODEL), lambda i, l: (seq_of(i), tile_of(i), 0)),
            pl.BlockSpec((1, WINDOW, KV_W), lambda i, l: (seq_of(i), 0, 0)),
            pl.BlockSpec((1, WINDOW, KV_W), lambda i, l: (seq_of(i), 0, 0)),
            pl.BlockSpec((1, CONV_K - 1, CONV_W), lambda i, l: (seq_of(i), 0, 0)),
            *[_chunk_out_spec(r, c) for r, c in _WEIGHT_SHAPES],
        ],
        scratch_shapes=[
            *[pltpu.VMEM(shape, BF16) for shape in _WEIGHT_SHAPES],
            pltpu.VMEM((N_KV_HEADS, WINDOW + tile, GQ_W), BF16),
            pltpu.VMEM((N_KV_HEADS, WINDOW + tile, 2 * LANES), BF16),
            pltpu.VMEM((CONV_PAD + tile, CONV_W), F32),
            pltpu.VMEM((tile, ATT_W), BF16),
            pltpu.VMEM((tile, ATT_W), F32),
        ],
    )
    return pl.pallas_call(
        kern,
        grid_spec=grid_spec,
        out_shape=[
            jax.ShapeDtypeStruct((nb, seq, D_MODEL), F32),
            jax.ShapeDtypeStruct((nb, WINDOW, KV_W), F32),
            jax.ShapeDtypeStruct((nb, WINDOW, KV_W), F32),
            jax.ShapeDtypeStruct((nb, CONV_K - 1, CONV_W), F32),
            *[jax.ShapeDtypeStruct(shape, BF16) for shape in _WEIGHT_SHAPES],
        ],
        compiler_params=pltpu.CompilerParams(
            dimension_semantics=("arbitrary",), vmem_limit_bytes=VMEM_LIMIT),
        name="prompt_layer",
    )(layer, x, g, win, wpa, wpb, wpx, wout, cw, cb, fg, bias, bd, mkb, mvx)


def _sample_layer(layer, xs, g, weights_b, cw, cb, fg, sbias, sbd, ck, cv, cc, cmk, cmv, *, nt, final):
    rows = xs.shape[0]
    nbs = rows // nt
    kern = functools.partial(_sample_layer_kernel, nb=nbs, nt=nt, final=final)
    grid_spec = pltpu.PrefetchScalarGridSpec(
        num_scalar_prefetch=1,
        grid=(1,),
        in_specs=[
            _const_spec((rows, D_MODEL)),
            _layer_spec((1, D_MODEL)),
            *[_const_spec(shape) for shape in _WEIGHT_SHAPES],
            _layer_spec((CONV_K, CONV_W)),
            _layer_spec((1, CONV_W)),
            _layer_spec((N_KV_HEADS, GROUP * nt, KEY_SLOTS)),
            _const_spec((GROUP * nt, GQ_W)),
            _layer_spec((nbs, WINDOW, KV_W)),
            _layer_spec((nbs, WINDOW, KV_W)),
            _layer_spec((nbs, CONV_K - 1, CONV_W)),
            _layer_spec((nbs, N_MEM * X_HEADS, X_HEAD_DIM)),
            _layer_spec((nbs, N_MEM * X_HEADS, X_HEAD_DIM)),
            _const_spec((1, D_MODEL)),
        ],
        out_specs=[
            pl.BlockSpec((rows, D_MODEL), lambda i, l: (0, 0)),
            pl.BlockSpec((rows, KV_W), lambda i, l: (0, 0)),
            pl.BlockSpec((rows, KV_W), lambda i, l: (0, 0)),
            pl.BlockSpec((nbs, CONV_K - 1, CONV_W), lambda i, l: (0, 0, 0)),
        ],
        scratch_shapes=[
            pltpu.VMEM((nbs, CONV_PAD + nt, CONV_W), F32),
            pltpu.VMEM((rows, ATT_W), F32),
            pltpu.VMEM((rows, CONV_W), F32),
            pltpu.VMEM((rows, X_W), F32),
        ],
    )
    return pl.pallas_call(
        kern,
        grid_spec=grid_spec,
        out_shape=[
            jax.ShapeDtypeStruct((rows, D_MODEL), F32),
            jax.ShapeDtypeStruct((rows, KV_W), F32),
            jax.ShapeDtypeStruct((rows, KV_W), F32),
            jax.ShapeDtypeStruct((nbs, CONV_K - 1, CONV_W), F32),
        ],
        compiler_params=pltpu.CompilerParams(
            dimension_semantics=("arbitrary",), vmem_limit_bytes=VMEM_LIMIT),
        name="sample_layer",
    )(layer, xs, g, *weights_b, cw, cb, sbias, sbd, ck, cv, cc, cmk, cmv, fg)


def _alibi_slopes():
    return np.power(2.0, -8.0 * np.arange(1, N_HEADS + 1, dtype=np.float32) / N_HEADS).astype(np.float32)


def _alibi_bias(nq, nk, masked):
    dist = np.abs(np.arange(nq)[:, None] + WINDOW - np.arange(nk)[None, :]).astype(np.float32)
    slope = _alibi_slopes().reshape(N_KV_HEADS, GROUP)
    bias = -(slope[:, :, None, None] * dist[None, None])
    bias = np.where(np.arange(nk)[None, None, None, :] < masked, -np.inf, bias)
    return bias.reshape(N_KV_HEADS, GROUP * nq, nk).astype(np.float32)


def _bias_with_sink(alibi, attn_sink, nq):
    nk = alibi.shape[-1]
    lead = alibi.shape[:-3]
    sink = jnp.repeat(attn_sink.reshape(DEPTH, N_KV_HEADS, GROUP), nq, axis=2)[..., None]
    sink = jnp.broadcast_to(sink.reshape((DEPTH,) + (1,) * len(lead) + sink.shape[1:]),
                            (DEPTH,) + alibi.shape[:-1] + (1,))
    real = jnp.broadcast_to(jnp.asarray(alibi), (DEPTH,) + alibi.shape)
    unused = jnp.full((DEPTH,) + alibi.shape[:-1] + (KEY_SLOTS - nk - 1,), -jnp.inf, F32)
    return jnp.concatenate([real, sink, unused], axis=-1)


def _block_diag_mask(nq):
    rows = np.arange(GROUP * nq)[:, None] // nq
    cols = np.arange(GQ_W)[None, :] // HEAD_DIM
    return (rows == cols).astype(np.float32)


def kernel(x_prompt, x_sample, cache_attn_k, cache_attn_v, cache_conv, cache_mem_k, cache_mem_v,
           mem_prompt, norm_g, w_in, attn_sink, w_pa, conv_w, conv_b, w_pb, mem_norm_g,
           w_mk, w_mv, w_px, w_out, final_g):
    bp, seq, _ = x_prompt.shape
    bs, nt, _ = x_sample.shape
    win_len = cache_attn_k.shape[2]
    assert win_len == WINDOW and nt <= CHUNK and seq >= WINDOW

    bias_p = _bias_with_sink(np.stack([_alibi_bias(CHUNK, BAND, m) for m in (WINDOW, CHUNK, 0)]), attn_sink, CHUNK)
    bias_s = _bias_with_sink(_alibi_bias(nt, WINDOW + nt, 0), attn_sink, nt)
    bd_p = jnp.asarray(_block_diag_mask(CHUNK), BF16)
    bd_s = jnp.asarray(_block_diag_mask(nt), BF16)
    fg = final_g.reshape(1, D_MODEL)

    mk, mv, mkb, mvx = _memory_kv(mem_prompt, mem_norm_g, w_mk, w_mv)

    xp = x_prompt
    xs = x_sample.reshape(bs * nt, D_MODEL)
    ck = cache_attn_k.reshape(DEPTH, bs, WINDOW, KV_W)
    cv = cache_attn_v.reshape(DEPTH, bs, WINDOW, KV_W)
    g = norm_g.reshape(DEPTH, 1, D_MODEL)
    cb = conv_b.reshape(DEPTH, 1, CONV_W)
    cmk = cache_mem_k.reshape(DEPTH, bs, N_MEM * X_HEADS, X_HEAD_DIM)
    cmv = cache_mem_v.reshape(DEPTH, bs, N_MEM * X_HEADS, X_HEAD_DIM)
    pk, pv, pc, sk, sv, sc = [], [], [], [], [], []
    for l in range(DEPTH):
        final = l == DEPTH - 1
        layer = jnp.full((1,), l, jnp.int32)
        xp, kl, vl, ul, *weights_b = _prompt_layer(layer, xp, g, w_in, w_pa, w_pb, w_px, w_out, conv_w, cb, fg,
                                                   bias_p, bd_p, mkb, mvx, final=final)
        xs, ko, vo, uo = _sample_layer(layer, xs, g, weights_b, conv_w, cb, fg, bias_s, bd_s, ck, cv,
                                       cache_conv, cmk, cmv, nt=nt, final=final)
        pk.append(kl)
        pv.append(vl)
        pc.append(ul)
        sk.append(ko)
        sv.append(vo)
        sc.append(uo)

    kv_p = (DEPTH, bp, WINDOW, N_KV_HEADS, HEAD_DIM)
    kv_s = (DEPTH, bs, nt, N_KV_HEADS, HEAD_DIM)
    mem_shape = (DEPTH, bp, N_MEM, X_HEADS, X_HEAD_DIM)
    return (xp, xs.reshape(bs, nt, D_MODEL),
            jnp.stack(pk).reshape(kv_p), jnp.stack(pv).reshape(kv_p), jnp.stack(pc),
            mk.reshape(mem_shape), mv.reshape(mem_shape),
            jnp.stack(sk).reshape(kv_s), jnp.stack(sv).reshape(kv_s), jnp.stack(sc))
```

```python
import functools

import numpy as np
import jax
import jax.numpy as jnp
from jax import lax
from jax.experimental import pallas as pl
from jax.experimental.pallas import tpu as pltpu

D_MODEL = 1024
DEPTH = 4
CHUNK = 64
WINDOW = 128
N_HEADS = 8
N_KV_HEADS = 2
GROUP = N_HEADS // N_KV_HEADS
HEAD_DIM = 64
ATT_W = N_HEADS * HEAD_DIM
KV_W = N_KV_HEADS * HEAD_DIM
CONV_W = 512
CONV_K = 3
X_HEADS = 4
X_HEAD_DIM = 128
X_W = X_HEADS * X_HEAD_DIM
N_MEM = 256
EPS = 1e-6

OFF_Q = 0
OFF_K = OFF_Q + ATT_W
OFF_V = OFF_K + KV_W
OFF_GA = OFF_V + KV_W
OFF_BB = OFF_GA + ATT_W
OFF_CC = OFF_BB + CONV_W
OFF_HH = OFF_CC + CONV_W
OFF_GB = OFF_HH + CONV_W
OFF_XQ = OFF_GB + CONV_W
OFF_GX = OFF_XQ + X_W
OFF_MG = OFF_GX + X_W
IN_W = OFF_MG + 3 * D_MODEL

LANES = 128
GQ_W = GROUP * HEAD_DIM
BAND = WINDOW + CHUNK
KEY_SLOTS = 256
CONV_PAD = 8
PROMPT_TILE = 512
Q_ROWS = 128
CAST_STEPS = 16
VMEM_LIMIT = 60 * 1024 * 1024

F32 = jnp.float32
BF16 = jnp.bfloat16
_NT = (((1,), (1,)), ((), ()))


def _rms_scale(x):
    return x * lax.rsqrt(jnp.mean(x * x, axis=-1, keepdims=True) + EPS)


def _silu(x):
    return x * jax.nn.sigmoid(x)


def _dot(a, b):
    return jnp.dot(a, b, preferred_element_type=F32)


def _dup_heads(a):
    lane = lax.broadcasted_iota(jnp.int32, a.shape, 1)
    sw = pltpu.roll(a, HEAD_DIM, 1)
    lo = lane < HEAD_DIM
    return jnp.where(lo, a, sw), jnp.where(lo, sw, a)


def _key_tails(nk):
    shape = (KEY_SLOTS - nk, 2 * LANES)
    row = lax.broadcasted_iota(jnp.int32, shape, 0)
    col = lax.broadcasted_iota(jnp.int32, shape, 1)
    vtail = jnp.where((row == 0) & (col >= LANES), 1.0, 0.0).astype(BF16)
    return jnp.zeros(shape, BF16), vtail


def _band_attention(qbd, kb, vb, bias):
    s = lax.dot_general(qbd, kb, _NT, preferred_element_type=F32) + bias
    p = jnp.exp(s - jnp.max(s, axis=-1, keepdims=True)).astype(BF16)
    oe = _dot(p, vb)
    return oe[:, 0:LANES] / oe[:, LANES:2 * LANES]


def _ungroup(on, nq):
    lo = lax.broadcasted_iota(jnp.int32, (nq, LANES), 1) < HEAD_DIM
    return (jnp.where(lo, on[0:nq], on[nq:2 * nq]),
            jnp.where(lo, on[2 * nq:3 * nq], on[3 * nq:4 * nq]))


def _cross_head(xq_h, mk_h, mvx_h):
    s = lax.dot_general(xq_h, mk_h, _NT, preferred_element_type=F32)
    p = jnp.exp(s - jnp.max(s, axis=-1, keepdims=True)).astype(BF16)
    oe = _dot(p, mvx_h)
    return oe[:, 0:X_HEAD_DIM] / oe[:, X_HEAD_DIM:2 * X_HEAD_DIM]


def _merge_out(x, xn, z_terms, win_ref, wp_refs, wout_ref):
    acc = None
    for i, (z, wp_ref) in enumerate(zip(z_terms, wp_refs)):
        gate = jax.nn.sigmoid(_dot(xn, win_ref[:, OFF_MG + i * D_MODEL:OFF_MG + (i + 1) * D_MODEL]))
        term = gate * _dot(z.astype(BF16), wp_ref[...])
        acc = term if acc is None else acc + term
    return x + _dot(acc.astype(BF16), wout_ref[...])


def _memkv_kernel(mem_ref, g_ref, wk_ref, wv_ref, mk_ref, mv_ref, mkb_ref, mvx_ref):
    mn = (_rms_scale(mem_ref[0]) * g_ref[0]).astype(BF16)
    mk = _dot(mn, wk_ref[0].astype(BF16))
    mv = _dot(mn, wv_ref[0].astype(BF16))
    mkb_ref[0, 0] = mk.astype(BF16)
    ones = jnp.ones((N_MEM, X_HEAD_DIM), BF16)
    for h in range(X_HEADS):
        cols = slice(h * X_HEAD_DIM, (h + 1) * X_HEAD_DIM)
        mk_ref[0, 0, pl.ds(h, N_MEM, stride=X_HEADS), :] = mk[:, cols]
        mv_ref[0, 0, pl.ds(h, N_MEM, stride=X_HEADS), :] = mv[:, cols]
        mvx_ref[0, 0, h, :, 0:X_HEAD_DIM] = mv[:, cols].astype(BF16)
        mvx_ref[0, 0, h, :, X_HEAD_DIM:2 * X_HEAD_DIM] = ones


def _memory_kv(mem, g, wk, wv):
    nb = mem.shape[0]
    return pl.pallas_call(
        _memkv_kernel,
        grid=(DEPTH, nb),
        in_specs=[
            pl.BlockSpec((1, N_MEM, D_MODEL), lambda l, b: (b, 0, 0)),
            pl.BlockSpec((1, 1, D_MODEL), lambda l, b: (l, 0, 0)),
            pl.BlockSpec((1, D_MODEL, X_W), lambda l, b: (l, 0, 0)),
            pl.BlockSpec((1, D_MODEL, X_W), lambda l, b: (l, 0, 0)),
        ],
        out_specs=[
            pl.BlockSpec((1, 1, N_MEM * X_HEADS, X_HEAD_DIM), lambda l, b: (l, b, 0, 0)),
            pl.BlockSpec((1, 1, N_MEM * X_HEADS, X_HEAD_DIM), lambda l, b: (l, b, 0, 0)),
            pl.BlockSpec((1, 1, N_MEM, X_W), lambda l, b: (l, b, 0, 0)),
            pl.BlockSpec((1, 1, X_HEADS, N_MEM, 2 * X_HEAD_DIM), lambda l, b: (l, b, 0, 0, 0)),
        ],
        out_shape=[
            jax.ShapeDtypeStruct((DEPTH, nb, N_MEM * X_HEADS, X_HEAD_DIM), F32),
            jax.ShapeDtypeStruct((DEPTH, nb, N_MEM * X_HEADS, X_HEAD_DIM), F32),
            jax.ShapeDtypeStruct((DEPTH, nb, N_MEM, X_W), BF16),
            jax.ShapeDtypeStruct((DEPTH, nb, X_HEADS, N_MEM, 2 * X_HEAD_DIM), BF16),
        ],
        name="memory_kv",
    )(mem, g.reshape(DEPTH, 1, D_MODEL), wk, wv)


def _prompt_tile(t, x_ref, g_ref, win_ref, wpa_ref, wpb_ref, wpx_ref, wout_ref, cw_ref, cb_ref,
                 bias_ref, bd_ref, mk_ref, mvx_ref, fg_ref,
                 xo_ref, kl_ref, vl_ref, ul_ref,
                 kr_ref, vv_ref, uc_ref, q_ref, attn_ref, *, tile, final):
    @pl.when(t == 0)
    def _():
        kr_ref[:, 0:WINDOW, :] = jnp.zeros((N_KV_HEADS, WINDOW, GQ_W), BF16)
        vv_ref[:, 0:WINDOW, 0:LANES] = jnp.zeros((N_KV_HEADS, WINDOW, LANES), BF16)
        vv_ref[:, :, LANES:2 * LANES] = jnp.ones((N_KV_HEADS, WINDOW + tile, LANES), BF16)
        uc_ref[0:CONV_PAD, :] = jnp.zeros((CONV_PAD, CONV_W), F32)

    x = x_ref[0]
    xn = (_rms_scale(x) * g_ref[...]).astype(BF16)

    def proj(off, width):
        return _dot(xn, win_ref[:, off:off + width])

    w_q = win_ref[:, OFF_Q:OFF_K]
    for r in range(0, tile, Q_ROWS):
        q_ref[r:r + Q_ROWS, :] = (_dot(xn[r:r + Q_ROWS], w_q) * (HEAD_DIM ** -0.5)).astype(BF16)
    half = tile // 2
    w_kv = win_ref[:, OFF_K:OFF_GA]
    kv = jnp.concatenate([_dot(xn[0:half], w_kv), _dot(xn[half:tile], w_kv)], axis=0)
    k = kv[:, 0:KV_W]
    v = kv[:, KV_W:2 * KV_W]
    ga = proj(OFF_GA, ATT_W)
    kl_ref[0] = k[tile - WINDOW:tile]
    vl_ref[0] = v[tile - WINDOW:tile]
    for j, (kj, vj) in enumerate(zip(_dup_heads(k), _dup_heads(v))):
        kj = kj.astype(BF16)
        kr_ref[j, WINDOW:WINDOW + tile, 0:LANES] = kj
        kr_ref[j, WINDOW:WINDOW + tile, LANES:2 * LANES] = kj
        vv_ref[j, WINDOW:WINDOW + tile, 0:LANES] = vj.astype(BF16)

    fillers = [functools.partial(proj, off, CONV_W) for off in (OFF_BB, OFF_CC, OFF_HH, OFF_GB)]
    fillers += [functools.partial(proj, off, X_W) for off in (OFF_XQ, OFF_GX)]
    fillers += [lambda i=i: jax.nn.sigmoid(proj(OFF_MG + i * D_MODEL, D_MODEL)) for i in (1, 2)]
    filled = []
    n_chunks = tile // CHUNK
    ktail, vtail = _key_tails(BAND)

    for c in range(n_chunks):
        r0 = c * CHUNK
        qc = q_ref[r0:r0 + CHUNK, :]
        variant = jnp.where(t == 0, c, 2) if c < 2 else 2
        for j in range(N_KV_HEADS):
            qg = qc[:, j * GQ_W:(j + 1) * GQ_W]
            qbd = jnp.concatenate([qg] * GROUP, axis=0) * bd_ref[...]
            on = _band_attention(qbd, jnp.concatenate([kr_ref[j, r0:r0 + BAND, :], ktail], axis=0),
                                 jnp.concatenate([vv_ref[j, r0:r0 + BAND, :], vtail], axis=0),
                                 bias_ref[variant, j])
            lo, hi = _ungroup(on, CHUNK)
            attn_ref[r0:r0 + CHUNK, j * GQ_W:j * GQ_W + LANES] = lo
            attn_ref[r0:r0 + CHUNK, j * GQ_W + LANES:(j + 1) * GQ_W] = hi
        while len(filled) * n_chunks < (c + 1) * len(fillers):
            filled.append(fillers[len(filled)]())
    bb, cc, hh, gb, xq, gx, gate_b, gate_x = filled
    kr_ref[:, 0:WINDOW, :] = kr_ref[:, tile:tile + WINDOW, :]
    vv_ref[:, 0:WINDOW, 0:LANES] = vv_ref[:, tile:tile + WINDOW, 0:LANES]

    u = cc * hh
    uc_ref[CONV_PAD:CONV_PAD + tile, :] = u
    ul_ref[0] = u[tile - (CONV_K - 1):tile]
    conv = (uc_ref[CONV_PAD - 2:CONV_PAD - 2 + tile, :] * cw_ref[0:1, :]
            + uc_ref[CONV_PAD - 1:CONV_PAD - 1 + tile, :] * cw_ref[1:2, :]
            + u * cw_ref[2:3, :]) + cb_ref[...]
    uc_ref[0:CONV_PAD, :] = uc_ref[tile:tile + CONV_PAD, :]
    zb = (bb * conv * _silu(gb)).astype(BF16)
    za = (attn_ref[...] * _silu(ga)).astype(BF16)

    xq = (xq * (X_HEAD_DIM ** -0.5)).astype(BF16)
    merge_work = [lambda: _dot(za, wpa_ref[...]),
                  lambda: jax.nn.sigmoid(proj(OFF_MG, D_MODEL)),
                  lambda: _dot(zb, wpb_ref[...])]
    merged = []
    heads = []
    for h in range(X_HEADS):
        heads.append(_cross_head(xq[:, h * X_HEAD_DIM:(h + 1) * X_HEAD_DIM],
                                 mk_ref[0, :, h * X_HEAD_DIM:(h + 1) * X_HEAD_DIM], mvx_ref[0, h]))
        if h < len(merge_work):
            merged.append(merge_work[h]())
    ya, gate_a, yb = merged
    zx = (jnp.concatenate(heads, axis=1) * _silu(gx)).astype(BF16)
    acc = gate_a * ya + gate_b * yb + gate_x * _dot(zx, wpx_ref[...])
    out = x + _dot(acc.astype(BF16), wout_ref[...])
    if final:
        out = _rms_scale(out) * fg_ref[...]
    xo_ref[0] = out


def _const_spec(shape):
    nd = len(shape)
    return pl.BlockSpec(shape, lambda *_: (0,) * nd, pipeline_mode=pl.Buffered(1))


def _layer_spec(shape):
    nd = len(shape)
    return pl.BlockSpec((None,) + tuple(shape), lambda *a: (a[-1][0],) + (0,) * nd, pipeline_mode=pl.Buffered(1))


def _sample_step(x_ref, g_ref, win_ref, wpa_ref, wpb_ref, wpx_ref, wout_ref, cw_ref, cb_ref,
                 bias_ref, bd_ref, ck_ref, cv_ref, cc_ref, cmk_ref, cmv_ref, fg_ref,
                 xo_ref, ko_ref, vo_ref, uo_ref,
                 uc_ref, attn_ref, conv_ref, xat_ref, *, nb, nt, final):
    x = x_ref[...]
    xn = (_rms_scale(x) * g_ref[...]).astype(BF16)

    p1 = _dot(xn, win_ref[:, OFF_Q:OFF_BB])
    q = (p1[:, OFF_Q:OFF_K] * (HEAD_DIM ** -0.5)).astype(BF16)
    k = p1[:, OFF_K:OFF_V]
    v = p1[:, OFF_V:OFF_GA]
    ga = p1[:, OFF_GA:OFF_BB]
    ko_ref[...] = k
    vo_ref[...] = v
    ones = jnp.ones((WINDOW + nt, LANES), BF16)
    ktail, vtail = _key_tails(WINDOW + nt)
    for b in range(nb):
        rows = slice(b * nt, (b + 1) * nt)
        kall = _dup_heads(jnp.concatenate([ck_ref[b], k[rows]], axis=0))
        vall = _dup_heads(jnp.concatenate([cv_ref[b], v[rows]], axis=0))
        for j in range(N_KV_HEADS):
            kj = kall[j].astype(BF16)
            vj = vall[j].astype(BF16)
            qg = q[rows, j * GQ_W:(j + 1) * GQ_W]
            qbd = jnp.concatenate([qg] * GROUP, axis=0) * bd_ref[...]
            on = _band_attention(qbd, jnp.concatenate([jnp.concatenate([kj, kj], axis=1), ktail], axis=0),
                                 jnp.concatenate([jnp.concatenate([vj, ones], axis=1), vtail], axis=0),
                                 bias_ref[j])
            lo, hi = _ungroup(on, nt)
            attn_ref[rows, j * GQ_W:j * GQ_W + LANES] = lo
            attn_ref[rows, j * GQ_W + LANES:(j + 1) * GQ_W] = hi
    za = attn_ref[...] * _silu(ga)

    p2 = _dot(xn, win_ref[:, OFF_BB:OFF_XQ])
    bb = p2[:, 0:CONV_W]
    u = p2[:, CONV_W:2 * CONV_W] * p2[:, 2 * CONV_W:3 * CONV_W]
    gb = p2[:, 3 * CONV_W:4 * CONV_W]
    for b in range(nb):
        rows = slice(b * nt, (b + 1) * nt)
        ub = u[rows]
        uc_ref[b, CONV_PAD - (CONV_K - 1):CONV_PAD, :] = cc_ref[b]
        uc_ref[b, CONV_PAD:CONV_PAD + nt, :] = ub
        uo_ref[b] = ub[nt - (CONV_K - 1):nt]
        conv_ref[rows, :] = (uc_ref[b, CONV_PAD - 2:CONV_PAD - 2 + nt, :] * cw_ref[0:1, :]
                             + uc_ref[b, CONV_PAD - 1:CONV_PAD - 1 + nt, :] * cw_ref[1:2, :]
                             + ub * cw_ref[2:3, :]) + cb_ref[...]
    zb = bb * conv_ref[...] * _silu(gb)

    p3 = _dot(xn, win_ref[:, OFF_XQ:OFF_MG])
    xq = (p3[:, 0:X_W] * (X_HEAD_DIM ** -0.5)).astype(BF16)
    gx = p3[:, X_W:2 * X_W]
    mones = jnp.ones((N_MEM, X_HEAD_DIM), BF16)
    for b in range(nb):
        rows = slice(b * nt, (b + 1) * nt)
        for h in range(X_HEADS):
            cols = slice(h * X_HEAD_DIM, (h + 1) * X_HEAD_DIM)
            head_rows = pl.ds(h, N_MEM, stride=X_HEADS)
            mvx = jnp.concatenate([cmv_ref[b, head_rows, :].astype(BF16), mones], axis=1)
            xat_ref[rows, cols] = _cross_head(xq[rows, cols], cmk_ref[b, head_rows, :].astype(BF16), mvx)
    zx = xat_ref[...] * _silu(gx)

    out = _merge_out(x, xn, (za, zb, zx), win_ref, (wpa_ref, wpb_ref, wpx_ref), wout_ref)
    if final:
        out = _rms_scale(out) * fg_ref[...]
    xo_ref[...] = out


def _layer_kernel(layer_ref, x_ref, g_ref, win32_ref, wpa32_ref, wpb32_ref, wpx32_ref, wout32_ref,
                  cw_ref, cb_ref, fg_ref, bias_ref, bd_ref, mk_ref, mvx_ref,
                  xs_ref, sbias_ref, sbd_ref, ck_ref, cv_ref, cc_ref, cmk_ref, cmv_ref,
                  xo_ref, kl_ref, vl_ref, ul_ref, xso_ref, ko_ref, vo_ref, uo_ref,
                  win_ref, wpa_ref, wpb_ref, wpx_ref, wout_ref,
                  kr_ref, vv_ref, uc_ref, q_ref, attn_ref, suc_ref, sattn_ref, sconv_ref, sxat_ref,
                  *, tile, tiles_per_seq, prompt_steps, nb, nt, final):
    i = pl.program_id(0)
    weights = (win_ref, wpa_ref, wpb_ref, wpx_ref, wout_ref)

    @pl.when(i < CAST_STEPS)
    def _():
        for src, dst in zip((win32_ref, wpa32_ref, wpb32_ref, wpx32_ref, wout32_ref), weights):
            rows = src.shape[0]
            dst[pl.ds(pl.multiple_of(i * rows, rows), rows), :] = src[...].astype(BF16)

    @pl.when((i >= CAST_STEPS) & (i < CAST_STEPS + prompt_steps))
    def _():
        _prompt_tile(lax.rem(i - CAST_STEPS, tiles_per_seq), x_ref, g_ref, *weights, cw_ref, cb_ref,
                     bias_ref, bd_ref, mk_ref, mvx_ref, fg_ref,
                     xo_ref, kl_ref, vl_ref, ul_ref, kr_ref, vv_ref, uc_ref, q_ref, attn_ref,
                     tile=tile, final=final)

    @pl.when(i == CAST_STEPS + prompt_steps)
    def _():
        _sample_step(xs_ref, g_ref, *weights, cw_ref, cb_ref,
                     sbias_ref, sbd_ref, ck_ref, cv_ref, cc_ref, cmk_ref, cmv_ref, fg_ref,
                     xso_ref, ko_ref, vo_ref, uo_ref, suc_ref, sattn_ref, sconv_ref, sxat_ref,
                     nb=nb, nt=nt, final=final)


def _chunk_spec(rows, cols):
    assert rows % CAST_STEPS == 0
    return pl.BlockSpec((None, rows // CAST_STEPS, cols),
                        lambda i, l: (l[0], jnp.minimum(i, CAST_STEPS - 1), 0))


_WEIGHT_SHAPES = ((D_MODEL, IN_W), (ATT_W, D_MODEL), (CONV_W, D_MODEL), (X_W, D_MODEL), (D_MODEL, D_MODEL))


def _layer(layer, x, xs, g, win, wpa, wpb, wpx, wout, cw, cb, fg, bias, bd, mkb, mvx,
           sbias, sbd, ck, cv, cc, cmk, cmv, *, nt, final):
    nb, seq, _ = x.shape
    rows = xs.shape[0]
    nbs = rows // nt
    tile = PROMPT_TILE
    assert seq % tile == 0 and tile % WINDOW == 0
    tps = seq // tile
    prompt_steps = nb * tps
    kern = functools.partial(_layer_kernel, tile=tile, tiles_per_seq=tps, prompt_steps=prompt_steps,
                             nb=nbs, nt=nt, final=final)

    def prompt_step(i):
        return jnp.clip(i - CAST_STEPS, 0, prompt_steps - 1)

    def seq_of(i):
        return prompt_step(i) // tps

    def tile_of(i):
        return prompt_step(i) % tps

    grid_spec = pltpu.PrefetchScalarGridSpec(
        num_scalar_prefetch=1,
        grid=(CAST_STEPS + prompt_steps + 1,),
        in_specs=[
            pl.BlockSpec((1, tile, D_MODEL), lambda i, l: (seq_of(i), tile_of(i), 0)),
            _layer_spec((1, D_MODEL)),
            *[_chunk_spec(r, c) for r, c in _WEIGHT_SHAPES],
            _layer_spec((CONV_K, CONV_W)),
            _layer_spec((1, CONV_W)),
            _const_spec((1, D_MODEL)),
            _layer_spec((3, N_KV_HEADS, GROUP * CHUNK, KEY_SLOTS)),
            _const_spec((GROUP * CHUNK, GQ_W)),
            pl.BlockSpec((None, 1, N_MEM, X_W), lambda i, l: (l[0], seq_of(i), 0, 0)),
            pl.BlockSpec((None, 1, X_HEADS, N_MEM, 2 * X_HEAD_DIM), lambda i, l: (l[0], seq_of(i), 0, 0, 0)),
            _const_spec((rows, D_MODEL)),
            _layer_spec((N_KV_HEADS, GROUP * nt, KEY_SLOTS)),
            _const_spec((GROUP * nt, GQ_W)),
            _layer_spec((nbs, WINDOW, KV_W)),
            _layer_spec((nbs, WINDOW, KV_W)),
            _layer_spec((nbs, CONV_K - 1, CONV_W)),
            _layer_spec((nbs, N_MEM * X_HEADS, X_HEAD_DIM)),
            _layer_spec((nbs, N_MEM * X_HEADS, X_HEAD_DIM)),
        ],
        out_specs=[
            pl.BlockSpec((1, tile, D_MODEL), lambda i, l: (seq_of(i), tile_of(i), 0)),
            pl.BlockSpec((1, WINDOW, KV_W), lambda i, l: (seq_of(i), 0, 0)),
            pl.BlockSpec((1, WINDOW, KV_W), lambda i, l: (seq_of(i), 0, 0)),
            pl.BlockSpec((1, CONV_K - 1, CONV_W), lambda i, l: (seq_of(i), 0, 0)),
            pl.BlockSpec((rows, D_MODEL), lambda i, l: (0, 0)),
            pl.BlockSpec((rows, KV_W), lambda i, l: (0, 0)),
            pl.BlockSpec((rows, KV_W), lambda i, l: (0, 0)),
            pl.BlockSpec((nbs, CONV_K - 1, CONV_W), lambda i, l: (0, 0, 0)),
        ],
        scratch_shapes=[
            *[pltpu.VMEM(shape, BF16) for shape in _WEIGHT_SHAPES],
            pltpu.VMEM((N_KV_HEADS, WINDOW + tile, GQ_W), BF16),
            pltpu.VMEM((N_KV_HEADS, WINDOW + tile, 2 * LANES), BF16),
            pltpu.VMEM((CONV_PAD + tile, CONV_W), F32),
            pltpu.VMEM((tile, ATT_W), BF16),
            pltpu.VMEM((tile, ATT_W), F32),
            pltpu.VMEM((nbs, CONV_PAD + nt, CONV_W), F32),
            pltpu.VMEM((rows, ATT_W), F32),
            pltpu.VMEM((rows, CONV_W), F32),
            pltpu.VMEM((rows, X_W), F32),
        ],
    )
    return pl.pallas_call(
        kern,
        grid_spec=grid_spec,
        out_shape=[
            jax.ShapeDtypeStruct((nb, seq, D_MODEL), F32),
            jax.ShapeDtypeStruct((nb, WINDOW, KV_W), F32),
            jax.ShapeDtypeStruct((nb, WINDOW, KV_W), F32),
            jax.ShapeDtypeStruct((nb, CONV_K - 1, CONV_W), F32),
            jax.ShapeDtypeStruct((rows, D_MODEL), F32),
            jax.ShapeDtypeStruct((rows, KV_W), F32),
            jax.ShapeDtypeStruct((rows, KV_W), F32),
            jax.ShapeDtypeStruct((nbs, CONV_K - 1, CONV_W), F32),
        ],
        compiler_params=pltpu.CompilerParams(
            dimension_semantics=("arbitrary",), vmem_limit_bytes=VMEM_LIMIT),
        name="layer",
    )(layer, x, g, win, wpa, wpb, wpx, wout, cw, cb, fg, bias, bd, mkb, mvx,
      xs, sbias, sbd, ck, cv, cc, cmk, cmv)


def _alibi_slopes():
    return np.power(2.0, -8.0 * np.arange(1, N_HEADS + 1, dtype=np.float32) / N_HEADS).astype(np.float32)


def _alibi_bias(nq, nk, masked):
    dist = np.abs(np.arange(nq)[:, None] + WINDOW - np.arange(nk)[None, :]).astype(np.float32)
    slope = _alibi_slopes().reshape(N_KV_HEADS, GROUP)
    bias = -(slope[:, :, None, None] * dist[None, None])
    bias = np.where(np.arange(nk)[None, None, None, :] < masked, -np.inf, bias)
    return bias.reshape(N_KV_HEADS, GROUP * nq, nk).astype(np.float32)


def _bias_with_sink(alibi, attn_sink, nq):
    nk = alibi.shape[-1]
    lead = alibi.shape[:-3]
    sink = jnp.repeat(attn_sink.reshape(DEPTH, N_KV_HEADS, GROUP), nq, axis=2)[..., None]
    sink = jnp.broadcast_to(sink.reshape((DEPTH,) + (1,) * len(lead) + sink.shape[1:]),
                            (DEPTH,) + alibi.shape[:-1] + (1,))
    real = jnp.broadcast_to(jnp.asarray(alibi), (DEPTH,) + alibi.shape)
    unused = jnp.full((DEPTH,) + alibi.shape[:-1] + (KEY_SLOTS - nk - 1,), -jnp.inf, F32)
    return jnp.concatenate([real, sink, unused], axis=-1)


def _block_diag_mask(nq):
    rows = np.arange(GROUP * nq)[:, None] // nq
    cols = np.arange(GQ_W)[None, :] // HEAD_DIM
    return (rows == cols).astype(np.float32)


def kernel(x_prompt, x_sample, cache_attn_k, cache_attn_v, cache_conv, cache_mem_k, cache_mem_v,
           mem_prompt, norm_g, w_in, attn_sink, w_pa, conv_w, conv_b, w_pb, mem_norm_g,
           w_mk, w_mv, w_px, w_out, final_g):
    bp, seq, _ = x_prompt.shape
    bs, nt, _ = x_sample.shape
    win_len = cache_attn_k.shape[2]
    assert win_len == WINDOW and nt <= CHUNK and seq >= WINDOW

    bias_p = _bias_with_sink(np.stack([_alibi_bias(CHUNK, BAND, m) for m in (WINDOW, CHUNK, 0)]), attn_sink, CHUNK)
    bias_s = _bias_with_sink(_alibi_bias(nt, WINDOW + nt, 0), attn_sink, nt)
    bd_p = jnp.asarray(_block_diag_mask(CHUNK), BF16)
    bd_s = jnp.asarray(_block_diag_mask(nt), BF16)
    fg = final_g.reshape(1, D_MODEL)

    mk, mv, mkb, mvx = _memory_kv(mem_prompt, mem_norm_g, w_mk, w_mv)

    xp = x_prompt
    xs = x_sample.reshape(bs * nt, D_MODEL)
    ck = cache_attn_k.reshape(DEPTH, bs, WINDOW, KV_W)
    cv = cache_attn_v.reshape(DEPTH, bs, WINDOW, KV_W)
    g = norm_g.reshape(DEPTH, 1, D_MODEL)
    cb = conv_b.reshape(DEPTH, 1, CONV_W)
    cmk = cache_mem_k.reshape(DEPTH, bs, N_MEM * X_HEADS, X_HEAD_DIM)
    cmv = cache_mem_v.reshape(DEPTH, bs, N_MEM * X_HEADS, X_HEAD_DIM)
    pk, pv, pc, sk, sv, sc = [], [], [], [], [], []
    for l in range(DEPTH):
        final = l == DEPTH - 1
        layer = jnp.full((1,), l, jnp.int32)
        xp, kl, vl, ul, xs, ko, vo, uo = _layer(
            layer, xp, xs, g, w_in, w_pa, w_pb, w_px, w_out, conv_w, cb, fg, bias_p, bd_p, mkb, mvx,
            bias_s, bd_s, ck, cv, cache_conv, cmk, cmv, nt=nt, final=final)
        pk.append(kl)
        pv.append(vl)
        pc.append(ul)
        sk.append(ko)
        sv.append(vo)
        sc.append(uo)

    kv_p = (DEPTH, bp, WINDOW, N_KV_HEADS, HEAD_DIM)
    kv_s = (DEPTH, bs, nt, N_KV_HEADS, HEAD_DIM)
    mem_shape = (DEPTH, bp, N_MEM, X_HEADS, X_HEAD_DIM)
    return (xp, xs.reshape(bs, nt, D_MODEL),
            jnp.stack(pk).reshape(kv_p), jnp.stack(pv).reshape(kv_p), jnp.stack(pc),
            mk.reshape(mem_shape), mv.reshape(mem_shape),
            jnp.stack(sk).reshape(kv_s), jnp.stack(sv).reshape(kv_s), jnp.stack(sc))
```

```python
import functools

import numpy as np
import jax
import jax.numpy as jnp
from jax import lax
from jax.experimental import pallas as pl
from jax.experimental.pallas import tpu as pltpu

D_MODEL = 1024
DEPTH = 4
CHUNK = 64
WINDOW = 128
N_HEADS = 8
N_KV_HEADS = 2
GROUP = N_HEADS // N_KV_HEADS
HEAD_DIM = 64
ATT_W = N_HEADS * HEAD_DIM
KV_W = N_KV_HEADS * HEAD_DIM
CONV_W = 512
CONV_K = 3
X_HEADS = 4
X_HEAD_DIM = 128
X_W = X_HEADS * X_HEAD_DIM
N_MEM = 256
EPS = 1e-6

OFF_Q = 0
OFF_K = OFF_Q + ATT_W
OFF_V = OFF_K + KV_W
OFF_GA = OFF_V + KV_W
OFF_BB = OFF_GA + ATT_W
OFF_CC = OFF_BB + CONV_W
OFF_HH = OFF_CC + CONV_W
OFF_GB = OFF_HH + CONV_W
OFF_XQ = OFF_GB + CONV_W
OFF_GX = OFF_XQ + X_W
OFF_MG = OFF_GX + X_W
IN_W = OFF_MG + 3 * D_MODEL

LANES = 128
GQ_W = GROUP * HEAD_DIM
BAND = WINDOW + CHUNK
KEY_SLOTS = 256
CONV_PAD = 8
PROMPT_TILE = 512
Q_ROWS = 128
CAST_STEPS = 16
VMEM_LIMIT = 60 * 1024 * 1024

F32 = jnp.float32
BF16 = jnp.bfloat16
_NT = (((1,), (1,)), ((), ()))


def _rms_scale(x):
    return x * lax.rsqrt(jnp.mean(x * x, axis=-1, keepdims=True) + EPS)


def _silu(x):
    return x * jax.nn.sigmoid(x)


def _dot(a, b):
    return jnp.dot(a, b, preferred_element_type=F32)


def _dup_heads(a):
    lane = lax.broadcasted_iota(jnp.int32, a.shape, 1)
    sw = pltpu.roll(a, HEAD_DIM, 1)
    lo = lane < HEAD_DIM
    return jnp.where(lo, a, sw), jnp.where(lo, sw, a)


def _key_tails(nk):
    shape = (KEY_SLOTS - nk, 2 * LANES)
    row = lax.broadcasted_iota(jnp.int32, shape, 0)
    col = lax.broadcasted_iota(jnp.int32, shape, 1)
    vtail = jnp.where((row == 0) & (col >= LANES), 1.0, 0.0).astype(BF16)
    return jnp.zeros(shape, BF16), vtail


def _band_attention(qbd, kb, vb, bias):
    s = lax.dot_general(qbd, kb, _NT, preferred_element_type=F32) + bias
    p = jnp.exp(s - jnp.max(s, axis=-1, keepdims=True)).astype(BF16)
    oe = _dot(p, vb)
    return oe[:, 0:LANES] / oe[:, LANES:2 * LANES]


def _ungroup(on, nq):
    lo = lax.broadcasted_iota(jnp.int32, (nq, LANES), 1) < HEAD_DIM
    return (jnp.where(lo, on[0:nq], on[nq:2 * nq]),
            jnp.where(lo, on[2 * nq:3 * nq], on[3 * nq:4 * nq]))


def _cross_head(xq_h, mk_h, mvx_h):
    s = lax.dot_general(xq_h, mk_h, _NT, preferred_element_type=F32)
    p = jnp.exp(s - jnp.max(s, axis=-1, keepdims=True)).astype(BF16)
    oe = _dot(p, mvx_h)
    return oe[:, 0:X_HEAD_DIM] / oe[:, X_HEAD_DIM:2 * X_HEAD_DIM]


def _memkv_kernel(mem_ref, g_ref, wk_ref, wv_ref, mk_ref, mv_ref, mkb_ref, mvx_ref):
    mn = (_rms_scale(mem_ref[0]) * g_ref[0]).astype(BF16)
    mk = _dot(mn, wk_ref[0].astype(BF16))
    mv = _dot(mn, wv_ref[0].astype(BF16))
    mkb_ref[0, 0] = mk.astype(BF16)
    ones = jnp.ones((N_MEM, X_HEAD_DIM), BF16)
    for h in range(X_HEADS):
        cols = slice(h * X_HEAD_DIM, (h + 1) * X_HEAD_DIM)
        mk_ref[0, 0, pl.ds(h, N_MEM, stride=X_HEADS), :] = mk[:, cols]
        mv_ref[0, 0, pl.ds(h, N_MEM, stride=X_HEADS), :] = mv[:, cols]
        mvx_ref[0, 0, h, :, 0:X_HEAD_DIM] = mv[:, cols].astype(BF16)
        mvx_ref[0, 0, h, :, X_HEAD_DIM:2 * X_HEAD_DIM] = ones


def _memory_kv(mem, g, wk, wv):
    nb = mem.shape[0]
    return pl.pallas_call(
        _memkv_kernel,
        grid=(DEPTH, nb),
        in_specs=[
            pl.BlockSpec((1, N_MEM, D_MODEL), lambda l, b: (b, 0, 0)),
            pl.BlockSpec((1, 1, D_MODEL), lambda l, b: (l, 0, 0)),
            pl.BlockSpec((1, D_MODEL, X_W), lambda l, b: (l, 0, 0)),
            pl.BlockSpec((1, D_MODEL, X_W), lambda l, b: (l, 0, 0)),
        ],
        out_specs=[
            pl.BlockSpec((1, 1, N_MEM * X_HEADS, X_HEAD_DIM), lambda l, b: (l, b, 0, 0)),
            pl.BlockSpec((1, 1, N_MEM * X_HEADS, X_HEAD_DIM), lambda l, b: (l, b, 0, 0)),
            pl.BlockSpec((1, 1, N_MEM, X_W), lambda l, b: (l, b, 0, 0)),
            pl.BlockSpec((1, 1, X_HEADS, N_MEM, 2 * X_HEAD_DIM), lambda l, b: (l, b, 0, 0, 0)),
        ],
        out_shape=[
            jax.ShapeDtypeStruct((DEPTH, nb, N_MEM * X_HEADS, X_HEAD_DIM), F32),
            jax.ShapeDtypeStruct((DEPTH, nb, N_MEM * X_HEADS, X_HEAD_DIM), F32),
            jax.ShapeDtypeStruct((DEPTH, nb, N_MEM, X_W), BF16),
            jax.ShapeDtypeStruct((DEPTH, nb, X_HEADS, N_MEM, 2 * X_HEAD_DIM), BF16),
        ],
        name="memory_kv",
    )(mem, g.reshape(DEPTH, 1, D_MODEL), wk, wv)


def _prompt_tile(t, x_ref, g_ref, win_ref, wpa_ref, wpb_ref, wpx_ref, wout_ref, cw_ref, cb_ref,
                 bias_ref, bd_ref, mk_ref, mvx_ref, fg_ref,
                 xo_ref, kl_ref, vl_ref, ul_ref,
                 kr_ref, vv_ref, uc_ref, q_ref, attn_ref, *, tile, final):
    @pl.when(t == 0)
    def _():
        kr_ref[:, 0:WINDOW, :] = jnp.zeros((N_KV_HEADS, WINDOW, GQ_W), BF16)
        vv_ref[:, 0:WINDOW, 0:LANES] = jnp.zeros((N_KV_HEADS, WINDOW, LANES), BF16)
        vv_ref[:, :, LANES:2 * LANES] = jnp.ones((N_KV_HEADS, WINDOW + tile, LANES), BF16)
        uc_ref[0:CONV_PAD, :] = jnp.zeros((CONV_PAD, CONV_W), F32)

    x = x_ref[0]
    xn = (_rms_scale(x) * g_ref[...]).astype(BF16)

    def proj(off, width):
        return _dot(xn, win_ref[:, off:off + width])

    w_q = win_ref[:, OFF_Q:OFF_K]
    for r in range(0, tile, Q_ROWS):
        q_ref[r:r + Q_ROWS, :] = (_dot(xn[r:r + Q_ROWS], w_q) * (HEAD_DIM ** -0.5)).astype(BF16)
    half = tile // 2
    w_kv = win_ref[:, OFF_K:OFF_GA]
    kv = jnp.concatenate([_dot(xn[0:half], w_kv), _dot(xn[half:tile], w_kv)], axis=0)
    k = kv[:, 0:KV_W]
    v = kv[:, KV_W:2 * KV_W]
    ga = proj(OFF_GA, ATT_W)
    kl_ref[0] = k[tile - WINDOW:tile]
    vl_ref[0] = v[tile - WINDOW:tile]
    for j, (kj, vj) in enumerate(zip(_dup_heads(k), _dup_heads(v))):
        kj = kj.astype(BF16)
        kr_ref[j, WINDOW:WINDOW + tile, 0:LANES] = kj
        kr_ref[j, WINDOW:WINDOW + tile, LANES:2 * LANES] = kj
        vv_ref[j, WINDOW:WINDOW + tile, 0:LANES] = vj.astype(BF16)

    fillers = [functools.partial(proj, off, CONV_W) for off in (OFF_BB, OFF_CC, OFF_HH, OFF_GB)]
    fillers += [functools.partial(proj, off, X_W) for off in (OFF_XQ, OFF_GX)]
    fillers += [lambda i=i: jax.nn.sigmoid(proj(OFF_MG + i * D_MODEL, D_MODEL)) for i in (1, 2)]
    filled = []
    n_chunks = tile // CHUNK
    ktail, vtail = _key_tails(BAND)

    for c in range(n_chunks):
        r0 = c * CHUNK
        qc = q_ref[r0:r0 + CHUNK, :]
        variant = jnp.where(t == 0, c, 2) if c < 2 else 2
        for j in range(N_KV_HEADS):
            qg = qc[:, j * GQ_W:(j + 1) * GQ_W]
            qbd = jnp.concatenate([qg] * GROUP, axis=0) * bd_ref[...]
            on = _band_attention(qbd, jnp.concatenate([kr_ref[j, r0:r0 + BAND, :], ktail], axis=0),
                                 jnp.concatenate([vv_ref[j, r0:r0 + BAND, :], vtail], axis=0),
                                 bias_ref[variant, j])
            lo, hi = _ungroup(on, CHUNK)
            attn_ref[r0:r0 + CHUNK, j * GQ_W:j * GQ_W + LANES] = lo
            attn_ref[r0:r0 + CHUNK, j * GQ_W + LANES:(j + 1) * GQ_W] = hi
        while len(filled) * n_chunks < (c + 1) * len(fillers):
            filled.append(fillers[len(filled)]())
    bb, cc, hh, gb, xq, gx, gate_b, gate_x = filled
    kr_ref[:, 0:WINDOW, :] = kr_ref[:, tile:tile + WINDOW, :]
    vv_ref[:, 0:WINDOW, 0:LANES] = vv_ref[:, tile:tile + WINDOW, 0:LANES]

    u = cc * hh
    uc_ref[CONV_PAD:CONV_PAD + tile, :] = u
    ul_ref[0] = u[tile - (CONV_K - 1):tile]
    conv = (uc_ref[CONV_PAD - 2:CONV_PAD - 2 + tile, :] * cw_ref[0:1, :]
            + uc_ref[CONV_PAD - 1:CONV_PAD - 1 + tile, :] * cw_ref[1:2, :]
            + u * cw_ref[2:3, :]) + cb_ref[...]
    uc_ref[0:CONV_PAD, :] = uc_ref[tile:tile + CONV_PAD, :]
    zb = (bb * conv * _silu(gb)).astype(BF16)
    za = (attn_ref[...] * _silu(ga)).astype(BF16)

    xq = (xq * (X_HEAD_DIM ** -0.5)).astype(BF16)
    merge_work = [lambda: _dot(za, wpa_ref[...]),
                  lambda: jax.nn.sigmoid(proj(OFF_MG, D_MODEL)),
                  lambda: _dot(zb, wpb_ref[...])]
    merged = []
    heads = []
    for h in range(X_HEADS):
        heads.append(_cross_head(xq[:, h * X_HEAD_DIM:(h + 1) * X_HEAD_DIM],
                                 mk_ref[0, :, h * X_HEAD_DIM:(h + 1) * X_HEAD_DIM], mvx_ref[0, h]))
        if h < len(merge_work):
            merged.append(merge_work[h]())
    ya, gate_a, yb = merged
    zx = (jnp.concatenate(heads, axis=1) * _silu(gx)).astype(BF16)
    acc = gate_a * ya + gate_b * yb + gate_x * _dot(zx, wpx_ref[...])
    out = x + _dot(acc.astype(BF16), wout_ref[...])
    if final:
        out = _rms_scale(out) * fg_ref[...]
    xo_ref[0] = out


def _const_spec(shape):
    nd = len(shape)
    return pl.BlockSpec(shape, lambda *_: (0,) * nd, pipeline_mode=pl.Buffered(1))


def _layer_spec(shape):
    nd = len(shape)
    return pl.BlockSpec((None,) + tuple(shape), lambda *a: (a[-1][0],) + (0,) * nd, pipeline_mode=pl.Buffered(1))


def _sample_step(x_ref, g_ref, win_ref, wpa_ref, wpb_ref, wpx_ref, wout_ref, cw_ref, cb_ref,
                 bias_ref, bd_ref, ck_ref, cv_ref, cc_ref, cmk_ref, cmv_ref, fg_ref,
                 xo_ref, ko_ref, vo_ref, uo_ref,
                 uc_ref, attn_ref, conv_ref, xat_ref, *, nb, nt, final):
    x = x_ref[...]
    xn = (_rms_scale(x) * g_ref[...]).astype(BF16)

    def proj(off, width):
        return _dot(xn, win_ref[:, off:off + width])

    p1 = proj(OFF_Q, OFF_BB - OFF_Q)
    q = (p1[:, OFF_Q:OFF_K] * (HEAD_DIM ** -0.5)).astype(BF16)
    k = p1[:, OFF_K:OFF_V]
    v = p1[:, OFF_V:OFF_GA]
    ga = p1[:, OFF_GA:OFF_BB]
    ko_ref[...] = k
    vo_ref[...] = v
    ones = jnp.ones((WINDOW + nt, LANES), BF16)
    ktail, vtail = _key_tails(WINDOW + nt)
    fillers = [functools.partial(proj, off, CONV_W) for off in (OFF_BB, OFF_CC, OFF_HH, OFF_GB)]
    fillers += [functools.partial(proj, off, X_W) for off in (OFF_XQ, OFF_GX)]
    filled = []
    for b in range(nb):
        rows = slice(b * nt, (b + 1) * nt)
        kall = _dup_heads(jnp.concatenate([ck_ref[b], k[rows]], axis=0))
        vall = _dup_heads(jnp.concatenate([cv_ref[b], v[rows]], axis=0))
        for j in range(N_KV_HEADS):
            kj = kall[j].astype(BF16)
            vj = vall[j].astype(BF16)
            qg = q[rows, j * GQ_W:(j + 1) * GQ_W]
            qbd = jnp.concatenate([qg] * GROUP, axis=0) * bd_ref[...]
            on = _band_attention(qbd, jnp.concatenate([jnp.concatenate([kj, kj], axis=1), ktail], axis=0),
                                 jnp.concatenate([jnp.concatenate([vj, ones], axis=1), vtail], axis=0),
                                 bias_ref[j])
            lo, hi = _ungroup(on, nt)
            attn_ref[rows, j * GQ_W:j * GQ_W + LANES] = lo
            attn_ref[rows, j * GQ_W + LANES:(j + 1) * GQ_W] = hi
        while len(filled) * nb < (b + 1) * len(fillers):
            filled.append(fillers[len(filled)]())
    bb, cc, hh, gb, p3q, gx = filled
    za = attn_ref[...] * _silu(ga)

    u = cc * hh
    for b in range(nb):
        rows = slice(b * nt, (b + 1) * nt)
        ub = u[rows]
        uc_ref[b, CONV_PAD - (CONV_K - 1):CONV_PAD, :] = cc_ref[b]
        uc_ref[b, CONV_PAD:CONV_PAD + nt, :] = ub
        uo_ref[b] = ub[nt - (CONV_K - 1):nt]
        conv_ref[rows, :] = (uc_ref[b, CONV_PAD - 2:CONV_PAD - 2 + nt, :] * cw_ref[0:1, :]
                             + uc_ref[b, CONV_PAD - 1:CONV_PAD - 1 + nt, :] * cw_ref[1:2, :]
                             + ub * cw_ref[2:3, :]) + cb_ref[...]
    zb = bb * conv_ref[...] * _silu(gb)

    xq = (p3q * (X_HEAD_DIM ** -0.5)).astype(BF16)
    mones = jnp.ones((N_MEM, X_HEAD_DIM), BF16)
    merge_work = [lambda: _dot(za.astype(BF16), wpa_ref[...]),
                  lambda: jax.nn.sigmoid(proj(OFF_MG, D_MODEL)),
                  lambda: _dot(zb.astype(BF16), wpb_ref[...]),
                  lambda: jax.nn.sigmoid(proj(OFF_MG + D_MODEL, D_MODEL)),
                  lambda: jax.nn.sigmoid(proj(OFF_MG + 2 * D_MODEL, D_MODEL))]
    merged = []
    for b in range(nb):
        rows = slice(b * nt, (b + 1) * nt)
        for h in range(X_HEADS):
            cols = slice(h * X_HEAD_DIM, (h + 1) * X_HEAD_DIM)
            head_rows = pl.ds(h, N_MEM, stride=X_HEADS)
            mvx = jnp.concatenate([cmv_ref[b, head_rows, :].astype(BF16), mones], axis=1)
            xat_ref[rows, cols] = _cross_head(xq[rows, cols], cmk_ref[b, head_rows, :].astype(BF16), mvx)
        if b < len(merge_work):
            merged.append(merge_work[b]())
    ya, gate_a, yb, gate_b, gate_x = merged
    zx = xat_ref[...] * _silu(gx)

    acc = gate_a * ya + gate_b * yb + gate_x * _dot(zx.astype(BF16), wpx_ref[...])
    out = x + _dot(acc.astype(BF16), wout_ref[...])
    if final:
        out = _rms_scale(out) * fg_ref[...]
    xo_ref[...] = out


def _layer_kernel(layer_ref, x_ref, g_ref, win32_ref, wpa32_ref, wpb32_ref, wpx32_ref, wout32_ref,
                  cw_ref, cb_ref, fg_ref, bias_ref, bd_ref, mk_ref, mvx_ref,
                  xs_ref, sbias_ref, sbd_ref, ck_ref, cv_ref, cc_ref, cmk_ref, cmv_ref,
                  xo_ref, kl_ref, vl_ref, ul_ref, xso_ref, ko_ref, vo_ref, uo_ref,
                  win_ref, wpa_ref, wpb_ref, wpx_ref, wout_ref,
                  kr_ref, vv_ref, uc_ref, q_ref, attn_ref, suc_ref, sattn_ref, sconv_ref, sxat_ref,
                  *, tile, tiles_per_seq, nb, nt, final):
    i = pl.program_id(0)
    weights = (win_ref, wpa_ref, wpb_ref, wpx_ref, wout_ref)

    @pl.when(i < CAST_STEPS)
    def _():
        for src, dst in zip((win32_ref, wpa32_ref, wpb32_ref, wpx32_ref, wout32_ref), weights):
            rows = src.shape[0]
            dst[pl.ds(pl.multiple_of(i * rows, rows), rows), :] = src[...].astype(BF16)

    @pl.when(i == CAST_STEPS)
    def _():
        _sample_step(xs_ref, g_ref, *weights, cw_ref, cb_ref,
                     sbias_ref, sbd_ref, ck_ref, cv_ref, cc_ref, cmk_ref, cmv_ref, fg_ref,
                     xso_ref, ko_ref, vo_ref, uo_ref, suc_ref, sattn_ref, sconv_ref, sxat_ref,
                     nb=nb, nt=nt, final=final)

    @pl.when(i > CAST_STEPS)
    def _():
        _prompt_tile(lax.rem(i - (CAST_STEPS + 1), tiles_per_seq), x_ref, g_ref, *weights, cw_ref, cb_ref,
                     bias_ref, bd_ref, mk_ref, mvx_ref, fg_ref,
                     xo_ref, kl_ref, vl_ref, ul_ref, kr_ref, vv_ref, uc_ref, q_ref, attn_ref,
                     tile=tile, final=final)


def _chunk_spec(rows, cols):
    assert rows % CAST_STEPS == 0
    return pl.BlockSpec((None, rows // CAST_STEPS, cols),
                        lambda i, l: (l[0], jnp.minimum(i, CAST_STEPS - 1), 0))


_WEIGHT_SHAPES = ((D_MODEL, IN_W), (ATT_W, D_MODEL), (CONV_W, D_MODEL), (X_W, D_MODEL), (D_MODEL, D_MODEL))


def _layer(layer, x, xs, g, win, wpa, wpb, wpx, wout, cw, cb, fg, bias, bd, mkb, mvx,
           sbias, sbd, ck, cv, cc, cmk, cmv, *, nt, final):
    nb, seq, _ = x.shape
    rows = xs.shape[0]
    nbs = rows // nt
    tile = PROMPT_TILE
    assert seq % tile == 0 and tile % WINDOW == 0
    tps = seq // tile
    kern = functools.partial(_layer_kernel, tile=tile, tiles_per_seq=tps, nb=nbs, nt=nt, final=final)

    def prompt_step(i):
        return jnp.maximum(i - (CAST_STEPS + 1), 0)

    def seq_of(i):
        return prompt_step(i) // tps

    def tile_of(i):
        return prompt_step(i) % tps

    grid_spec = pltpu.PrefetchScalarGridSpec(
        num_scalar_prefetch=1,
        grid=(CAST_STEPS + 1 + nb * tps,),
        in_specs=[
            pl.BlockSpec((1, tile, D_MODEL), lambda i, l: (seq_of(i), tile_of(i), 0)),
            _layer_spec((1, D_MODEL)),
            *[_chunk_spec(r, c) for r, c in _WEIGHT_SHAPES],
            _layer_spec((CONV_K, CONV_W)),
            _layer_spec((1, CONV_W)),
            _const_spec((1, D_MODEL)),
            _layer_spec((3, N_KV_HEADS, GROUP * CHUNK, KEY_SLOTS)),
            _const_spec((GROUP * CHUNK, GQ_W)),
            pl.BlockSpec((None, 1, N_MEM, X_W), lambda i, l: (l[0], seq_of(i), 0, 0)),
            pl.BlockSpec((None, 1, X_HEADS, N_MEM, 2 * X_HEAD_DIM), lambda i, l: (l[0], seq_of(i), 0, 0, 0)),
            _const_spec((rows, D_MODEL)),
            _layer_spec((N_KV_HEADS, GROUP * nt, KEY_SLOTS)),
            _const_spec((GROUP * nt, GQ_W)),
            _layer_spec((nbs, WINDOW, KV_W)),
            _layer_spec((nbs, WINDOW, KV_W)),
            _layer_spec((nbs, CONV_K - 1, CONV_W)),
            _layer_spec((nbs, N_MEM * X_HEADS, X_HEAD_DIM)),
            _layer_spec((nbs, N_MEM * X_HEADS, X_HEAD_DIM)),
        ],
        out_specs=[
            pl.BlockSpec((1, tile, D_MODEL), lambda i, l: (seq_of(i), tile_of(i), 0)),
            pl.BlockSpec((1, WINDOW, KV_W), lambda i, l: (seq_of(i), 0, 0)),
            pl.BlockSpec((1, WINDOW, KV_W), lambda i, l: (seq_of(i), 0, 0)),
            pl.BlockSpec((1, CONV_K - 1, CONV_W), lambda i, l: (seq_of(i), 0, 0)),
            pl.BlockSpec((rows, D_MODEL), lambda i, l: (0, 0)),
            pl.BlockSpec((rows, KV_W), lambda i, l: (0, 0)),
            pl.BlockSpec((rows, KV_W), lambda i, l: (0, 0)),
            pl.BlockSpec((nbs, CONV_K - 1, CONV_W), lambda i, l: (0, 0, 0)),
        ],
        scratch_shapes=[
            *[pltpu.VMEM(shape, BF16) for shape in _WEIGHT_SHAPES],
            pltpu.VMEM((N_KV_HEADS, WINDOW + tile, GQ_W), BF16),
            pltpu.VMEM((N_KV_HEADS, WINDOW + tile, 2 * LANES), BF16),
            pltpu.VMEM((CONV_PAD + tile, CONV_W), F32),
            pltpu.VMEM((tile, ATT_W), BF16),
            pltpu.VMEM((tile, ATT_W), F32),
            pltpu.VMEM((nbs, CONV_PAD + nt, CONV_W), F32),
            pltpu.VMEM((rows, ATT_W), F32),
            pltpu.VMEM((rows, CONV_W), F32),
            pltpu.VMEM((rows, X_W), F32),
        ],
    )
    return pl.pallas_call(
        kern,
        grid_spec=grid_spec,
        out_shape=[
            jax.ShapeDtypeStruct((nb, seq, D_MODEL), F32),
            jax.ShapeDtypeStruct((nb, WINDOW, KV_W), F32),
            jax.ShapeDtypeStruct((nb, WINDOW, KV_W), F32),
            jax.ShapeDtypeStruct((nb, CONV_K - 1, CONV_W), F32),
            jax.ShapeDtypeStruct((rows, D_MODEL), F32),
            jax.ShapeDtypeStruct((rows, KV_W), F32),
            jax.ShapeDtypeStruct((rows, KV_W), F32),
            jax.ShapeDtypeStruct((nbs, CONV_K - 1, CONV_W), F32),
        ],
        compiler_params=pltpu.CompilerParams(
            dimension_semantics=("arbitrary",), vmem_limit_bytes=VMEM_LIMIT),
        name="layer",
    )(layer, x, g, win, wpa, wpb, wpx, wout, cw, cb, fg, bias, bd, mkb, mvx,
      xs, sbias, sbd, ck, cv, cc, cmk, cmv)


def _alibi_slopes():
    return np.power(2.0, -8.0 * np.arange(1, N_HEADS + 1, dtype=np.float32) / N_HEADS).astype(np.float32)


def _alibi_bias(nq, nk, masked):
    dist = np.abs(np.arange(nq)[:, None] + WINDOW - np.arange(nk)[None, :]).astype(np.float32)
    slope = _alibi_slopes().reshape(N_KV_HEADS, GROUP)
    bias = -(slope[:, :, None, None] * dist[None, None])
    bias = np.where(np.arange(nk)[None, None, None, :] < masked, -np.inf, bias)
    return bias.reshape(N_KV_HEADS, GROUP * nq, nk).astype(np.float32)


def _bias_with_sink(alibi, attn_sink, nq):
    nk = alibi.shape[-1]
    lead = alibi.shape[:-3]
    sink = jnp.repeat(attn_sink.reshape(DEPTH, N_KV_HEADS, GROUP), nq, axis=2)[..., None]
    sink = jnp.broadcast_to(sink.reshape((DEPTH,) + (1,) * len(lead) + sink.shape[1:]),
                            (DEPTH,) + alibi.shape[:-1] + (1,))
    real = jnp.broadcast_to(jnp.asarray(alibi), (DEPTH,) + alibi.shape)
    unused = jnp.full((DEPTH,) + alibi.shape[:-1] + (KEY_SLOTS - nk - 1,), -jnp.inf, F32)
    return jnp.concatenate([real, sink, unused], axis=-1)


def _block_diag_mask(nq):
    rows = np.arange(GROUP * nq)[:, None] // nq
    cols = np.arange(GQ_W)[None, :] // HEAD_DIM
    return (rows == cols).astype(np.float32)


def kernel(x_prompt, x_sample, cache_attn_k, cache_attn_v, cache_conv, cache_mem_k, cache_mem_v,
           mem_prompt, norm_g, w_in, attn_sink, w_pa, conv_w, conv_b, w_pb, mem_norm_g,
           w_mk, w_mv, w_px, w_out, final_g):
    bp, seq, _ = x_prompt.shape
    bs, nt, _ = x_sample.shape
    win_len = cache_attn_k.shape[2]
    assert win_len == WINDOW and nt <= CHUNK and seq >= WINDOW

    bias_p = _bias_with_sink(np.stack([_alibi_bias(CHUNK, BAND, m) for m in (WINDOW, CHUNK, 0)]), attn_sink, CHUNK)
    bias_s = _bias_with_sink(_alibi_bias(nt, WINDOW + nt, 0), attn_sink, nt)
    bd_p = jnp.asarray(_block_diag_mask(CHUNK), BF16)
    bd_s = jnp.asarray(_block_diag_mask(nt), BF16)
    fg = final_g.reshape(1, D_MODEL)

    mk, mv, mkb, mvx = _memory_kv(mem_prompt, mem_norm_g, w_mk, w_mv)

    xp = x_prompt
    xs = x_sample.reshape(bs * nt, D_MODEL)
    ck = cache_attn_k.reshape(DEPTH, bs, WINDOW, KV_W)
    cv = cache_attn_v.reshape(DEPTH, bs, WINDOW, KV_W)
    g = norm_g.reshape(DEPTH, 1, D_MODEL)
    cb = conv_b.reshape(DEPTH, 1, CONV_W)
    cmk = cache_mem_k.reshape(DEPTH, bs, N_MEM * X_HEADS, X_HEAD_DIM)
    cmv = cache_mem_v.reshape(DEPTH, bs, N_MEM * X_HEADS, X_HEAD_DIM)
    pk, pv, pc, sk, sv, sc = [], [], [], [], [], []
    for l in range(DEPTH):
        final = l == DEPTH - 1
        layer = jnp.full((1,), l, jnp.int32)
        xp, kl, vl, ul, xs, ko, vo, uo = _layer(
            layer, xp, xs, g, w_in, w_pa, w_pb, w_px, w_out, conv_w, cb, fg, bias_p, bd_p, mkb, mvx,
            bias_s, bd_s, ck, cv, cache_conv, cmk, cmv, nt=nt, final=final)
        pk.append(kl)
        pv.append(vl)
        pc.append(ul)
        sk.append(ko)
        sv.append(vo)
        sc.append(uo)

    kv_p = (DEPTH, bp, WINDOW, N_KV_HEADS, HEAD_DIM)
    kv_s = (DEPTH, bs, nt, N_KV_HEADS, HEAD_DIM)
    mem_shape = (DEPTH, bp, N_MEM, X_HEADS, X_HEAD_DIM)
    return (xp, xs.reshape(bs, nt, D_MODEL),
            jnp.stack(pk).reshape(kv_p), jnp.stack(pv).reshape(kv_p), jnp.stack(pc),
            mk.reshape(mem_shape), mv.reshape(mem_shape),
            jnp.stack(sk).reshape(kv_s), jnp.stack(sv).reshape(kv_s), jnp.stack(sc))
```

```python
import functools

import numpy as np
import jax
import jax.numpy as jnp
from jax import lax
from jax.experimental import pallas as pl
from jax.experimental.pallas import tpu as pltpu

D_MODEL = 1024
DEPTH = 4
CHUNK = 64
WINDOW = 128
N_HEADS = 8
N_KV_HEADS = 2
GROUP = N_HEADS // N_KV_HEADS
HEAD_DIM = 64
ATT_W = N_HEADS * HEAD_DIM
KV_W = N_KV_HEADS * HEAD_DIM
CONV_W = 512
CONV_K = 3
X_HEADS = 4
X_HEAD_DIM = 128
X_W = X_HEADS * X_HEAD_DIM
N_MEM = 256
EPS = 1e-6

OFF_Q = 0
OFF_K = OFF_Q + ATT_W
OFF_V = OFF_K + KV_W
OFF_GA = OFF_V + KV_W
OFF_BB = OFF_GA + ATT_W
OFF_CC = OFF_BB + CONV_W
OFF_HH = OFF_CC + CONV_W
OFF_GB = OFF_HH + CONV_W
OFF_XQ = OFF_GB + CONV_W
OFF_GX = OFF_XQ + X_W
OFF_MG = OFF_GX + X_W
IN_W = OFF_MG + 3 * D_MODEL
W256_OFFS = (OFF_K,)
W512_OFFS = (OFF_Q, OFF_GA, OFF_BB, OFF_CC, OFF_HH, OFF_GB, OFF_XQ, OFF_GX)
W1024_OFFS = (OFF_MG, OFF_MG + D_MODEL, OFF_MG + 2 * D_MODEL)
W_IN_GROUPS = ((256, W256_OFFS), (512, W512_OFFS), (1024, W1024_OFFS))

LANES = 128
GQ_W = GROUP * HEAD_DIM
BAND = WINDOW + CHUNK
KEY_SLOTS = 256
CONV_PAD = 8
PROMPT_TILE = 512
Q_ROWS = 128
CAST_STEPS = 16
VMEM_LIMIT = 60 * 1024 * 1024

F32 = jnp.float32
BF16 = jnp.bfloat16
_NT = (((1,), (1,)), ((), ()))


def _rms_scale(x):
    return x * lax.rsqrt(jnp.mean(x * x, axis=-1, keepdims=True) + EPS)


def _silu(x):
    return x * jax.nn.sigmoid(x)


def _dot(a, b):
    return jnp.dot(a, b, preferred_element_type=F32)


def _w_in(win_refs, off):
    for ref, (_, offs) in zip(win_refs, W_IN_GROUPS):
        if off in offs:
            return ref[offs.index(off)]
    raise ValueError(off)


def _dup_heads(a):
    lane = lax.broadcasted_iota(jnp.int32, a.shape, 1)
    sw = pltpu.roll(a, HEAD_DIM, 1)
    lo = lane < HEAD_DIM
    return jnp.where(lo, a, sw), jnp.where(lo, sw, a)


def _key_tails(nk):
    shape = (KEY_SLOTS - nk, 2 * LANES)
    row = lax.broadcasted_iota(jnp.int32, shape, 0)
    col = lax.broadcasted_iota(jnp.int32, shape, 1)
    vtail = jnp.where((row == 0) & (col >= LANES), 1.0, 0.0).astype(BF16)
    return jnp.zeros(shape, BF16), vtail


def _band_attention(qbd, kb, vb, bias):
    s = lax.dot_general(qbd, kb, _NT, preferred_element_type=F32) + bias
    p = jnp.exp(s - jnp.max(s, axis=-1, keepdims=True)).astype(BF16)
    oe = _dot(p, vb)
    return oe[:, 0:LANES] / oe[:, LANES:2 * LANES]


def _band_attention_t(kb, qbd, vt2, bias_t, sink_row):
    s = lax.dot_general(kb, qbd, _NT, preferred_element_type=F32) + bias_t
    m = jnp.maximum(jnp.max(s, axis=0, keepdims=True), sink_row)
    p = jnp.exp(s - m)
    den = jnp.sum(p, axis=0, keepdims=True) + jnp.exp(sink_row - m)
    return (_dot(vt2, p.astype(BF16)) / den).T


def _ungroup(on, nq):
    lo = lax.broadcasted_iota(jnp.int32, (nq, LANES), 1) < HEAD_DIM
    return (jnp.where(lo, on[0:nq], on[nq:2 * nq]),
            jnp.where(lo, on[2 * nq:3 * nq], on[3 * nq:4 * nq]))


def _cross_head(xq_h, mk_h, mvx_h):
    s = lax.dot_general(xq_h, mk_h, _NT, preferred_element_type=F32)
    p = jnp.exp(s - jnp.max(s, axis=-1, keepdims=True)).astype(BF16)
    oe = _dot(p, mvx_h)
    return oe[:, 0:X_HEAD_DIM] / oe[:, X_HEAD_DIM:2 * X_HEAD_DIM]


def _memkv_kernel(mem_ref, g_ref, wk_ref, wv_ref, mk_ref, mv_ref, mkb_ref, mvx_ref):
    nb = mem_ref.shape[0]
    mn = (_rms_scale(mem_ref[...].reshape(nb * N_MEM, D_MODEL)) * g_ref[0]).astype(BF16)
    mk = _dot(mn, wk_ref[0].astype(BF16))
    mv = _dot(mn, wv_ref[0].astype(BF16))
    ones = jnp.ones((N_MEM, X_HEAD_DIM), BF16)
    for b in range(nb):
        rows = slice(b * N_MEM, (b + 1) * N_MEM)
        mkb_ref[0, b] = mk[rows].astype(BF16)
        for h in range(X_HEADS):
            cols = slice(h * X_HEAD_DIM, (h + 1) * X_HEAD_DIM)
            mk_ref[0, b, pl.ds(h, N_MEM, stride=X_HEADS), :] = mk[rows, cols]
            mv_ref[0, b, pl.ds(h, N_MEM, stride=X_HEADS), :] = mv[rows, cols]
            mvx_ref[0, b, h, :, 0:X_HEAD_DIM] = mv[rows, cols].astype(BF16)
            mvx_ref[0, b, h, :, X_HEAD_DIM:2 * X_HEAD_DIM] = ones


def _memory_kv(mem, g, wk, wv):
    nb = mem.shape[0]
    return pl.pallas_call(
        _memkv_kernel,
        grid=(DEPTH,),
        in_specs=[
            pl.BlockSpec((nb, N_MEM, D_MODEL), lambda l: (0, 0, 0)),
            pl.BlockSpec((1, 1, D_MODEL), lambda l: (l, 0, 0)),
            pl.BlockSpec((1, D_MODEL, X_W), lambda l: (l, 0, 0)),
            pl.BlockSpec((1, D_MODEL, X_W), lambda l: (l, 0, 0)),
        ],
        out_specs=[
            pl.BlockSpec((1, nb, N_MEM * X_HEADS, X_HEAD_DIM), lambda l: (l, 0, 0, 0)),
            pl.BlockSpec((1, nb, N_MEM * X_HEADS, X_HEAD_DIM), lambda l: (l, 0, 0, 0)),
            pl.BlockSpec((1, nb, N_MEM, X_W), lambda l: (l, 0, 0, 0)),
            pl.BlockSpec((1, nb, X_HEADS, N_MEM, 2 * X_HEAD_DIM), lambda l: (l, 0, 0, 0, 0)),
        ],
        out_shape=[
            jax.ShapeDtypeStruct((DEPTH, nb, N_MEM * X_HEADS, X_HEAD_DIM), F32),
            jax.ShapeDtypeStruct((DEPTH, nb, N_MEM * X_HEADS, X_HEAD_DIM), F32),
            jax.ShapeDtypeStruct((DEPTH, nb, N_MEM, X_W), BF16),
            jax.ShapeDtypeStruct((DEPTH, nb, X_HEADS, N_MEM, 2 * X_HEAD_DIM), BF16),
        ],
        name="memory_kv",
    )(mem, g.reshape(DEPTH, 1, D_MODEL), wk, wv)


def _prompt_tile(t, x_ref, g_ref, win_refs, wpa_ref, wpb_ref, wpx_ref, wout_ref, cw_ref, cb_ref,
                 biast_ref, sink_ref, bd_ref, mk_ref, mvx_ref, fg_ref,
                 xo_ref, kl_ref, vl_ref, ul_ref,
                 kr_ref, vall_ref, uc_ref, q_ref, attn_ref, *, tile, final):
    @pl.when(t == 0)
    def _():
        kr_ref[:, 0:WINDOW, :] = jnp.zeros((N_KV_HEADS, WINDOW, GQ_W), BF16)
        vall_ref[0:WINDOW, :] = jnp.zeros((WINDOW, KV_W), F32)
        vall_ref[WINDOW + tile:WINDOW + tile + CHUNK, :] = jnp.zeros((CHUNK, KV_W), F32)
        uc_ref[0:CONV_PAD, :] = jnp.zeros((CONV_PAD, CONV_W), F32)

    x = x_ref[0]
    xn = (_rms_scale(x) * g_ref[...]).astype(BF16)

    def proj(off):
        return _dot(xn, _w_in(win_refs, off))

    w_q = _w_in(win_refs, OFF_Q)
    for r in range(0, tile, Q_ROWS):
        q_ref[r:r + Q_ROWS, :] = (_dot(xn[r:r + Q_ROWS], w_q) * (HEAD_DIM ** -0.5)).astype(BF16)
    half = tile // 2
    w_kv = _w_in(win_refs, OFF_K)
    kv = jnp.concatenate([_dot(xn[0:half], w_kv), _dot(xn[half:tile], w_kv)], axis=0)
    k = kv[:, 0:KV_W]
    v = kv[:, KV_W:2 * KV_W]
    ga = proj(OFF_GA)
    kl_ref[0] = k[tile - WINDOW:tile]
    vl_ref[0] = v[tile - WINDOW:tile]
    vall_ref[WINDOW:WINDOW + tile, :] = v
    for j, kj in enumerate(_dup_heads(k)):
        kj = kj.astype(BF16)
        kr_ref[j, WINDOW:WINDOW + tile, 0:LANES] = kj
        kr_ref[j, WINDOW:WINDOW + tile, LANES:2 * LANES] = kj

    fillers = [functools.partial(proj, off) for off in (OFF_BB, OFF_CC, OFF_HH, OFF_GB, OFF_XQ, OFF_GX)]
    fillers += [lambda i=i: jax.nn.sigmoid(proj(OFF_MG + i * D_MODEL)) for i in (1, 2)]
    filled = []
    n_chunks = tile // CHUNK

    for c in range(n_chunks):
        r0 = c * CHUNK
        qc = q_ref[r0:r0 + CHUNK, :]
        variant = jnp.where(t == 0, c, 2) if c < 2 else 2
        vt = vall_ref[r0:r0 + BAND + CHUNK, :].T[:, 0:BAND].astype(BF16)
        for j in range(N_KV_HEADS):
            qg = qc[:, j * GQ_W:(j + 1) * GQ_W]
            qbd = jnp.concatenate([qg] * GROUP, axis=0) * bd_ref[...]
            vt_j = vt[j * HEAD_DIM:(j + 1) * HEAD_DIM]
            on = _band_attention_t(kr_ref[j, r0:r0 + BAND, :], qbd, jnp.concatenate([vt_j, vt_j], axis=0),
                                   biast_ref[variant, j], sink_ref[j])
            lo, hi = _ungroup(on, CHUNK)
            attn_ref[r0:r0 + CHUNK, j * GQ_W:j * GQ_W + LANES] = lo
            attn_ref[r0:r0 + CHUNK, j * GQ_W + LANES:(j + 1) * GQ_W] = hi
        while len(filled) * n_chunks < (c + 1) * len(fillers):
            filled.append(fillers[len(filled)]())
    bb, cc, hh, gb, xq, gx, gate_b, gate_x = filled
    kr_ref[:, 0:WINDOW, :] = kr_ref[:, tile:tile + WINDOW, :]
    vall_ref[0:WINDOW, :] = vall_ref[tile:tile + WINDOW, :]

    u = cc * hh
    uc_ref[CONV_PAD:CONV_PAD + tile, :] = u
    ul_ref[0] = u[tile - (CONV_K - 1):tile]
    conv = (uc_ref[CONV_PAD - 2:CONV_PAD - 2 + tile, :] * cw_ref[0:1, :]
            + uc_ref[CONV_PAD - 1:CONV_PAD - 1 + tile, :] * cw_ref[1:2, :]
            + u * cw_ref[2:3, :]) + cb_ref[...]
    uc_ref[0:CONV_PAD, :] = uc_ref[tile:tile + CONV_PAD, :]
    zb = (bb * conv * _silu(gb)).astype(BF16)
    za = (attn_ref[...] * _silu(ga)).astype(BF16)

    xq = (xq * (X_HEAD_DIM ** -0.5)).astype(BF16)
    merge_work = [lambda: _dot(za, wpa_ref[...]),
                  lambda: jax.nn.sigmoid(proj(OFF_MG)),
                  lambda: _dot(zb, wpb_ref[...])]
    merged = []
    heads = []
    for h in range(X_HEADS):
        heads.append(_cross_head(xq[:, h * X_HEAD_DIM:(h + 1) * X_HEAD_DIM],
                                 mk_ref[0, :, h * X_HEAD_DIM:(h + 1) * X_HEAD_DIM], mvx_ref[0, h]))
        if h < len(merge_work):
            merged.append(merge_work[h]())
    ya, gate_a, yb = merged
    zx = (jnp.concatenate(heads, axis=1) * _silu(gx)).astype(BF16)
    acc = gate_a * ya + gate_b * yb + gate_x * _dot(zx, wpx_ref[...])
    out = x + _dot(acc.astype(BF16), wout_ref[...])
    if final:
        out = _rms_scale(out) * fg_ref[...]
    xo_ref[0] = out


def _const_spec(shape):
    nd = len(shape)
    return pl.BlockSpec(shape, lambda *_: (0,) * nd, pipeline_mode=pl.Buffered(1))


def _layer_spec(shape):
    nd = len(shape)
    return pl.BlockSpec((None,) + tuple(shape), lambda *a: (a[-1][0],) + (0,) * nd, pipeline_mode=pl.Buffered(1))


def _sample_step(x_ref, g_ref, win_refs, wpa_ref, wpb_ref, wpx_ref, wout_ref, cw_ref, cb_ref,
                 bias_ref, bd_ref, ck_ref, cv_ref, cc_ref, cmk_ref, cmv_ref, fg_ref,
                 xo_ref, ko_ref, vo_ref, uo_ref,
                 uc_ref, attn_ref, conv_ref, xat_ref, *, nb, nt, final):
    x = x_ref[...]
    xn = (_rms_scale(x) * g_ref[...]).astype(BF16)

    def proj(off):
        return _dot(xn, _w_in(win_refs, off))

    q = (proj(OFF_Q) * (HEAD_DIM ** -0.5)).astype(BF16)
    kv = proj(OFF_K)
    k = kv[:, 0:KV_W]
    v = kv[:, KV_W:2 * KV_W]
    ga = proj(OFF_GA)
    ko_ref[...] = k
    vo_ref[...] = v
    ones = jnp.ones((WINDOW + nt, LANES), BF16)
    ktail, vtail = _key_tails(WINDOW + nt)
    fillers = [functools.partial(proj, off) for off in (OFF_BB, OFF_CC, OFF_HH, OFF_GB, OFF_XQ, OFF_GX)]
    filled = []
    for b in range(nb):
        rows = slice(b * nt, (b + 1) * nt)
        kall = _dup_heads(jnp.concatenate([ck_ref[b], k[rows]], axis=0))
        vall = _dup_heads(jnp.concatenate([cv_ref[b], v[rows]], axis=0))
        for j in range(N_KV_HEADS):
            kj = kall[j].astype(BF16)
            vj = vall[j].astype(BF16)
            qg = q[rows, j * GQ_W:(j + 1) * GQ_W]
            qbd = jnp.concatenate([qg] * GROUP, axis=0) * bd_ref[...]
            on = _band_attention(qbd, jnp.concatenate([jnp.concatenate([kj, kj], axis=1), ktail], axis=0),
                                 jnp.concatenate([jnp.concatenate([vj, ones], axis=1), vtail], axis=0),
                                 bias_ref[j])
            lo, hi = _ungroup(on, nt)
            attn_ref[rows, j * GQ_W:j * GQ_W + LANES] = lo
            attn_ref[rows, j * GQ_W + LANES:(j + 1) * GQ_W] = hi
        while len(filled) * nb < (b + 1) * len(fillers):
            filled.append(fillers[len(filled)]())
    bb, cc, hh, gb, p3q, gx = filled
    za = attn_ref[...] * _silu(ga)

    u = cc * hh
    for b in range(nb):
        rows = slice(b * nt, (b + 1) * nt)
        ub = u[rows]
        uc_ref[b, CONV_PAD - (CONV_K - 1):CONV_PAD, :] = cc_ref[b]
        uc_ref[b, CONV_PAD:CONV_PAD + nt, :] = ub
        uo_ref[b] = ub[nt - (CONV_K - 1):nt]
        conv_ref[rows, :] = (uc_ref[b, CONV_PAD - 2:CONV_PAD - 2 + nt, :] * cw_ref[0:1, :]
                             + uc_ref[b, CONV_PAD - 1:CONV_PAD - 1 + nt, :] * cw_ref[1:2, :]
                             + ub * cw_ref[2:3, :]) + cb_ref[...]
    zb = bb * conv_ref[...] * _silu(gb)

    xq = (p3q * (X_HEAD_DIM ** -0.5)).astype(BF16)
    mones = jnp.ones((N_MEM, X_HEAD_DIM), BF16)
    merge_work = [lambda: _dot(za.astype(BF16), wpa_ref[...]),
                  lambda: jax.nn.sigmoid(proj(OFF_MG)),
                  lambda: _dot(zb.astype(BF16), wpb_ref[...]),
                  lambda: jax.nn.sigmoid(proj(OFF_MG + D_MODEL)),
                  lambda: jax.nn.sigmoid(proj(OFF_MG + 2 * D_MODEL))]
    merged = []
    for b in range(nb):
        rows = slice(b * nt, (b + 1) * nt)
        for h in range(X_HEADS):
            cols = slice(h * X_HEAD_DIM, (h + 1) * X_HEAD_DIM)
            head_rows = pl.ds(h, N_MEM, stride=X_HEADS)
            mvx = jnp.concatenate([cmv_ref[b, head_rows, :].astype(BF16), mones], axis=1)
            xat_ref[rows, cols] = _cross_head(xq[rows, cols], cmk_ref[b, head_rows, :].astype(BF16), mvx)
        if b < len(merge_work):
            merged.append(merge_work[b]())
    ya, gate_a, yb, gate_b, gate_x = merged
    zx = xat_ref[...] * _silu(gx)

    acc = gate_a * ya + gate_b * yb + gate_x * _dot(zx.astype(BF16), wpx_ref[...])
    out = x + _dot(acc.astype(BF16), wout_ref[...])
    if final:
        out = _rms_scale(out) * fg_ref[...]
    xo_ref[...] = out


def _layer_kernel(layer_ref, x_ref, g_ref, win32_ref, wpa32_ref, wpb32_ref, wpx32_ref, wout32_ref,
                  cw_ref, cb_ref, fg_ref, biast_ref, sink_ref, bd_ref, mk_ref, mvx_ref,
                  xs_ref, sbias_ref, sbd_ref, ck_ref, cv_ref, cc_ref, cmk_ref, cmv_ref,
                  xo_ref, kl_ref, vl_ref, ul_ref, xso_ref, ko_ref, vo_ref, uo_ref,
                  w256_ref, w512_ref, w1024_ref, wpa_ref, wpb_ref, wpx_ref, wout_ref,
                  kr_ref, vall_ref, uc_ref, q_ref, attn_ref, suc_ref, sattn_ref, sconv_ref, sxat_ref,
                  *, tile, tiles_per_seq, nb, nt, final):
    i = pl.program_id(0)
    win_refs = (w256_ref, w512_ref, w1024_ref)
    weights = (win_refs, wpa_ref, wpb_ref, wpx_ref, wout_ref)

    @pl.when(i < CAST_STEPS)
    def _():
        def chunk_rows(src):
            rows = src.shape[0]
            return pl.ds(pl.multiple_of(i * rows, rows), rows)

        for ref, (width, offs) in zip(win_refs, W_IN_GROUPS):
            for n, off in enumerate(offs):
                ref[n, chunk_rows(win32_ref), :] = win32_ref[:, off:off + width].astype(BF16)
        for src, dst in zip((wpa32_ref, wpb32_ref, wpx32_ref, wout32_ref), weights[1:]):
            dst[chunk_rows(src), :] = src[...].astype(BF16)

    @pl.when(i == CAST_STEPS)
    def _():
        _sample_step(xs_ref, g_ref, *weights, cw_ref, cb_ref,
                     sbias_ref, sbd_ref, ck_ref, cv_ref, cc_ref, cmk_ref, cmv_ref, fg_ref,
                     xso_ref, ko_ref, vo_ref, uo_ref, suc_ref, sattn_ref, sconv_ref, sxat_ref,
                     nb=nb, nt=nt, final=final)

    @pl.when(i > CAST_STEPS)
    def _():
        _prompt_tile(lax.rem(i - (CAST_STEPS + 1), tiles_per_seq), x_ref, g_ref, *weights, cw_ref, cb_ref,
                     biast_ref, sink_ref, bd_ref, mk_ref, mvx_ref, fg_ref,
                     xo_ref, kl_ref, vl_ref, ul_ref, kr_ref, vall_ref, uc_ref, q_ref, attn_ref,
                     tile=tile, final=final)


def _chunk_spec(rows, cols):
    assert rows % CAST_STEPS == 0
    return pl.BlockSpec((None, rows // CAST_STEPS, cols),
                        lambda i, l: (l[0], jnp.minimum(i, CAST_STEPS - 1), 0))


_WEIGHT_SHAPES = ((D_MODEL, IN_W), (ATT_W, D_MODEL), (CONV_W, D_MODEL), (X_W, D_MODEL), (D_MODEL, D_MODEL))
_RESIDENT_SHAPES = tuple((len(offs), D_MODEL, width) for width, offs in W_IN_GROUPS) + _WEIGHT_SHAPES[1:]


def _layer(layer, x, xs, g, win, wpa, wpb, wpx, wout, cw, cb, fg, bias_t, sink, bd, mkb, mvx,
           sbias, sbd, ck, cv, cc, cmk, cmv, *, nt, final):
    nb, seq, _ = x.shape
    rows = xs.shape[0]
    nbs = rows // nt
    tile = PROMPT_TILE
    assert seq % tile == 0 and tile % WINDOW == 0
    tps = seq // tile
    kern = functools.partial(_layer_kernel, tile=tile, tiles_per_seq=tps, nb=nbs, nt=nt, final=final)

    def prompt_step(i):
        return jnp.maximum(i - (CAST_STEPS + 1), 0)

    def seq_of(i):
        return prompt_step(i) // tps

    def tile_of(i):
        return prompt_step(i) % tps

    grid_spec = pltpu.PrefetchScalarGridSpec(
        num_scalar_prefetch=1,
        grid=(CAST_STEPS + 1 + nb * tps,),
        in_specs=[
            pl.BlockSpec((1, tile, D_MODEL), lambda i, l: (seq_of(i), tile_of(i), 0)),
            _layer_spec((1, D_MODEL)),
            *[_chunk_spec(r, c) for r, c in _WEIGHT_SHAPES],
            _layer_spec((CONV_K, CONV_W)),
            _layer_spec((1, CONV_W)),
            _const_spec((1, D_MODEL)),
            _const_spec((3, N_KV_HEADS, BAND, GROUP * CHUNK)),
            _layer_spec((N_KV_HEADS, 1, GROUP * CHUNK)),
            _const_spec((GROUP * CHUNK, GQ_W)),
            pl.BlockSpec((None, 1, N_MEM, X_W), lambda i, l: (l[0], seq_of(i), 0, 0)),
            pl.BlockSpec((None, 1, X_HEADS, N_MEM, 2 * X_HEAD_DIM), lambda i, l: (l[0], seq_of(i), 0, 0, 0)),
            _const_spec((rows, D_MODEL)),
            _layer_spec((N_KV_HEADS, GROUP * nt, KEY_SLOTS)),
            _const_spec((GROUP * nt, GQ_W)),
            _layer_spec((nbs, WINDOW, KV_W)),
            _layer_spec((nbs, WINDOW, KV_W)),
            _layer_spec((nbs, CONV_K - 1, CONV_W)),
            _layer_spec((nbs, N_MEM * X_HEADS, X_HEAD_DIM)),
            _layer_spec((nbs, N_MEM * X_HEADS, X_HEAD_DIM)),
        ],
        out_specs=[
            pl.BlockSpec((1, tile, D_MODEL), lambda i, l: (seq_of(i), tile_of(i), 0)),
            pl.BlockSpec((1, WINDOW, KV_W), lambda i, l: (seq_of(i), 0, 0)),
            pl.BlockSpec((1, WINDOW, KV_W), lambda i, l: (seq_of(i), 0, 0)),
            pl.BlockSpec((1, CONV_K - 1, CONV_W), lambda i, l: (seq_of(i), 0, 0)),
            pl.BlockSpec((rows, D_MODEL), lambda i, l: (0, 0)),
            pl.BlockSpec((rows, KV_W), lambda i, l: (0, 0)),
            pl.BlockSpec((rows, KV_W), lambda i, l: (0, 0)),
            pl.BlockSpec((nbs, CONV_K - 1, CONV_W), lambda i, l: (0, 0, 0)),
        ],
        scratch_shapes=[
            *[pltpu.VMEM(shape, BF16) for shape in _RESIDENT_SHAPES],
            pltpu.VMEM((N_KV_HEADS, WINDOW + tile, GQ_W), BF16),
            pltpu.VMEM((WINDOW + tile + CHUNK, KV_W), F32),
            pltpu.VMEM((CONV_PAD + tile, CONV_W), F32),
            pltpu.VMEM((tile, ATT_W), BF16),
            pltpu.VMEM((tile, ATT_W), F32),
            pltpu.VMEM((nbs, CONV_PAD + nt, CONV_W), F32),
            pltpu.VMEM((rows, ATT_W), F32),
            pltpu.VMEM((rows, CONV_W), F32),
            pltpu.VMEM((rows, X_W), F32),
        ],
    )
    return pl.pallas_call(
        kern,
        grid_spec=grid_spec,
        out_shape=[
            jax.ShapeDtypeStruct((nb, seq, D_MODEL), F32),
            jax.ShapeDtypeStruct((nb, WINDOW, KV_W), F32),
            jax.ShapeDtypeStruct((nb, WINDOW, KV_W), F32),
            jax.ShapeDtypeStruct((nb, CONV_K - 1, CONV_W), F32),
            jax.ShapeDtypeStruct((rows, D_MODEL), F32),
            jax.ShapeDtypeStruct((rows, KV_W), F32),
            jax.ShapeDtypeStruct((rows, KV_W), F32),
            jax.ShapeDtypeStruct((nbs, CONV_K - 1, CONV_W), F32),
        ],
        compiler_params=pltpu.CompilerParams(
            dimension_semantics=("arbitrary",), vmem_limit_bytes=VMEM_LIMIT),
        name="layer",
    )(layer, x, g, win, wpa, wpb, wpx, wout, cw, cb, fg, bias_t, sink, bd, mkb, mvx,
      xs, sbias, sbd, ck, cv, cc, cmk, cmv)


def _alibi_slopes():
    return np.power(2.0, -8.0 * np.arange(1, N_HEADS + 1, dtype=np.float32) / N_HEADS).astype(np.float32)


def _alibi_bias(nq, nk, masked):
    dist = np.abs(np.arange(nq)[:, None] + WINDOW - np.arange(nk)[None, :]).astype(np.float32)
    slope = _alibi_slopes().reshape(N_KV_HEADS, GROUP)
    bias = -(slope[:, :, None, None] * dist[None, None])
    bias = np.where(np.arange(nk)[None, None, None, :] < masked, -np.inf, bias)
    return bias.reshape(N_KV_HEADS, GROUP * nq, nk).astype(np.float32)


def _bias_with_sink(alibi, attn_sink, nq):
    nk = alibi.shape[-1]
    lead = alibi.shape[:-3]
    sink = jnp.repeat(attn_sink.reshape(DEPTH, N_KV_HEADS, GROUP), nq, axis=2)[..., None]
    sink = jnp.broadcast_to(sink.reshape((DEPTH,) + (1,) * len(lead) + sink.shape[1:]),
                            (DEPTH,) + alibi.shape[:-1] + (1,))
    real = jnp.broadcast_to(jnp.asarray(alibi), (DEPTH,) + alibi.shape)
    unused = jnp.full((DEPTH,) + alibi.shape[:-1] + (KEY_SLOTS - nk - 1,), -jnp.inf, F32)
    return jnp.concatenate([real, sink, unused], axis=-1)


def _block_diag_mask(nq):
    rows = np.arange(GROUP * nq)[:, None] // nq
    cols = np.arange(GQ_W)[None, :] // HEAD_DIM
    return (rows == cols).astype(np.float32)


def kernel(x_prompt, x_sample, cache_attn_k, cache_attn_v, cache_conv, cache_mem_k, cache_mem_v,
           mem_prompt, norm_g, w_in, attn_sink, w_pa, conv_w, conv_b, w_pb, mem_norm_g,
           w_mk, w_mv, w_px, w_out, final_g):
    bp, seq, _ = x_prompt.shape
    bs, nt, _ = x_sample.shape
    win_len = cache_attn_k.shape[2]
    assert win_len == WINDOW and nt <= CHUNK and seq >= WINDOW

    bias_p = jnp.asarray(np.stack([_alibi_bias(CHUNK, BAND, m) for m in (WINDOW, CHUNK, 0)]).transpose(0, 1, 3, 2))
    sink_p = jnp.repeat(attn_sink.reshape(DEPTH, N_KV_HEADS, 1, GROUP), CHUNK, axis=3)
    bias_s = _bias_with_sink(_alibi_bias(nt, WINDOW + nt, 0), attn_sink, nt)
    bd_p = jnp.asarray(_block_diag_mask(CHUNK), BF16)
    bd_s = jnp.asarray(_block_diag_mask(nt), BF16)
    fg = final_g.reshape(1, D_MODEL)

    mk, mv, mkb, mvx = _memory_kv(mem_prompt, mem_norm_g, w_mk, w_mv)

    xp = x_prompt
    xs = x_sample.reshape(bs * nt, D_MODEL)
    ck = cache_attn_k.reshape(DEPTH, bs, WINDOW, KV_W)
    cv = cache_attn_v.reshape(DEPTH, bs, WINDOW, KV_W)
    g = norm_g.reshape(DEPTH, 1, D_MODEL)
    cb = conv_b.reshape(DEPTH, 1, CONV_W)
    cmk = cache_mem_k.reshape(DEPTH, bs, N_MEM * X_HEADS, X_HEAD_DIM)
    cmv = cache_mem_v.reshape(DEPTH, bs, N_MEM * X_HEADS, X_HEAD_DIM)
    pk, pv, pc, sk, sv, sc = [], [], [], [], [], []
    for l in range(DEPTH):
        final = l == DEPTH - 1
        layer = jnp.full((1,), l, jnp.int32)
        xp, kl, vl, ul, xs, ko, vo, uo = _layer(
            layer, xp, xs, g, w_in, w_pa, w_pb, w_px, w_out, conv_w, cb, fg, bias_p, sink_p, bd_p, mkb, mvx,
            bias_s, bd_s, ck, cv, cache_conv, cmk, cmv, nt=nt, final=final)
        pk.append(kl)
        pv.append(vl)
        pc.append(ul)
        sk.append(ko)
        sv.append(vo)
        sc.append(uo)

    kv_p = (DEPTH, bp, WINDOW, N_KV_HEADS, HEAD_DIM)
    kv_s = (DEPTH, bs, nt, N_KV_HEADS, HEAD_DIM)
    mem_shape = (DEPTH, bp, N_MEM, X_HEADS, X_HEAD_DIM)
    return (xp, xs.reshape(bs, nt, D_MODEL),
            jnp.stack(pk).reshape(kv_p), jnp.stack(pv).reshape(kv_p), jnp.stack(pc),
            mk.reshape(mem_shape), mv.reshape(mem_shape),
            jnp.stack(sk).reshape(kv_s), jnp.stack(sv).reshape(kv_s), jnp.stack(sc))
```

```python
import functools

import numpy as np
import jax
import jax.numpy as jnp
from jax import lax
from jax.experimental import pallas as pl
from jax.experimental.pallas import tpu as pltpu

D_MODEL = 1024
DEPTH = 4
CHUNK = 64
WINDOW = 128
N_HEADS = 8
N_KV_HEADS = 2
GROUP = N_HEADS // N_KV_HEADS
HEAD_DIM = 64
ATT_W = N_HEADS * HEAD_DIM
KV_W = N_KV_HEADS * HEAD_DIM
CONV_W = 512
CONV_K = 3
X_HEADS = 4
X_HEAD_DIM = 128
X_W = X_HEADS * X_HEAD_DIM
N_MEM = 256
EPS = 1e-6

OFF_Q = 0
OFF_K = OFF_Q + ATT_W
OFF_V = OFF_K + KV_W
OFF_GA = OFF_V + KV_W
OFF_BB = OFF_GA + ATT_W
OFF_CC = OFF_BB + CONV_W
OFF_HH = OFF_CC + CONV_W
OFF_GB = OFF_HH + CONV_W
OFF_XQ = OFF_GB + CONV_W
OFF_GX = OFF_XQ + X_W
OFF_MG = OFF_GX + X_W
IN_W = OFF_MG + 3 * D_MODEL
W256_OFFS = (OFF_K,)
W512_OFFS = (OFF_Q, OFF_GA, OFF_BB, OFF_CC, OFF_HH, OFF_GB, OFF_XQ, OFF_GX)
W1024_OFFS = (OFF_MG, OFF_MG + D_MODEL, OFF_MG + 2 * D_MODEL)
W_IN_GROUPS = ((256, W256_OFFS), (512, W512_OFFS), (1024, W1024_OFFS))

LANES = 128
GQ_W = GROUP * HEAD_DIM
BAND = WINDOW + CHUNK
KEY_SLOTS = 256
CONV_PAD = 8
PROMPT_TILE = 512
Q_ROWS = 128
CAST_STEPS = 16
VMEM_LIMIT = 60 * 1024 * 1024

F32 = jnp.float32
BF16 = jnp.bfloat16
_NT = (((1,), (1,)), ((), ()))


def _rms_scale(x):
    return x * lax.rsqrt(jnp.mean(x * x, axis=-1, keepdims=True) + EPS)


def _silu(x):
    return x * jax.nn.sigmoid(x)


def _dot(a, b):
    return jnp.dot(a, b, preferred_element_type=F32)


def _w_in(win_refs, off):
    for ref, (_, offs) in zip(win_refs, W_IN_GROUPS):
        if off in offs:
            return ref[offs.index(off)]
    raise ValueError(off)


def _dup_heads(a):
    lane = lax.broadcasted_iota(jnp.int32, a.shape, 1)
    sw = pltpu.roll(a, HEAD_DIM, 1)
    lo = lane < HEAD_DIM
    return jnp.where(lo, a, sw), jnp.where(lo, sw, a)


def _key_tails(nk):
    shape = (KEY_SLOTS - nk, 2 * LANES)
    row = lax.broadcasted_iota(jnp.int32, shape, 0)
    col = lax.broadcasted_iota(jnp.int32, shape, 1)
    vtail = jnp.where((row == 0) & (col >= LANES), 1.0, 0.0).astype(BF16)
    return jnp.zeros(shape, BF16), vtail


def _band_attention(qbd, kb, vb, bias):
    s = lax.dot_general(qbd, kb, _NT, preferred_element_type=F32) + bias
    p = jnp.exp(s - jnp.max(s, axis=-1, keepdims=True)).astype(BF16)
    oe = _dot(p, vb)
    return oe[:, 0:LANES] / oe[:, LANES:2 * LANES]


def _ungroup(on, nq):
    lo = lax.broadcasted_iota(jnp.int32, (nq, LANES), 1) < HEAD_DIM
    return (jnp.where(lo, on[0:nq], on[nq:2 * nq]),
            jnp.where(lo, on[2 * nq:3 * nq], on[3 * nq:4 * nq]))


def _cross_head(xq_h, mk_h, mvx_h):
    s = lax.dot_general(xq_h, mk_h, _NT, preferred_element_type=F32)
    p = jnp.exp(s - jnp.max(s, axis=-1, keepdims=True)).astype(BF16)
    oe = _dot(p, mvx_h)
    return oe[:, 0:X_HEAD_DIM] / oe[:, X_HEAD_DIM:2 * X_HEAD_DIM]


def _memkv_kernel(mem_ref, g_ref, wk_ref, wv_ref, mk_ref, mv_ref, mkb_ref, mvx_ref):
    nb = mem_ref.shape[0]
    mn = (_rms_scale(mem_ref[...].reshape(nb * N_MEM, D_MODEL)) * g_ref[0]).astype(BF16)
    mk = _dot(mn, wk_ref[0].astype(BF16))
    mv = _dot(mn, wv_ref[0].astype(BF16))
    ones = jnp.ones((N_MEM, X_HEAD_DIM), BF16)
    for b in range(nb):
        rows = slice(b * N_MEM, (b + 1) * N_MEM)
        mkb_ref[0, b] = mk[rows].astype(BF16)
        for h in range(X_HEADS):
            cols = slice(h * X_HEAD_DIM, (h + 1) * X_HEAD_DIM)
            mk_ref[0, b, pl.ds(h, N_MEM, stride=X_HEADS), :] = mk[rows, cols]
            mv_ref[0, b, pl.ds(h, N_MEM, stride=X_HEADS), :] = mv[rows, cols]
            mvx_ref[0, b, h, :, 0:X_HEAD_DIM] = mv[rows, cols].astype(BF16)
            mvx_ref[0, b, h, :, X_HEAD_DIM:2 * X_HEAD_DIM] = ones


def _memory_kv(mem, g, wk, wv):
    nb = mem.shape[0]
    return pl.pallas_call(
        _memkv_kernel,
        grid=(DEPTH,),
        in_specs=[
            pl.BlockSpec((nb, N_MEM, D_MODEL), lambda l: (0, 0, 0)),
            pl.BlockSpec((1, 1, D_MODEL), lambda l: (l, 0, 0)),
            pl.BlockSpec((1, D_MODEL, X_W), lambda l: (l, 0, 0)),
            pl.BlockSpec((1, D_MODEL, X_W), lambda l: (l, 0, 0)),
        ],
        out_specs=[
            pl.BlockSpec((1, nb, N_MEM * X_HEADS, X_HEAD_DIM), lambda l: (l, 0, 0, 0)),
            pl.BlockSpec((1, nb, N_MEM * X_HEADS, X_HEAD_DIM), lambda l: (l, 0, 0, 0)),
            pl.BlockSpec((1, nb, N_MEM, X_W), lambda l: (l, 0, 0, 0)),
            pl.BlockSpec((1, nb, X_HEADS, N_MEM, 2 * X_HEAD_DIM), lambda l: (l, 0, 0, 0, 0)),
        ],
        out_shape=[
            jax.ShapeDtypeStruct((DEPTH, nb, N_MEM * X_HEADS, X_HEAD_DIM), F32),
            jax.ShapeDtypeStruct((DEPTH, nb, N_MEM * X_HEADS, X_HEAD_DIM), F32),
            jax.ShapeDtypeStruct((DEPTH, nb, N_MEM, X_W), BF16),
            jax.ShapeDtypeStruct((DEPTH, nb, X_HEADS, N_MEM, 2 * X_HEAD_DIM), BF16),
        ],
        name="memory_kv",
    )(mem, g.reshape(DEPTH, 1, D_MODEL), wk, wv)


def _prompt_tile(t, x_ref, g_ref, win_refs, wpa_ref, wpb_ref, wpx_ref, wout_ref, cw_ref, cb_ref,
                 bias_ref, bd_ref, mk_ref, mvx_ref, fg_ref,
                 xo_ref, kl_ref, vl_ref, ul_ref,
                 kr_ref, vv_ref, uc_ref, q_ref, attn_ref, *, tile, final):
    @pl.when(t == 0)
    def _():
        kr_ref[:, 0:WINDOW, :] = jnp.zeros((N_KV_HEADS, WINDOW, GQ_W), BF16)
        vv_ref[:, 0:WINDOW, 0:LANES] = jnp.zeros((N_KV_HEADS, WINDOW, LANES), BF16)
        vv_ref[:, :, LANES:2 * LANES] = jnp.ones((N_KV_HEADS, WINDOW + tile, LANES), BF16)
        uc_ref[0:CONV_PAD, :] = jnp.zeros((CONV_PAD, CONV_W), F32)

    x = x_ref[0]
    xn = (_rms_scale(x) * g_ref[...]).astype(BF16)

    def proj(off):
        return _dot(xn, _w_in(win_refs, off))

    w_q = _w_in(win_refs, OFF_Q)
    for r in range(0, tile, Q_ROWS):
        q_ref[r:r + Q_ROWS, :] = (_dot(xn[r:r + Q_ROWS], w_q) * (HEAD_DIM ** -0.5)).astype(BF16)
    half = tile // 2
    w_kv = _w_in(win_refs, OFF_K)
    kv = jnp.concatenate([_dot(xn[0:half], w_kv), _dot(xn[half:tile], w_kv)], axis=0)
    k = kv[:, 0:KV_W]
    v = kv[:, KV_W:2 * KV_W]
    ga = proj(OFF_GA)
    kl_ref[0] = k[tile - WINDOW:tile]
    vl_ref[0] = v[tile - WINDOW:tile]
    for j, (kj, vj) in enumerate(zip(_dup_heads(k), _dup_heads(v))):
        kj = kj.astype(BF16)
        kr_ref[j, WINDOW:WINDOW + tile, 0:LANES] = kj
        kr_ref[j, WINDOW:WINDOW + tile, LANES:2 * LANES] = kj
        vv_ref[j, WINDOW:WINDOW + tile, 0:LANES] = vj.astype(BF16)

    fillers = [functools.partial(proj, off) for off in (OFF_BB, OFF_CC, OFF_HH, OFF_GB, OFF_XQ, OFF_GX)]
    fillers += [lambda i=i: jax.nn.sigmoid(proj(OFF_MG + i * D_MODEL)) for i in (1, 2)]
    filled = []
    n_chunks = tile // CHUNK
    ktail, vtail = _key_tails(BAND)

    for c in range(n_chunks):
        r0 = c * CHUNK
        qc = q_ref[r0:r0 + CHUNK, :]
        variant = jnp.where(t == 0, c, 2) if c < 2 else 2
        for j in range(N_KV_HEADS):
            qg = qc[:, j * GQ_W:(j + 1) * GQ_W]
            qbd = jnp.concatenate([qg] * GROUP, axis=0) * bd_ref[...]
            on = _band_attention(qbd, jnp.concatenate([kr_ref[j, r0:r0 + BAND, :], ktail], axis=0),
                                 jnp.concatenate([vv_ref[j, r0:r0 + BAND, :], vtail], axis=0),
                                 bias_ref[variant, j])
            lo, hi = _ungroup(on, CHUNK)
            attn_ref[r0:r0 + CHUNK, j * GQ_W:j * GQ_W + LANES] = lo
            attn_ref[r0:r0 + CHUNK, j * GQ_W + LANES:(j + 1) * GQ_W] = hi
        while len(filled) * n_chunks < (c + 1) * len(fillers):
            filled.append(fillers[len(filled)]())
    bb, cc, hh, gb, xq, gx, gate_b, gate_x = filled
    kr_ref[:, 0:WINDOW, :] = kr_ref[:, tile:tile + WINDOW, :]
    vv_ref[:, 0:WINDOW, 0:LANES] = vv_ref[:, tile:tile + WINDOW, 0:LANES]

    u = cc * hh
    uc_ref[CONV_PAD:CONV_PAD + tile, :] = u
    ul_ref[0] = u[tile - (CONV_K - 1):tile]
    conv = (uc_ref[CONV_PAD - 2:CONV_PAD - 2 + tile, :] * cw_ref[0:1, :]
            + uc_ref[CONV_PAD - 1:CONV_PAD - 1 + tile, :] * cw_ref[1:2, :]
            + u * cw_ref[2:3, :]) + cb_ref[...]
    uc_ref[0:CONV_PAD, :] = uc_ref[tile:tile + CONV_PAD, :]
    zb = (bb * conv * _silu(gb)).astype(BF16)
    za = (attn_ref[...] * _silu(ga)).astype(BF16)

    xq = (xq * (X_HEAD_DIM ** -0.5)).astype(BF16)
    merge_work = [lambda: _dot(za, wpa_ref[...]),
                  lambda: jax.nn.sigmoid(proj(OFF_MG)),
                  lambda: _dot(zb, wpb_ref[...])]
    merged = []
    heads = []
    for h in range(X_HEADS):
        heads.append(_cross_head(xq[:, h * X_HEAD_DIM:(h + 1) * X_HEAD_DIM],
                                 mk_ref[0, :, h * X_HEAD_DIM:(h + 1) * X_HEAD_DIM], mvx_ref[0, h]))
        if h < len(merge_work):
            merged.append(merge_work[h]())
    ya, gate_a, yb = merged
    zx = (jnp.concatenate(heads, axis=1) * _silu(gx)).astype(BF16)
    acc = gate_a * ya + gate_b * yb + gate_x * _dot(zx, wpx_ref[...])
    out = x + _dot(acc.astype(BF16), wout_ref[...])
    if final:
        out = _rms_scale(out) * fg_ref[...]
    xo_ref[0] = out


def _const_spec(shape):
    nd = len(shape)
    return pl.BlockSpec(shape, lambda *_: (0,) * nd, pipeline_mode=pl.Buffered(1))


def _layer_spec(shape):
    nd = len(shape)
    return pl.BlockSpec((None,) + tuple(shape), lambda *a: (a[-1][0],) + (0,) * nd, pipeline_mode=pl.Buffered(1))


def _sample_step(x_ref, g_ref, win_refs, wpa_ref, wpb_ref, wpx_ref, wout_ref, cw_ref, cb_ref,
                 bias_ref, bd_ref, ck_ref, cv_ref, cc_ref, cmk_ref, cmv_ref, fg_ref,
                 xo_ref, ko_ref, vo_ref, uo_ref,
                 uc_ref, attn_ref, conv_ref, xat_ref, *, nb, nt, final):
    x = x_ref[...]
    xn = (_rms_scale(x) * g_ref[...]).astype(BF16)

    def proj(off):
        return _dot(xn, _w_in(win_refs, off))

    q = (proj(OFF_Q) * (HEAD_DIM ** -0.5)).astype(BF16)
    kv = proj(OFF_K)
    k = kv[:, 0:KV_W]
    v = kv[:, KV_W:2 * KV_W]
    ga = proj(OFF_GA)
    ko_ref[...] = k
    vo_ref[...] = v
    ones = jnp.ones((WINDOW + nt, LANES), BF16)
    ktail, vtail = _key_tails(WINDOW + nt)
    fillers = [functools.partial(proj, off) for off in (OFF_BB, OFF_CC, OFF_HH, OFF_GB, OFF_XQ, OFF_GX)]
    filled = []
    for b in range(nb):
        rows = slice(b * nt, (b + 1) * nt)
        kall = _dup_heads(jnp.concatenate([ck_ref[b], k[rows]], axis=0))
        vall = _dup_heads(jnp.concatenate([cv_ref[b], v[rows]], axis=0))
        for j in range(N_KV_HEADS):
            kj = kall[j].astype(BF16)
            vj = vall[j].astype(BF16)
            qg = q[rows, j * GQ_W:(j + 1) * GQ_W]
            qbd = jnp.concatenate([qg] * GROUP, axis=0) * bd_ref[...]
            on = _band_attention(qbd, jnp.concatenate([jnp.concatenate([kj, kj], axis=1), ktail], axis=0),
                                 jnp.concatenate([jnp.concatenate([vj, ones], axis=1), vtail], axis=0),
                                 bias_ref[j])
            lo, hi = _ungroup(on, nt)
            attn_ref[rows, j * GQ_W:j * GQ_W + LANES] = lo
            attn_ref[rows, j * GQ_W + LANES:(j + 1) * GQ_W] = hi
        while len(filled) * nb < (b + 1) * len(fillers):
            filled.append(fillers[len(filled)]())
    bb, cc, hh, gb, p3q, gx = filled
    za = attn_ref[...] * _silu(ga)

    u = cc * hh
    for b in range(nb):
        rows = slice(b * nt, (b + 1) * nt)
        ub = u[rows]
        uc_ref[b, CONV_PAD - (CONV_K - 1):CONV_PAD, :] = cc_ref[b]
        uc_ref[b, CONV_PAD:CONV_PAD + nt, :] = ub
        uo_ref[b] = ub[nt - (CONV_K - 1):nt]
        conv_ref[rows, :] = (uc_ref[b, CONV_PAD - 2:CONV_PAD - 2 + nt, :] * cw_ref[0:1, :]
                             + uc_ref[b, CONV_PAD - 1:CONV_PAD - 1 + nt, :] * cw_ref[1:2, :]
                             + ub * cw_ref[2:3, :]) + cb_ref[...]
    zb = bb * conv_ref[...] * _silu(gb)

    xq = (p3q * (X_HEAD_DIM ** -0.5)).astype(BF16)
    mones = jnp.ones((N_MEM, X_HEAD_DIM), BF16)
    merge_work = [lambda: _dot(za.astype(BF16), wpa_ref[...]),
                  lambda: jax.nn.sigmoid(proj(OFF_MG)),
                  lambda: _dot(zb.astype(BF16), wpb_ref[...]),
                  lambda: jax.nn.sigmoid(proj(OFF_MG + D_MODEL)),
                  lambda: jax.nn.sigmoid(proj(OFF_MG + 2 * D_MODEL))]
    merged = []
    for b in range(nb):
        rows = slice(b * nt, (b + 1) * nt)
        for h in range(X_HEADS):
            cols = slice(h * X_HEAD_DIM, (h + 1) * X_HEAD_DIM)
            head_rows = pl.ds(h, N_MEM, stride=X_HEADS)
            mvx = jnp.concatenate([cmv_ref[b, head_rows, :].astype(BF16), mones], axis=1)
            xat_ref[rows, cols] = _cross_head(xq[rows, cols], cmk_ref[b, head_rows, :].astype(BF16), mvx)
        if b < len(merge_work):
            merged.append(merge_work[b]())
    ya, gate_a, yb, gate_b, gate_x = merged
    zx = xat_ref[...] * _silu(gx)

    acc = gate_a * ya + gate_b * yb + gate_x * _dot(zx.astype(BF16), wpx_ref[...])
    out = x + _dot(acc.astype(BF16), wout_ref[...])
    if final:
        out = _rms_scale(out) * fg_ref[...]
    xo_ref[...] = out


def _layer_kernel(layer_ref, x_ref, g_ref, win32_ref, wpa32_ref, wpb32_ref, wpx32_ref, wout32_ref,
                  cw_ref, cb_ref, fg_ref, bias_ref, bd_ref, mk_ref, mvx_ref,
                  xs_ref, sbias_ref, sbd_ref, ck_ref, cv_ref, cc_ref, cmk_ref, cmv_ref,
                  xo_ref, kl_ref, vl_ref, ul_ref, xso_ref, ko_ref, vo_ref, uo_ref,
                  w256_ref, w512_ref, w1024_ref, wpa_ref, wpb_ref, wpx_ref, wout_ref,
                  kr_ref, vv_ref, uc_ref, q_ref, attn_ref, suc_ref, sattn_ref, sconv_ref, sxat_ref,
                  *, tile, tiles_per_seq, nb, nt, final):
    i = pl.program_id(0)
    win_refs = (w256_ref, w512_ref, w1024_ref)
    weights = (win_refs, wpa_ref, wpb_ref, wpx_ref, wout_ref)

    @pl.when(i < CAST_STEPS)
    def _():
        def chunk_rows(src):
            rows = src.shape[0]
            return pl.ds(pl.multiple_of(i * rows, rows), rows)

        for ref, (width, offs) in zip(win_refs, W_IN_GROUPS):
            for n, off in enumerate(offs):
                ref[n, chunk_rows(win32_ref), :] = win32_ref[:, off:off + width].astype(BF16)
        for src, dst in zip((wpa32_ref, wpb32_ref, wpx32_ref, wout32_ref), weights[1:]):
            dst[chunk_rows(src), :] = src[...].astype(BF16)

    @pl.when(i > CAST_STEPS)
    def _():
        _prompt_tile(lax.rem(i - (CAST_STEPS + 1), tiles_per_seq), x_ref, g_ref, *weights, cw_ref, cb_ref,
                     bias_ref, bd_ref, mk_ref, mvx_ref, fg_ref,
                     xo_ref, kl_ref, vl_ref, ul_ref, kr_ref, vv_ref, uc_ref, q_ref, attn_ref,
                     tile=tile, final=final)

    @pl.when(i == CAST_STEPS)
    def _():
        _sample_step(xs_ref, g_ref, *weights, cw_ref, cb_ref,
                     sbias_ref, sbd_ref, ck_ref, cv_ref, cc_ref, cmk_ref, cmv_ref, fg_ref,
                     xso_ref, ko_ref, vo_ref, uo_ref, suc_ref, sattn_ref, sconv_ref, sxat_ref,
                     nb=nb, nt=nt, final=final)


def _chunk_spec(rows, cols):
    assert rows % CAST_STEPS == 0
    return pl.BlockSpec((None, rows // CAST_STEPS, cols),
                        lambda i, l: (l[0], jnp.minimum(i, CAST_STEPS - 1), 0))


_WEIGHT_SHAPES = ((D_MODEL, IN_W), (ATT_W, D_MODEL), (CONV_W, D_MODEL), (X_W, D_MODEL), (D_MODEL, D_MODEL))
_RESIDENT_SHAPES = tuple((len(offs), D_MODEL, width) for width, offs in W_IN_GROUPS) + _WEIGHT_SHAPES[1:]


def _layer(layer, x, xs, g, win, wpa, wpb, wpx, wout, cw, cb, fg, bias, bd, mkb, mvx,
           sbias, sbd, ck, cv, cc, cmk, cmv, *, nt, final):
    nb, seq, _ = x.shape
    rows = xs.shape[0]
    nbs = rows // nt
    tile = PROMPT_TILE
    assert seq % tile == 0 and tile % WINDOW == 0
    tps = seq // tile
    kern = functools.partial(_layer_kernel, tile=tile, tiles_per_seq=tps, nb=nbs, nt=nt, final=final)

    def prompt_step(i):
        return jnp.maximum(i - (CAST_STEPS + 1), 0)

    def seq_of(i):
        return prompt_step(i) // tps

    def tile_of(i):
        return prompt_step(i) % tps

    grid_spec = pltpu.PrefetchScalarGridSpec(
        num_scalar_prefetch=1,
        grid=(CAST_STEPS + 1 + nb * tps,),
        in_specs=[
            pl.BlockSpec((1, tile, D_MODEL), lambda i, l: (seq_of(i), tile_of(i), 0)),
            _layer_spec((1, D_MODEL)),
            *[_chunk_spec(r, c) for r, c in _WEIGHT_SHAPES],
            _layer_spec((CONV_K, CONV_W)),
            _layer_spec((1, CONV_W)),
            _const_spec((1, D_MODEL)),
            _layer_spec((3, N_KV_HEADS, GROUP * CHUNK, KEY_SLOTS)),
            _const_spec((GROUP * CHUNK, GQ_W)),
            pl.BlockSpec((None, 1, N_MEM, X_W), lambda i, l: (l[0], seq_of(i), 0, 0)),
            pl.BlockSpec((None, 1, X_HEADS, N_MEM, 2 * X_HEAD_DIM), lambda i, l: (l[0], seq_of(i), 0, 0, 0)),
            _const_spec((rows, D_MODEL)),
            _layer_spec((N_KV_HEADS, GROUP * nt, KEY_SLOTS)),
            _const_spec((GROUP * nt, GQ_W)),
            _layer_spec((nbs, WINDOW, KV_W)),
            _layer_spec((nbs, WINDOW, KV_W)),
            _layer_spec((nbs, CONV_K - 1, CONV_W)),
            _layer_spec((nbs, N_MEM * X_HEADS, X_HEAD_DIM)),
            _layer_spec((nbs, N_MEM * X_HEADS, X_HEAD_DIM)),
        ],
        out_specs=[
            pl.BlockSpec((1, tile, D_MODEL), lambda i, l: (seq_of(i), tile_of(i), 0)),
            pl.BlockSpec((1, WINDOW, KV_W), lambda i, l: (seq_of(i), 0, 0)),
            pl.BlockSpec((1, WINDOW, KV_W), lambda i, l: (seq_of(i), 0, 0)),
            pl.BlockSpec((1, CONV_K - 1, CONV_W), lambda i, l: (seq_of(i), 0, 0)),
            pl.BlockSpec((rows, D_MODEL), lambda i, l: (0, 0)),
            pl.BlockSpec((rows, KV_W), lambda i, l: (0, 0)),
            pl.BlockSpec((rows, KV_W), lambda i, l: (0, 0)),
            pl.BlockSpec((nbs, CONV_K - 1, CONV_W), lambda i, l: (0, 0, 0)),
        ],
        scratch_shapes=[
            *[pltpu.VMEM(shape, BF16) for shape in _RESIDENT_SHAPES],
            pltpu.VMEM((N_KV_HEADS, WINDOW + tile, GQ_W), BF16),
            pltpu.VMEM((N_KV_HEADS, WINDOW + tile, 2 * LANES), BF16),
            pltpu.VMEM((CONV_PAD + tile, CONV_W), F32),
            pltpu.VMEM((tile, ATT_W), BF16),
            pltpu.VMEM((tile, ATT_W), F32),
            pltpu.VMEM((nbs, CONV_PAD + nt, CONV_W), F32),
            pltpu.VMEM((rows, ATT_W), F32),
            pltpu.VMEM((rows, CONV_W), F32),
            pltpu.VMEM((rows, X_W), F32),
        ],
    )
    return pl.pallas_call(
        kern,
        grid_spec=grid_spec,
        out_shape=[
            jax.ShapeDtypeStruct((nb, seq, D_MODEL), F32),
            jax.ShapeDtypeStruct((nb, WINDOW, KV_W), F32),
            jax.ShapeDtypeStruct((nb, WINDOW, KV_W), F32),
            jax.ShapeDtypeStruct((nb, CONV_K - 1, CONV_W), F32),
            jax.ShapeDtypeStruct((rows, D_MODEL), F32),
            jax.ShapeDtypeStruct((rows, KV_W), F32),
            jax.ShapeDtypeStruct((rows, KV_W), F32),
            jax.ShapeDtypeStruct((nbs, CONV_K - 1, CONV_W), F32),
        ],
        compiler_params=pltpu.CompilerParams(
            dimension_semantics=("arbitrary",), vmem_limit_bytes=VMEM_LIMIT),
        name="layer",
    )(layer, x, g, win, wpa, wpb, wpx, wout, cw, cb, fg, bias, bd, mkb, mvx,
      xs, sbias, sbd, ck, cv, cc, cmk, cmv)


def _alibi_slopes():
    return np.power(2.0, -8.0 * np.arange(1, N_HEADS + 1, dtype=np.float32) / N_HEADS).astype(np.float32)


def _alibi_bias(nq, nk, masked):
    dist = np.abs(np.arange(nq)[:, None] + WINDOW - np.arange(nk)[None, :]).astype(np.float32)
    slope = _alibi_slopes().reshape(N_KV_HEADS, GROUP)
    bias = -(slope[:, :, None, None] * dist[None, None])
    bias = np.where(np.arange(nk)[None, None, None, :] < masked, -np.inf, bias)
    return bias.reshape(N_KV_HEADS, GROUP * nq, nk).astype(np.float32)


def _bias_with_sink(alibi, attn_sink, nq):
    nk = alibi.shape[-1]
    lead = alibi.shape[:-3]
    sink = jnp.repeat(attn_sink.reshape(DEPTH, N_KV_HEADS, GROUP), nq, axis=2)[..., None]
    sink = jnp.broadcast_to(sink.reshape((DEPTH,) + (1,) * len(lead) + sink.shape[1:]),
                            (DEPTH,) + alibi.shape[:-1] + (1,))
    real = jnp.broadcast_to(jnp.asarray(alibi), (DEPTH,) + alibi.shape)
    unused = jnp.full((DEPTH,) + alibi.shape[:-1] + (KEY_SLOTS - nk - 1,), -jnp.inf, F32)
    return jnp.concatenate([real, sink, unused], axis=-1)


def _block_diag_mask(nq):
    rows = np.arange(GROUP * nq)[:, None] // nq
    cols = np.arange(GQ_W)[None, :] // HEAD_DIM
    return (rows == cols).astype(np.float32)


def kernel(x_prompt, x_sample, cache_attn_k, cache_attn_v, cache_conv, cache_mem_k, cache_mem_v,
           mem_prompt, norm_g, w_in, attn_sink, w_pa, conv_w, conv_b, w_pb, mem_norm_g,
           w_mk, w_mv, w_px, w_out, final_g):
    bp, seq, _ = x_prompt.shape
    bs, nt, _ = x_sample.shape
    win_len = cache_attn_k.shape[2]
    assert win_len == WINDOW and nt <= CHUNK and seq >= WINDOW

    bias_p = _bias_with_sink(np.stack([_alibi_bias(CHUNK, BAND, m) for m in (WINDOW, CHUNK, 0)]), attn_sink, CHUNK)
    bias_s = _bias_with_sink(_alibi_bias(nt, WINDOW + nt, 0), attn_sink, nt)
    bd_p = jnp.asarray(_block_diag_mask(CHUNK), BF16)
    bd_s = jnp.asarray(_block_diag_mask(nt), BF16)
    fg = final_g.reshape(1, D_MODEL)

    mk, mv, mkb, mvx = _memory_kv(mem_prompt, mem_norm_g, w_mk, w_mv)

    xp = x_prompt
    xs = x_sample.reshape(bs * nt, D_MODEL)
    ck = cache_attn_k.reshape(DEPTH, bs, WINDOW, KV_W)
    cv = cache_attn_v.reshape(DEPTH, bs, WINDOW, KV_W)
    g = norm_g.reshape(DEPTH, 1, D_MODEL)
    cb = conv_b.reshape(DEPTH, 1, CONV_W)
    cmk = cache_mem_k.reshape(DEPTH, bs, N_MEM * X_HEADS, X_HEAD_DIM)
    cmv = cache_mem_v.reshape(DEPTH, bs, N_MEM * X_HEADS, X_HEAD_DIM)
    pk, pv, pc, sk, sv, sc = [], [], [], [], [], []
    for l in range(DEPTH):
        final = l == DEPTH - 1
        layer = jnp.full((1,), l, jnp.int32)
        xp, kl, vl, ul, xs, ko, vo, uo = _layer(
            layer, xp, xs, g, w_in, w_pa, w_pb, w_px, w_out, conv_w, cb, fg, bias_p, bd_p, mkb, mvx,
            bias_s, bd_s, ck, cv, cache_conv, cmk, cmv, nt=nt, final=final)
        pk.append(kl)
        pv.append(vl)
        pc.append(ul)
        sk.append(ko)
        sv.append(vo)
        sc.append(uo)

    kv_p = (DEPTH, bp, WINDOW, N_KV_HEADS, HEAD_DIM)
    kv_s = (DEPTH, bs, nt, N_KV_HEADS, HEAD_DIM)
    mem_shape = (DEPTH, bp, N_MEM, X_HEADS, X_HEAD_DIM)
    return (xp, xs.reshape(bs, nt, D_MODEL),
            jnp.stack(pk).reshape(kv_p), jnp.stack(pv).reshape(kv_p), jnp.stack(pc),
            mk.reshape(mem_shape), mv.reshape(mem_shape),
            jnp.stack(sk).reshape(kv_s), jnp.stack(sv).reshape(kv_s), jnp.stack(sc))
```

```python
import functools

import numpy as np
import jax
import jax.numpy as jnp
from jax import lax
from jax.experimental import pallas as pl
from jax.experimental.pallas import tpu as pltpu

D_MODEL = 1024
DEPTH = 4
CHUNK = 64
WINDOW = 128
N_HEADS = 8
N_KV_HEADS = 2
GROUP = N_HEADS // N_KV_HEADS
HEAD_DIM = 64
ATT_W = N_HEADS * HEAD_DIM
KV_W = N_KV_HEADS * HEAD_DIM
CONV_W = 512
CONV_K = 3
X_HEADS = 4
X_HEAD_DIM = 128
X_W = X_HEADS * X_HEAD_DIM
N_MEM = 256
EPS = 1e-6

OFF_Q = 0
OFF_K = OFF_Q + ATT_W
OFF_V = OFF_K + KV_W
OFF_GA = OFF_V + KV_W
OFF_BB = OFF_GA + ATT_W
OFF_CC = OFF_BB + CONV_W
OFF_HH = OFF_CC + CONV_W
OFF_GB = OFF_HH + CONV_W
OFF_XQ = OFF_GB + CONV_W
OFF_GX = OFF_XQ + X_W
OFF_MG = OFF_GX + X_W
IN_W = OFF_MG + 3 * D_MODEL
W256_OFFS = (OFF_K,)
W512_OFFS = (OFF_Q, OFF_GA, OFF_BB, OFF_CC, OFF_HH, OFF_GB, OFF_XQ, OFF_GX)
W1024_OFFS = (OFF_MG, OFF_MG + D_MODEL, OFF_MG + 2 * D_MODEL)
W_IN_GROUPS = ((256, W256_OFFS), (512, W512_OFFS), (1024, W1024_OFFS))

LANES = 128
GQ_W = GROUP * HEAD_DIM
BAND = WINDOW + CHUNK
KEY_SLOTS = 256
CONV_PAD = 8
PROMPT_TILE = 256
Q_ROWS = 128
CAST_STEPS = 16
VMEM_LIMIT = 60 * 1024 * 1024

F32 = jnp.float32
BF16 = jnp.bfloat16
_NT = (((1,), (1,)), ((), ()))


def _rms_scale(x):
    return x * lax.rsqrt(jnp.mean(x * x, axis=-1, keepdims=True) + EPS)


def _silu(x):
    return x * jax.nn.sigmoid(x)


def _dot(a, b):
    return jnp.dot(a, b, preferred_element_type=F32)


def _w_in(win_refs, off):
    for ref, (_, offs) in zip(win_refs, W_IN_GROUPS):
        if off in offs:
            return ref[offs.index(off)]
    raise ValueError(off)


def _dup_heads(a):
    lane = lax.broadcasted_iota(jnp.int32, a.shape, 1)
    sw = pltpu.roll(a, HEAD_DIM, 1)
    lo = lane < HEAD_DIM
    return jnp.where(lo, a, sw), jnp.where(lo, sw, a)


def _key_tails(nk):
    shape = (KEY_SLOTS - nk, 2 * LANES)
    row = lax.broadcasted_iota(jnp.int32, shape, 0)
    col = lax.broadcasted_iota(jnp.int32, shape, 1)
    vtail = jnp.where((row == 0) & (col >= LANES), 1.0, 0.0).astype(BF16)
    return jnp.zeros(shape, BF16), vtail


def _band_attention(qbd, kb, vb, bias):
    s = lax.dot_general(qbd, kb, _NT, preferred_element_type=F32) + bias
    p = jnp.exp(s - jnp.max(s, axis=-1, keepdims=True)).astype(BF16)
    oe = _dot(p, vb)
    return oe[:, 0:LANES] / oe[:, LANES:2 * LANES]


def _ungroup(on, nq):
    lo = lax.broadcasted_iota(jnp.int32, (nq, LANES), 1) < HEAD_DIM
    return (jnp.where(lo, on[0:nq], on[nq:2 * nq]),
            jnp.where(lo, on[2 * nq:3 * nq], on[3 * nq:4 * nq]))


def _cross_head(xq_h, mk_h, mvx_h):
    s = lax.dot_general(xq_h, mk_h, _NT, preferred_element_type=F32)
    p = jnp.exp(s - jnp.max(s, axis=-1, keepdims=True)).astype(BF16)
    oe = _dot(p, mvx_h)
    return oe[:, 0:X_HEAD_DIM] / oe[:, X_HEAD_DIM:2 * X_HEAD_DIM]


def _memkv_kernel(mem_ref, g_ref, wk_ref, wv_ref, mk_ref, mv_ref, mkb_ref, mvx_ref):
    nb = mem_ref.shape[0]
    mn = (_rms_scale(mem_ref[...].reshape(nb * N_MEM, D_MODEL)) * g_ref[0]).astype(BF16)
    mk = _dot(mn, wk_ref[0].astype(BF16))
    mv = _dot(mn, wv_ref[0].astype(BF16))
    ones = jnp.ones((N_MEM, X_HEAD_DIM), BF16)
    for b in range(nb):
        rows = slice(b * N_MEM, (b + 1) * N_MEM)
        mkb_ref[0, b] = mk[rows].astype(BF16)
        for h in range(X_HEADS):
            cols = slice(h * X_HEAD_DIM, (h + 1) * X_HEAD_DIM)
            mk_ref[0, b, pl.ds(h, N_MEM, stride=X_HEADS), :] = mk[rows, cols]
            mv_ref[0, b, pl.ds(h, N_MEM, stride=X_HEADS), :] = mv[rows, cols]
            mvx_ref[0, b, h, :, 0:X_HEAD_DIM] = mv[rows, cols].astype(BF16)
            mvx_ref[0, b, h, :, X_HEAD_DIM:2 * X_HEAD_DIM] = ones


def _memory_kv(mem, g, wk, wv):
    nb = mem.shape[0]
    return pl.pallas_call(
        _memkv_kernel,
        grid=(DEPTH,),
        in_specs=[
            pl.BlockSpec((nb, N_MEM, D_MODEL), lambda l: (0, 0, 0)),
            pl.BlockSpec((1, 1, D_MODEL), lambda l: (l, 0, 0)),
            pl.BlockSpec((1, D_MODEL, X_W), lambda l: (l, 0, 0)),
            pl.BlockSpec((1, D_MODEL, X_W), lambda l: (l, 0, 0)),
        ],
        out_specs=[
            pl.BlockSpec((1, nb, N_MEM * X_HEADS, X_HEAD_DIM), lambda l: (l, 0, 0, 0)),
            pl.BlockSpec((1, nb, N_MEM * X_HEADS, X_HEAD_DIM), lambda l: (l, 0, 0, 0)),
            pl.BlockSpec((1, nb, N_MEM, X_W), lambda l: (l, 0, 0, 0)),
            pl.BlockSpec((1, nb, X_HEADS, N_MEM, 2 * X_HEAD_DIM), lambda l: (l, 0, 0, 0, 0)),
        ],
        out_shape=[
            jax.ShapeDtypeStruct((DEPTH, nb, N_MEM * X_HEADS, X_HEAD_DIM), F32),
            jax.ShapeDtypeStruct((DEPTH, nb, N_MEM * X_HEADS, X_HEAD_DIM), F32),
            jax.ShapeDtypeStruct((DEPTH, nb, N_MEM, X_W), BF16),
            jax.ShapeDtypeStruct((DEPTH, nb, X_HEADS, N_MEM, 2 * X_HEAD_DIM), BF16),
        ],
        name="memory_kv",
    )(mem, g.reshape(DEPTH, 1, D_MODEL), wk, wv)


def _prompt_tile(t, x_ref, g_ref, win_refs, wpa_ref, wpb_ref, wpx_ref, wout_ref, cw_ref, cb_ref,
                 bias_ref, bd_ref, mk_ref, mvx_ref, fg_ref,
                 xo_ref, kl_ref, vl_ref, ul_ref,
                 kr_ref, vv_ref, uc_ref, q_ref, attn_ref, *, tile, final):
    @pl.when(t == 0)
    def _():
        kr_ref[:, 0:WINDOW, :] = jnp.zeros((N_KV_HEADS, WINDOW, GQ_W), BF16)
        vv_ref[:, 0:WINDOW, 0:LANES] = jnp.zeros((N_KV_HEADS, WINDOW, LANES), BF16)
        vv_ref[:, :, LANES:2 * LANES] = jnp.ones((N_KV_HEADS, WINDOW + tile, LANES), BF16)
        uc_ref[0:CONV_PAD, :] = jnp.zeros((CONV_PAD, CONV_W), F32)

    x = x_ref[0]
    xn = (_rms_scale(x) * g_ref[...]).astype(BF16)

    def proj(off):
        return _dot(xn, _w_in(win_refs, off))

    w_q = _w_in(win_refs, OFF_Q)
    for r in range(0, tile, Q_ROWS):
        q_ref[r:r + Q_ROWS, :] = (_dot(xn[r:r + Q_ROWS], w_q) * (HEAD_DIM ** -0.5)).astype(BF16)
    half = tile // 2
    w_kv = _w_in(win_refs, OFF_K)
    kv = jnp.concatenate([_dot(xn[0:half], w_kv), _dot(xn[half:tile], w_kv)], axis=0)
    k = kv[:, 0:KV_W]
    v = kv[:, KV_W:2 * KV_W]
    ga = proj(OFF_GA)
    kl_ref[0] = k[tile - WINDOW:tile]
    vl_ref[0] = v[tile - WINDOW:tile]
    for j, (kj, vj) in enumerate(zip(_dup_heads(k), _dup_heads(v))):
        kj = kj.astype(BF16)
        kr_ref[j, WINDOW:WINDOW + tile, 0:LANES] = kj
        kr_ref[j, WINDOW:WINDOW + tile, LANES:2 * LANES] = kj
        vv_ref[j, WINDOW:WINDOW + tile, 0:LANES] = vj.astype(BF16)

    fillers = [functools.partial(proj, off) for off in (OFF_BB, OFF_CC, OFF_HH, OFF_GB, OFF_XQ, OFF_GX)]
    fillers += [lambda i=i: jax.nn.sigmoid(proj(OFF_MG + i * D_MODEL)) for i in (1, 2)]
    filled = []
    n_chunks = tile // CHUNK
    ktail, vtail = _key_tails(BAND)

    for c in range(n_chunks):
        r0 = c * CHUNK
        qc = q_ref[r0:r0 + CHUNK, :]
        variant = jnp.where(t == 0, c, 2) if c < 2 else 2
        for j in range(N_KV_HEADS):
            qg = qc[:, j * GQ_W:(j + 1) * GQ_W]
            qbd = jnp.concatenate([qg] * GROUP, axis=0) * bd_ref[...]
            on = _band_attention(qbd, jnp.concatenate([kr_ref[j, r0:r0 + BAND, :], ktail], axis=0),
                                 jnp.concatenate([vv_ref[j, r0:r0 + BAND, :], vtail], axis=0),
                                 bias_ref[variant, j])
            lo, hi = _ungroup(on, CHUNK)
            attn_ref[r0:r0 + CHUNK, j * GQ_W:j * GQ_W + LANES] = lo
            attn_ref[r0:r0 + CHUNK, j * GQ_W + LANES:(j + 1) * GQ_W] = hi
        while len(filled) * n_chunks < (c + 1) * len(fillers):
            filled.append(fillers[len(filled)]())
    bb, cc, hh, gb, xq, gx, gate_b, gate_x = filled
    kr_ref[:, 0:WINDOW, :] = kr_ref[:, tile:tile + WINDOW, :]
    vv_ref[:, 0:WINDOW, 0:LANES] = vv_ref[:, tile:tile + WINDOW, 0:LANES]

    u = cc * hh
    uc_ref[CONV_PAD:CONV_PAD + tile, :] = u
    ul_ref[0] = u[tile - (CONV_K - 1):tile]
    conv = (uc_ref[CONV_PAD - 2:CONV_PAD - 2 + tile, :] * cw_ref[0:1, :]
            + uc_ref[CONV_PAD - 1:CONV_PAD - 1 + tile, :] * cw_ref[1:2, :]
            + u * cw_ref[2:3, :]) + cb_ref[...]
    uc_ref[0:CONV_PAD, :] = uc_ref[tile:tile + CONV_PAD, :]
    zb = (bb * conv * _silu(gb)).astype(BF16)
    za = (attn_ref[...] * _silu(ga)).astype(BF16)

    xq = (xq * (X_HEAD_DIM ** -0.5)).astype(BF16)
    merge_work = [lambda: _dot(za, wpa_ref[...]),
                  lambda: jax.nn.sigmoid(proj(OFF_MG)),
                  lambda: _dot(zb, wpb_ref[...])]
    merged = []
    heads = []
    for h in range(X_HEADS):
        heads.append(_cross_head(xq[:, h * X_HEAD_DIM:(h + 1) * X_HEAD_DIM],
                                 mk_ref[0, :, h * X_HEAD_DIM:(h + 1) * X_HEAD_DIM], mvx_ref[0, h]))
        if h < len(merge_work):
            merged.append(merge_work[h]())
    ya, gate_a, yb = merged
    zx = (jnp.concatenate(heads, axis=1) * _silu(gx)).astype(BF16)
    acc = gate_a * ya + gate_b * yb + gate_x * _dot(zx, wpx_ref[...])
    out = x + _dot(acc.astype(BF16), wout_ref[...])
    if final:
        out = _rms_scale(out) * fg_ref[...]
    xo_ref[0] = out


def _const_spec(shape):
    nd = len(shape)
    return pl.BlockSpec(shape, lambda *_: (0,) * nd, pipeline_mode=pl.Buffered(1))


def _layer_spec(shape):
    nd = len(shape)
    return pl.BlockSpec((None,) + tuple(shape), lambda *a: (a[-1][0],) + (0,) * nd, pipeline_mode=pl.Buffered(1))


def _sample_step(x_ref, g_ref, win_refs, wpa_ref, wpb_ref, wpx_ref, wout_ref, cw_ref, cb_ref,
                 bias_ref, bd_ref, ck_ref, cv_ref, cc_ref, cmk_ref, cmv_ref, fg_ref,
                 xo_ref, ko_ref, vo_ref, uo_ref,
                 uc_ref, attn_ref, conv_ref, xat_ref, *, nb, nt, final):
    x = x_ref[...]
    xn = (_rms_scale(x) * g_ref[...]).astype(BF16)

    def proj(off):
        return _dot(xn, _w_in(win_refs, off))

    q = (proj(OFF_Q) * (HEAD_DIM ** -0.5)).astype(BF16)
    kv = proj(OFF_K)
    k = kv[:, 0:KV_W]
    v = kv[:, KV_W:2 * KV_W]
    ga = proj(OFF_GA)
    ko_ref[...] = k
    vo_ref[...] = v
    ones = jnp.ones((WINDOW + nt, LANES), BF16)
    ktail, vtail = _key_tails(WINDOW + nt)
    fillers = [functools.partial(proj, off) for off in (OFF_BB, OFF_CC, OFF_HH, OFF_GB, OFF_XQ, OFF_GX)]
    filled = []
    for b in range(nb):
        rows = slice(b * nt, (b + 1) * nt)
        kall = _dup_heads(jnp.concatenate([ck_ref[b], k[rows]], axis=0))
        vall = _dup_heads(jnp.concatenate([cv_ref[b], v[rows]], axis=0))
        for j in range(N_KV_HEADS):
            kj = kall[j].astype(BF16)
            vj = vall[j].astype(BF16)
            qg = q[rows, j * GQ_W:(j + 1) * GQ_W]
            qbd = jnp.concatenate([qg] * GROUP, axis=0) * bd_ref[...]
            on = _band_attention(qbd, jnp.concatenate([jnp.concatenate([kj, kj], axis=1), ktail], axis=0),
                                 jnp.concatenate([jnp.concatenate([vj, ones], axis=1), vtail], axis=0),
                                 bias_ref[j])
            lo, hi = _ungroup(on, nt)
            attn_ref[rows, j * GQ_W:j * GQ_W + LANES] = lo
            attn_ref[rows, j * GQ_W + LANES:(j + 1) * GQ_W] = hi
        while len(filled) * nb < (b + 1) * len(fillers):
            filled.append(fillers[len(filled)]())
    bb, cc, hh, gb, p3q, gx = filled
    za = attn_ref[...] * _silu(ga)

    u = cc * hh
    for b in range(nb):
        rows = slice(b * nt, (b + 1) * nt)
        ub = u[rows]
        uc_ref[b, CONV_PAD - (CONV_K - 1):CONV_PAD, :] = cc_ref[b]
        uc_ref[b, CONV_PAD:CONV_PAD + nt, :] = ub
        uo_ref[b] = ub[nt - (CONV_K - 1):nt]
        conv_ref[rows, :] = (uc_ref[b, CONV_PAD - 2:CONV_PAD - 2 + nt, :] * cw_ref[0:1, :]
                             + uc_ref[b, CONV_PAD - 1:CONV_PAD - 1 + nt, :] * cw_ref[1:2, :]
                             + ub * cw_ref[2:3, :]) + cb_ref[...]
    zb = bb * conv_ref[...] * _silu(gb)

    xq = (p3q * (X_HEAD_DIM ** -0.5)).astype(BF16)
    mones = jnp.ones((N_MEM, X_HEAD_DIM), BF16)
    merge_work = [lambda: _dot(za.astype(BF16), wpa_ref[...]),
                  lambda: jax.nn.sigmoid(proj(OFF_MG)),
                  lambda: _dot(zb.astype(BF16), wpb_ref[...]),
                  lambda: jax.nn.sigmoid(proj(OFF_MG + D_MODEL)),
                  lambda: jax.nn.sigmoid(proj(OFF_MG + 2 * D_MODEL))]
    merged = []
    for b in range(nb):
        rows = slice(b * nt, (b + 1) * nt)
        for h in range(X_HEADS):
            cols = slice(h * X_HEAD_DIM, (h + 1) * X_HEAD_DIM)
            head_rows = pl.ds(h, N_MEM, stride=X_HEADS)
            mvx = jnp.concatenate([cmv_ref[b, head_rows, :].astype(BF16), mones], axis=1)
            xat_ref[rows, cols] = _cross_head(xq[rows, cols], cmk_ref[b, head_rows, :].astype(BF16), mvx)
        if b < len(merge_work):
            merged.append(merge_work[b]())
    ya, gate_a, yb, gate_b, gate_x = merged
    zx = xat_ref[...] * _silu(gx)

    acc = gate_a * ya + gate_b * yb + gate_x * _dot(zx.astype(BF16), wpx_ref[...])
    out = x + _dot(acc.astype(BF16), wout_ref[...])
    if final:
        out = _rms_scale(out) * fg_ref[...]
    xo_ref[...] = out


def _layer_kernel(layer_ref, x_ref, g_ref, win32_ref, wpa32_ref, wpb32_ref, wpx32_ref, wout32_ref,
                  cw_ref, cb_ref, fg_ref, bias_ref, bd_ref, mk_ref, mvx_ref,
                  xs_ref, sbias_ref, sbd_ref, ck_ref, cv_ref, cc_ref, cmk_ref, cmv_ref,
                  xo_ref, kl_ref, vl_ref, ul_ref, xso_ref, ko_ref, vo_ref, uo_ref,
                  w256_ref, w512_ref, w1024_ref, wpa_ref, wpb_ref, wpx_ref, wout_ref,
                  kr_ref, vv_ref, uc_ref, q_ref, attn_ref, suc_ref, sattn_ref, sconv_ref, sxat_ref,
                  *, tile, tiles_per_seq, nb, nt, final):
    i = pl.program_id(0)
    win_refs = (w256_ref, w512_ref, w1024_ref)
    weights = (win_refs, wpa_ref, wpb_ref, wpx_ref, wout_ref)

    @pl.when(i < CAST_STEPS)
    def _():
        def chunk_rows(src):
            rows = src.shape[0]
            return pl.ds(pl.multiple_of(i * rows, rows), rows)

        for ref, (width, offs) in zip(win_refs, W_IN_GROUPS):
            for n, off in enumerate(offs):
                ref[n, chunk_rows(win32_ref), :] = win32_ref[:, off:off + width].astype(BF16)
        for src, dst in zip((wpa32_ref, wpb32_ref, wpx32_ref, wout32_ref), weights[1:]):
            dst[chunk_rows(src), :] = src[...].astype(BF16)

    @pl.when(i == CAST_STEPS)
    def _():
        _sample_step(xs_ref, g_ref, *weights, cw_ref, cb_ref,
                     sbias_ref, sbd_ref, ck_ref, cv_ref, cc_ref, cmk_ref, cmv_ref, fg_ref,
                     xso_ref, ko_ref, vo_ref, uo_ref, suc_ref, sattn_ref, sconv_ref, sxat_ref,
                     nb=nb, nt=nt, final=final)

    @pl.when(i > CAST_STEPS)
    def _():
        _prompt_tile(lax.rem(i - (CAST_STEPS + 1), tiles_per_seq), x_ref, g_ref, *weights, cw_ref, cb_ref,
                     bias_ref, bd_ref, mk_ref, mvx_ref, fg_ref,
                     xo_ref, kl_ref, vl_ref, ul_ref, kr_ref, vv_ref, uc_ref, q_ref, attn_ref,
                     tile=tile, final=final)


def _chunk_spec(rows, cols):
    assert rows % CAST_STEPS == 0
    return pl.BlockSpec((None, rows // CAST_STEPS, cols),
                        lambda i, l: (l[0], jnp.minimum(i, CAST_STEPS - 1), 0))


_WEIGHT_SHAPES = ((D_MODEL, IN_W), (ATT_W, D_MODEL), (CONV_W, D_MODEL), (X_W, D_MODEL), (D_MODEL, D_MODEL))
_RESIDENT_SHAPES = tuple((len(offs), D_MODEL, width) for width, offs in W_IN_GROUPS) + _WEIGHT_SHAPES[1:]


def _layer(layer, x, xs, g, win, wpa, wpb, wpx, wout, cw, cb, fg, bias, bd, mkb, mvx,
           sbias, sbd, ck, cv, cc, cmk, cmv, *, nt, final):
    nb, seq, _ = x.shape
    rows = xs.shape[0]
    nbs = rows // nt
    tile = PROMPT_TILE
    assert seq % tile == 0 and tile % WINDOW == 0
    tps = seq // tile
    kern = functools.partial(_layer_kernel, tile=tile, tiles_per_seq=tps, nb=nbs, nt=nt, final=final)

    def prompt_step(i):
        return jnp.maximum(i - (CAST_STEPS + 1), 0)

    def seq_of(i):
        return prompt_step(i) // tps

    def tile_of(i):
        return prompt_step(i) % tps

    grid_spec = pltpu.PrefetchScalarGridSpec(
        num_scalar_prefetch=1,
        grid=(CAST_STEPS + 1 + nb * tps,),
        in_specs=[
            pl.BlockSpec((1, tile, D_MODEL), lambda i, l: (seq_of(i), tile_of(i), 0)),
            _layer_spec((1, D_MODEL)),
            *[_chunk_spec(r, c) for r, c in _WEIGHT_SHAPES],
            _layer_spec((CONV_K, CONV_W)),
            _layer_spec((1, CONV_W)),
            _const_spec((1, D_MODEL)),
            _layer_spec((3, N_KV_HEADS, GROUP * CHUNK, KEY_SLOTS)),
            _const_spec((GROUP * CHUNK, GQ_W)),
            pl.BlockSpec((None, 1, N_MEM, X_W), lambda i, l: (l[0], seq_of(i), 0, 0)),
            pl.BlockSpec((None, 1, X_HEADS, N_MEM, 2 * X_HEAD_DIM), lambda i, l: (l[0], seq_of(i), 0, 0, 0)),
            _const_spec((rows, D_MODEL)),
            _layer_spec((N_KV_HEADS, GROUP * nt, KEY_SLOTS)),
            _const_spec((GROUP * nt, GQ_W)),
            _layer_spec((nbs, WINDOW, KV_W)),
            _layer_spec((nbs, WINDOW, KV_W)),
            _layer_spec((nbs, CONV_K - 1, CONV_W)),
            _layer_spec((nbs, N_MEM * X_HEADS, X_HEAD_DIM)),
            _layer_spec((nbs, N_MEM * X_HEADS, X_HEAD_DIM)),
        ],
        out_specs=[
            pl.BlockSpec((1, tile, D_MODEL), lambda i, l: (seq_of(i), tile_of(i), 0)),
            pl.BlockSpec((1, WINDOW, KV_W), lambda i, l: (seq_of(i), 0, 0)),
            pl.BlockSpec((1, WINDOW, KV_W), lambda i, l: (seq_of(i), 0, 0)),
            pl.BlockSpec((1, CONV_K - 1, CONV_W), lambda i, l: (seq_of(i), 0, 0)),
            pl.BlockSpec((rows, D_MODEL), lambda i, l: (0, 0)),
            pl.BlockSpec((rows, KV_W), lambda i, l: (0, 0)),
            pl.BlockSpec((rows, KV_W), lambda i, l: (0, 0)),
            pl.BlockSpec((nbs, CONV_K - 1, CONV_W), lambda i, l: (0, 0, 0)),
        ],
        scratch_shapes=[
            *[pltpu.VMEM(shape, BF16) for shape in _RESIDENT_SHAPES],
            pltpu.VMEM((N_KV_HEADS, WINDOW + tile, GQ_W), BF16),
            pltpu.VMEM((N_KV_HEADS, WINDOW + tile, 2 * LANES), BF16),
            pltpu.VMEM((CONV_PAD + tile, CONV_W), F32),
            pltpu.VMEM((tile, ATT_W), BF16),
            pltpu.VMEM((tile, ATT_W), F32),
            pltpu.VMEM((nbs, CONV_PAD + nt, CONV_W), F32),
            pltpu.VMEM((rows, ATT_W), F32),
            pltpu.VMEM((rows, CONV_W), F32),
            pltpu.VMEM((rows, X_W), F32),
        ],
    )
    return pl.pallas_call(
        kern,
        grid_spec=grid_spec,
        out_shape=[
            jax.ShapeDtypeStruct((nb, seq, D_MODEL), F32),
            jax.ShapeDtypeStruct((nb, WINDOW, KV_W), F32),
            jax.ShapeDtypeStruct((nb, WINDOW, KV_W), F32),
            jax.ShapeDtypeStruct((nb, CONV_K - 1, CONV_W), F32),
            jax.ShapeDtypeStruct((rows, D_MODEL), F32),
            jax.ShapeDtypeStruct((rows, KV_W), F32),
            jax.ShapeDtypeStruct((rows, KV_W), F32),
            jax.ShapeDtypeStruct((nbs, CONV_K - 1, CONV_W), F32),
        ],
        compiler_params=pltpu.CompilerParams(
            dimension_semantics=("arbitrary",), vmem_limit_bytes=VMEM_LIMIT),
        name="layer",
    )(layer, x, g, win, wpa, wpb, wpx, wout, cw, cb, fg, bias, bd, mkb, mvx,
      xs, sbias, sbd, ck, cv, cc, cmk, cmv)


def _alibi_slopes():
    return np.power(2.0, -8.0 * np.arange(1, N_HEADS + 1, dtype=np.float32) / N_HEADS).astype(np.float32)


def _alibi_bias(nq, nk, masked):
    dist = np.abs(np.arange(nq)[:, None] + WINDOW - np.arange(nk)[None, :]).astype(np.float32)
    slope = _alibi_slopes().reshape(N_KV_HEADS, GROUP)
    bias = -(slope[:, :, None, None] * dist[None, None])
    bias = np.where(np.arange(nk)[None, None, None, :] < masked, -np.inf, bias)
    return bias.reshape(N_KV_HEADS, GROUP * nq, nk).astype(np.float32)


def _bias_with_sink(alibi, attn_sink, nq):
    nk = alibi.shape[-1]
    lead = alibi.shape[:-3]
    sink = jnp.repeat(attn_sink.reshape(DEPTH, N_KV_HEADS, GROUP), nq, axis=2)[..., None]
    sink = jnp.broadcast_to(sink.reshape((DEPTH,) + (1,) * len(lead) + sink.shape[1:]),
                            (DEPTH,) + alibi.shape[:-1] + (1,))
    real = jnp.broadcast_to(jnp.asarray(alibi), (DEPTH,) + alibi.shape)
    unused = jnp.full((DEPTH,) + alibi.shape[:-1] + (KEY_SLOTS - nk - 1,), -jnp.inf, F32)
    return jnp.concatenate([real, sink, unused], axis=-1)


def _block_diag_mask(nq):
    rows = np.arange(GROUP * nq)[:, None] // nq
    cols = np.arange(GQ_W)[None, :] // HEAD_DIM
    return (rows == cols).astype(np.float32)


def kernel(x_prompt, x_sample, cache_attn_k, cache_attn_v, cache_conv, cache_mem_k, cache_mem_v,
           mem_prompt, norm_g, w_in, attn_sink, w_pa, conv_w, conv_b, w_pb, mem_norm_g,
           w_mk, w_mv, w_px, w_out, final_g):
    bp, seq, _ = x_prompt.shape
    bs, nt, _ = x_sample.shape
    win_len = cache_attn_k.shape[2]
    assert win_len == WINDOW and nt <= CHUNK and seq >= WINDOW

    bias_p = _bias_with_sink(np.stack([_alibi_bias(CHUNK, BAND, m) for m in (WINDOW, CHUNK, 0)]), attn_sink, CHUNK)
    bias_s = _bias_with_sink(_alibi_bias(nt, WINDOW + nt, 0), attn_sink, nt)
    bd_p = jnp.asarray(_block_diag_mask(CHUNK), BF16)
    bd_s = jnp.asarray(_block_diag_mask(nt), BF16)
    fg = final_g.reshape(1, D_MODEL)

    mk, mv, mkb, mvx = _memory_kv(mem_prompt, mem_norm_g, w_mk, w_mv)

    xp = x_prompt
    xs = x_sample.reshape(bs * nt, D_MODEL)
    ck = cache_attn_k.reshape(DEPTH, bs, WINDOW, KV_W)
    cv = cache_attn_v.reshape(DEPTH, bs, WINDOW, KV_W)
    g = norm_g.reshape(DEPTH, 1, D_MODEL)
    cb = conv_b.reshape(DEPTH, 1, CONV_W)
    cmk = cache_mem_k.reshape(DEPTH, bs, N_MEM * X_HEADS, X_HEAD_DIM)
    cmv = cache_mem_v.reshape(DEPTH, bs, N_MEM * X_HEADS, X_HEAD_DIM)
    pk, pv, pc, sk, sv, sc = [], [], [], [], [], []
    for l in range(DEPTH):
        final = l == DEPTH - 1
        layer = jnp.full((1,), l, jnp.int32)
        xp, kl, vl, ul, xs, ko, vo, uo = _layer(
            layer, xp, xs, g, w_in, w_pa, w_pb, w_px, w_out, conv_w, cb, fg, bias_p, bd_p, mkb, mvx,
            bias_s, bd_s, ck, cv, cache_conv, cmk, cmv, nt=nt, final=final)
        pk.append(kl)
        pv.append(vl)
        pc.append(ul)
        sk.append(ko)
        sv.append(vo)
        sc.append(uo)

    kv_p = (DEPTH, bp, WINDOW, N_KV_HEADS, HEAD_DIM)
    kv_s = (DEPTH, bs, nt, N_KV_HEADS, HEAD_DIM)
    mem_shape = (DEPTH, bp, N_MEM, X_HEADS, X_HEAD_DIM)
    return (xp, xs.reshape(bs, nt, D_MODEL),
            jnp.stack(pk).reshape(kv_p), jnp.stack(pv).reshape(kv_p), jnp.stack(pc),
            mk.reshape(mem_shape), mv.reshape(mem_shape),
            jnp.stack(sk).reshape(kv_s), jnp.stack(sv).reshape(kv_s), jnp.stack(sc))
```

```python
import functools

import numpy as np
import jax
import jax.numpy as jnp
from jax import lax
from jax.experimental import pallas as pl
from jax.experimental.pallas import tpu as pltpu

D_MODEL = 1024
DEPTH = 4
CHUNK = 64
WINDOW = 128
N_HEADS = 8
N_KV_HEADS = 2
GROUP = N_HEADS // N_KV_HEADS
HEAD_DIM = 64
ATT_W = N_HEADS * HEAD_DIM
KV_W = N_KV_HEADS * HEAD_DIM
CONV_W = 512
CONV_K = 3
X_HEADS = 4
X_HEAD_DIM = 128
X_W = X_HEADS * X_HEAD_DIM
N_MEM = 256
EPS = 1e-6

OFF_Q = 0
OFF_K = OFF_Q + ATT_W
OFF_V = OFF_K + KV_W
OFF_GA = OFF_V + KV_W
OFF_BB = OFF_GA + ATT_W
OFF_CC = OFF_BB + CONV_W
OFF_HH = OFF_CC + CONV_W
OFF_GB = OFF_HH + CONV_W
OFF_XQ = OFF_GB + CONV_W
OFF_GX = OFF_XQ + X_W
OFF_MG = OFF_GX + X_W
IN_W = OFF_MG + 3 * D_MODEL
W256_OFFS = (OFF_K,)
W512_OFFS = (OFF_Q, OFF_GA, OFF_BB, OFF_CC, OFF_HH, OFF_GB, OFF_XQ, OFF_GX)
W1024_OFFS = (OFF_MG, OFF_MG + D_MODEL, OFF_MG + 2 * D_MODEL)
W_IN_GROUPS = ((256, W256_OFFS), (512, W512_OFFS), (1024, W1024_OFFS))

LANES = 128
GQ_W = GROUP * HEAD_DIM
BAND = WINDOW + CHUNK
KEY_SLOTS = 256
CONV_PAD = 8
PROMPT_TILE = 512
DECODE_PARTS = 2
Q_ROWS = 128
CAST_STEPS = 16
VMEM_LIMIT = 60 * 1024 * 1024

F32 = jnp.float32
BF16 = jnp.bfloat16
_NT = (((1,), (1,)), ((), ()))


def _rms_scale(x):
    return x * lax.rsqrt(jnp.mean(x * x, axis=-1, keepdims=True) + EPS)


def _silu(x):
    return x * jax.nn.sigmoid(x)


def _dot(a, b):
    return jnp.dot(a, b, preferred_element_type=F32)


def _w_in(win_refs, off):
    for ref, (_, offs) in zip(win_refs, W_IN_GROUPS):
        if off in offs:
            return ref[offs.index(off)]
    raise ValueError(off)


def _dup_heads(a):
    lane = lax.broadcasted_iota(jnp.int32, a.shape, 1)
    sw = pltpu.roll(a, HEAD_DIM, 1)
    lo = lane < HEAD_DIM
    return jnp.where(lo, a, sw), jnp.where(lo, sw, a)


def _key_tails(nk):
    shape = (KEY_SLOTS - nk, 2 * LANES)
    row = lax.broadcasted_iota(jnp.int32, shape, 0)
    col = lax.broadcasted_iota(jnp.int32, shape, 1)
    vtail = jnp.where((row == 0) & (col >= LANES), 1.0, 0.0).astype(BF16)
    return jnp.zeros(shape, BF16), vtail


def _band_attention(qbd, kb, vb, bias):
    s = lax.dot_general(qbd, kb, _NT, preferred_element_type=F32) + bias
    p = jnp.exp(s - jnp.max(s, axis=-1, keepdims=True)).astype(BF16)
    oe = _dot(p, vb)
    return oe[:, 0:LANES] / oe[:, LANES:2 * LANES]


def _ungroup(on, nq):
    lo = lax.broadcasted_iota(jnp.int32, (nq, LANES), 1) < HEAD_DIM
    return (jnp.where(lo, on[0:nq], on[nq:2 * nq]),
            jnp.where(lo, on[2 * nq:3 * nq], on[3 * nq:4 * nq]))


def _cross_head(xq_h, mk_h, mvx_h):
    s = lax.dot_general(xq_h, mk_h, _NT, preferred_element_type=F32)
    p = jnp.exp(s - jnp.max(s, axis=-1, keepdims=True)).astype(BF16)
    oe = _dot(p, mvx_h)
    return oe[:, 0:X_HEAD_DIM] / oe[:, X_HEAD_DIM:2 * X_HEAD_DIM]


def _memkv_kernel(mem_ref, g_ref, wk_ref, wv_ref, mk_ref, mv_ref, mkb_ref, mvx_ref):
    nb = mem_ref.shape[0]
    mn = (_rms_scale(mem_ref[...].reshape(nb * N_MEM, D_MODEL)) * g_ref[0]).astype(BF16)
    mk = _dot(mn, wk_ref[0].astype(BF16))
    mv = _dot(mn, wv_ref[0].astype(BF16))
    ones = jnp.ones((N_MEM, X_HEAD_DIM), BF16)
    for b in range(nb):
        rows = slice(b * N_MEM, (b + 1) * N_MEM)
        mkb_ref[0, b] = mk[rows].astype(BF16)
        for h in range(X_HEADS):
            cols = slice(h * X_HEAD_DIM, (h + 1) * X_HEAD_DIM)
            mk_ref[0, b, pl.ds(h, N_MEM, stride=X_HEADS), :] = mk[rows, cols]
            mv_ref[0, b, pl.ds(h, N_MEM, stride=X_HEADS), :] = mv[rows, cols]
            mvx_ref[0, b, h, :, 0:X_HEAD_DIM] = mv[rows, cols].astype(BF16)
            mvx_ref[0, b, h, :, X_HEAD_DIM:2 * X_HEAD_DIM] = ones


def _memory_kv(mem, g, wk, wv):
    nb = mem.shape[0]
    return pl.pallas_call(
        _memkv_kernel,
        grid=(DEPTH,),
        in_specs=[
            pl.BlockSpec((nb, N_MEM, D_MODEL), lambda l: (0, 0, 0)),
            pl.BlockSpec((1, 1, D_MODEL), lambda l: (l, 0, 0)),
            pl.BlockSpec((1, D_MODEL, X_W), lambda l: (l, 0, 0)),
            pl.BlockSpec((1, D_MODEL, X_W), lambda l: (l, 0, 0)),
        ],
        out_specs=[
            pl.BlockSpec((1, nb, N_MEM * X_HEADS, X_HEAD_DIM), lambda l: (l, 0, 0, 0)),
            pl.BlockSpec((1, nb, N_MEM * X_HEADS, X_HEAD_DIM), lambda l: (l, 0, 0, 0)),
            pl.BlockSpec((1, nb, N_MEM, X_W), lambda l: (l, 0, 0, 0)),
            pl.BlockSpec((1, nb, X_HEADS, N_MEM, 2 * X_HEAD_DIM), lambda l: (l, 0, 0, 0, 0)),
        ],
        out_shape=[
            jax.ShapeDtypeStruct((DEPTH, nb, N_MEM * X_HEADS, X_HEAD_DIM), F32),
            jax.ShapeDtypeStruct((DEPTH, nb, N_MEM * X_HEADS, X_HEAD_DIM), F32),
            jax.ShapeDtypeStruct((DEPTH, nb, N_MEM, X_W), BF16),
            jax.ShapeDtypeStruct((DEPTH, nb, X_HEADS, N_MEM, 2 * X_HEAD_DIM), BF16),
        ],
        name="memory_kv",
    )(mem, g.reshape(DEPTH, 1, D_MODEL), wk, wv)


def _prompt_tile(t, x_ref, g_ref, win_refs, wpa_ref, wpb_ref, wpx_ref, wout_ref, cw_ref, cb_ref,
                 bias_ref, bd_ref, mk_ref, mvx_ref, fg_ref,
                 xo_ref, kl_ref, vl_ref, ul_ref,
                 kr_ref, vv_ref, uc_ref, q_ref, attn_ref, *, tile, final):
    @pl.when(t == 0)
    def _():
        kr_ref[:, 0:WINDOW, :] = jnp.zeros((N_KV_HEADS, WINDOW, GQ_W), BF16)
        vv_ref[:, 0:WINDOW, 0:LANES] = jnp.zeros((N_KV_HEADS, WINDOW, LANES), BF16)
        vv_ref[:, :, LANES:2 * LANES] = jnp.ones((N_KV_HEADS, WINDOW + tile, LANES), BF16)
        uc_ref[0:CONV_PAD, :] = jnp.zeros((CONV_PAD, CONV_W), F32)

    x = x_ref[0]
    xn = (_rms_scale(x) * g_ref[...]).astype(BF16)

    def proj(off):
        return _dot(xn, _w_in(win_refs, off))

    w_q = _w_in(win_refs, OFF_Q)
    for r in range(0, tile, Q_ROWS):
        q_ref[r:r + Q_ROWS, :] = (_dot(xn[r:r + Q_ROWS], w_q) * (HEAD_DIM ** -0.5)).astype(BF16)
    half = tile // 2
    w_kv = _w_in(win_refs, OFF_K)
    kv = jnp.concatenate([_dot(xn[0:half], w_kv), _dot(xn[half:tile], w_kv)], axis=0)
    k = kv[:, 0:KV_W]
    v = kv[:, KV_W:2 * KV_W]
    ga = proj(OFF_GA)
    kl_ref[0] = k[tile - WINDOW:tile]
    vl_ref[0] = v[tile - WINDOW:tile]
    for j, (kj, vj) in enumerate(zip(_dup_heads(k), _dup_heads(v))):
        kj = kj.astype(BF16)
        kr_ref[j, WINDOW:WINDOW + tile, 0:LANES] = kj
        kr_ref[j, WINDOW:WINDOW + tile, LANES:2 * LANES] = kj
        vv_ref[j, WINDOW:WINDOW + tile, 0:LANES] = vj.astype(BF16)

    fillers = [functools.partial(proj, off) for off in (OFF_BB, OFF_CC, OFF_HH, OFF_GB, OFF_XQ, OFF_GX)]
    fillers += [lambda i=i: jax.nn.sigmoid(proj(OFF_MG + i * D_MODEL)) for i in (1, 2)]
    filled = []
    n_chunks = tile // CHUNK
    ktail, vtail = _key_tails(BAND)

    for c in range(n_chunks):
        r0 = c * CHUNK
        qc = q_ref[r0:r0 + CHUNK, :]
        variant = jnp.where(t == 0, c, 2) if c < 2 else 2
        for j in range(N_KV_HEADS):
            qg = qc[:, j * GQ_W:(j + 1) * GQ_W]
            qbd = jnp.concatenate([qg] * GROUP, axis=0) * bd_ref[...]
            on = _band_attention(qbd, jnp.concatenate([kr_ref[j, r0:r0 + BAND, :], ktail], axis=0),
                                 jnp.concatenate([vv_ref[j, r0:r0 + BAND, :], vtail], axis=0),
                                 bias_ref[variant, j])
            lo, hi = _ungroup(on, CHUNK)
            attn_ref[r0:r0 + CHUNK, j * GQ_W:j * GQ_W + LANES] = lo
            attn_ref[r0:r0 + CHUNK, j * GQ_W + LANES:(j + 1) * GQ_W] = hi
        while len(filled) * n_chunks < (c + 1) * len(fillers):
            filled.append(fillers[len(filled)]())
    bb, cc, hh, gb, xq, gx, gate_b, gate_x = filled
    kr_ref[:, 0:WINDOW, :] = kr_ref[:, tile:tile + WINDOW, :]
    vv_ref[:, 0:WINDOW, 0:LANES] = vv_ref[:, tile:tile + WINDOW, 0:LANES]

    u = cc * hh
    uc_ref[CONV_PAD:CONV_PAD + tile, :] = u
    ul_ref[0] = u[tile - (CONV_K - 1):tile]
    conv = (uc_ref[CONV_PAD - 2:CONV_PAD - 2 + tile, :] * cw_ref[0:1, :]
            + uc_ref[CONV_PAD - 1:CONV_PAD - 1 + tile, :] * cw_ref[1:2, :]
            + u * cw_ref[2:3, :]) + cb_ref[...]
    uc_ref[0:CONV_PAD, :] = uc_ref[tile:tile + CONV_PAD, :]
    zb = (bb * conv * _silu(gb)).astype(BF16)
    za = (attn_ref[...] * _silu(ga)).astype(BF16)

    xq = (xq * (X_HEAD_DIM ** -0.5)).astype(BF16)
    merge_work = [lambda: _dot(za, wpa_ref[...]),
                  lambda: jax.nn.sigmoid(proj(OFF_MG)),
                  lambda: _dot(zb, wpb_ref[...])]
    merged = []
    heads = []
    for h in range(X_HEADS):
        heads.append(_cross_head(xq[:, h * X_HEAD_DIM:(h + 1) * X_HEAD_DIM],
                                 mk_ref[0, :, h * X_HEAD_DIM:(h + 1) * X_HEAD_DIM], mvx_ref[0, h]))
        if h < len(merge_work):
            merged.append(merge_work[h]())
    ya, gate_a, yb = merged
    zx = (jnp.concatenate(heads, axis=1) * _silu(gx)).astype(BF16)
    acc = gate_a * ya + gate_b * yb + gate_x * _dot(zx, wpx_ref[...])
    out = x + _dot(acc.astype(BF16), wout_ref[...])
    if final:
        out = _rms_scale(out) * fg_ref[...]
    xo_ref[0] = out


def _const_spec(shape):
    nd = len(shape)
    return pl.BlockSpec(shape, lambda *_: (0,) * nd, pipeline_mode=pl.Buffered(1))


def _layer_spec(shape):
    nd = len(shape)
    return pl.BlockSpec((None,) + tuple(shape), lambda *a: (a[-1][0],) + (0,) * nd, pipeline_mode=pl.Buffered(1))


def _sample_step(x_ref, g_ref, win_refs, wpa_ref, wpb_ref, wpx_ref, wout_ref, cw_ref, cb_ref,
                 bias_ref, bd_ref, ck_ref, cv_ref, cc_ref, cmk_ref, cmv_ref, fg_ref,
                 xo_ref, ko_ref, vo_ref, uo_ref,
                 uc_ref, attn_ref, conv_ref, xat_ref, *, nb, nt, final):
    nbp = nb // DECODE_PARTS
    rows_p = nbp * nt

    def part(pi, carry):
        b0 = pi * nbp
        part_rows = pl.ds(pl.multiple_of(pi * rows_p, rows_p), rows_p)
        x = x_ref[part_rows, :]
        xn = (_rms_scale(x) * g_ref[...]).astype(BF16)

        def proj(off):
            return _dot(xn, _w_in(win_refs, off))

        q = (proj(OFF_Q) * (HEAD_DIM ** -0.5)).astype(BF16)
        kv = proj(OFF_K)
        k = kv[:, 0:KV_W]
        v = kv[:, KV_W:2 * KV_W]
        ga = proj(OFF_GA)
        ko_ref[part_rows, :] = k
        vo_ref[part_rows, :] = v
        ones = jnp.ones((WINDOW + nt, LANES), BF16)
        ktail, vtail = _key_tails(WINDOW + nt)
        fillers = [functools.partial(proj, off) for off in (OFF_BB, OFF_CC, OFF_HH, OFF_GB, OFF_XQ, OFF_GX)]
        filled = []
        for b in range(nbp):
            rows = slice(b * nt, (b + 1) * nt)
            kall = _dup_heads(jnp.concatenate([ck_ref[b0 + b], k[rows]], axis=0))
            vall = _dup_heads(jnp.concatenate([cv_ref[b0 + b], v[rows]], axis=0))
            for j in range(N_KV_HEADS):
                kj = kall[j].astype(BF16)
                vj = vall[j].astype(BF16)
                qg = q[rows, j * GQ_W:(j + 1) * GQ_W]
                qbd = jnp.concatenate([qg] * GROUP, axis=0) * bd_ref[...]
                on = _band_attention(qbd, jnp.concatenate([jnp.concatenate([kj, kj], axis=1), ktail], axis=0),
                                     jnp.concatenate([jnp.concatenate([vj, ones], axis=1), vtail], axis=0),
                                     bias_ref[j])
                lo, hi = _ungroup(on, nt)
                attn_ref[rows, j * GQ_W:j * GQ_W + LANES] = lo
                attn_ref[rows, j * GQ_W + LANES:(j + 1) * GQ_W] = hi
            while len(filled) * nbp < (b + 1) * len(fillers):
                filled.append(fillers[len(filled)]())
        bb, cc, hh, gb, p3q, gx = filled
        za = attn_ref[...] * _silu(ga)

        u = cc * hh
        for b in range(nbp):
            rows = slice(b * nt, (b + 1) * nt)
            ub = u[rows]
            uc_ref[b, CONV_PAD - (CONV_K - 1):CONV_PAD, :] = cc_ref[b0 + b]
            uc_ref[b, CONV_PAD:CONV_PAD + nt, :] = ub
            uo_ref[b0 + b] = ub[nt - (CONV_K - 1):nt]
            conv_ref[rows, :] = (uc_ref[b, CONV_PAD - 2:CONV_PAD - 2 + nt, :] * cw_ref[0:1, :]
                                 + uc_ref[b, CONV_PAD - 1:CONV_PAD - 1 + nt, :] * cw_ref[1:2, :]
                                 + ub * cw_ref[2:3, :]) + cb_ref[...]
        zb = bb * conv_ref[...] * _silu(gb)

        xq = (p3q * (X_HEAD_DIM ** -0.5)).astype(BF16)
        mones = jnp.ones((N_MEM, X_HEAD_DIM), BF16)
        merge_work = [lambda: _dot(za.astype(BF16), wpa_ref[...]),
                      lambda: jax.nn.sigmoid(proj(OFF_MG)),
                      lambda: _dot(zb.astype(BF16), wpb_ref[...]),
                      lambda: jax.nn.sigmoid(proj(OFF_MG + D_MODEL)),
                      lambda: jax.nn.sigmoid(proj(OFF_MG + 2 * D_MODEL))]
        merged = []
        for b in range(nbp):
            rows = slice(b * nt, (b + 1) * nt)
            for h in range(X_HEADS):
                cols = slice(h * X_HEAD_DIM, (h + 1) * X_HEAD_DIM)
                head_rows = pl.ds(h, N_MEM, stride=X_HEADS)
                mvx = jnp.concatenate([cmv_ref[b0 + b, head_rows, :].astype(BF16), mones], axis=1)
                xat_ref[rows, cols] = _cross_head(xq[rows, cols], cmk_ref[b0 + b, head_rows, :].astype(BF16), mvx)
            while len(merged) * nbp < (b + 1) * len(merge_work):
                merged.append(merge_work[len(merged)]())
        ya, gate_a, yb, gate_b, gate_x = merged
        zx = xat_ref[...] * _silu(gx)

        acc = gate_a * ya + gate_b * yb + gate_x * _dot(zx.astype(BF16), wpx_ref[...])
        out = x + _dot(acc.astype(BF16), wout_ref[...])
        if final:
            out = _rms_scale(out) * fg_ref[...]
        xo_ref[part_rows, :] = out
        return carry

    lax.fori_loop(0, DECODE_PARTS, part, 0)


def _layer_kernel(layer_ref, x_ref, g_ref, win32_ref, wpa32_ref, wpb32_ref, wpx32_ref, wout32_ref,
                  cw_ref, cb_ref, fg_ref, bias_ref, bd_ref, mk_ref, mvx_ref,
                  xs_ref, sbias_ref, sbd_ref, ck_ref, cv_ref, cc_ref, cmk_ref, cmv_ref,
                  xo_ref, kl_ref, vl_ref, ul_ref, xso_ref, ko_ref, vo_ref, uo_ref,
                  w256_ref, w512_ref, w1024_ref, wpa_ref, wpb_ref, wpx_ref, wout_ref,
                  kr_ref, vv_ref, uc_ref, q_ref, attn_ref, suc_ref, sattn_ref, sconv_ref, sxat_ref,
                  *, tile, tiles_per_seq, nb, nt, final):
    i = pl.program_id(0)
    win_refs = (w256_ref, w512_ref, w1024_ref)
    weights = (win_refs, wpa_ref, wpb_ref, wpx_ref, wout_ref)

    @pl.when(i < CAST_STEPS)
    def _():
        def chunk_rows(src):
            rows = src.shape[0]
            return pl.ds(pl.multiple_of(i * rows, rows), rows)

        for ref, (width, offs) in zip(win_refs, W_IN_GROUPS):
            for n, off in enumerate(offs):
                ref[n, chunk_rows(win32_ref), :] = win32_ref[:, off:off + width].astype(BF16)
        for src, dst in zip((wpa32_ref, wpb32_ref, wpx32_ref, wout32_ref), weights[1:]):
            dst[chunk_rows(src), :] = src[...].astype(BF16)

    @pl.when(i == CAST_STEPS)
    def _():
        _sample_step(xs_ref, g_ref, *weights, cw_ref, cb_ref,
                     sbias_ref, sbd_ref, ck_ref, cv_ref, cc_ref, cmk_ref, cmv_ref, fg_ref,
                     xso_ref, ko_ref, vo_ref, uo_ref, suc_ref, sattn_ref, sconv_ref, sxat_ref,
                     nb=nb, nt=nt, final=final)

    @pl.when(i > CAST_STEPS)
    def _():
        _prompt_tile(lax.rem(i - (CAST_STEPS + 1), tiles_per_seq), x_ref, g_ref, *weights, cw_ref, cb_ref,
                     bias_ref, bd_ref, mk_ref, mvx_ref, fg_ref,
                     xo_ref, kl_ref, vl_ref, ul_ref, kr_ref, vv_ref, uc_ref, q_ref, attn_ref,
                     tile=tile, final=final)


def _chunk_spec(rows, cols):
    assert rows % CAST_STEPS == 0
    return pl.BlockSpec((None, rows // CAST_STEPS, cols),
                        lambda i, l: (l[0], jnp.minimum(i, CAST_STEPS - 1), 0))


_WEIGHT_SHAPES = ((D_MODEL, IN_W), (ATT_W, D_MODEL), (CONV_W, D_MODEL), (X_W, D_MODEL), (D_MODEL, D_MODEL))
_RESIDENT_SHAPES = tuple((len(offs), D_MODEL, width) for width, offs in W_IN_GROUPS) + _WEIGHT_SHAPES[1:]


def _layer(layer, x, xs, g, win, wpa, wpb, wpx, wout, cw, cb, fg, bias, bd, mkb, mvx,
           sbias, sbd, ck, cv, cc, cmk, cmv, *, nt, final):
    nb, seq, _ = x.shape
    rows = xs.shape[0]
    nbs = rows // nt
    tile = PROMPT_TILE
    assert seq % tile == 0 and tile % WINDOW == 0
    tps = seq // tile
    kern = functools.partial(_layer_kernel, tile=tile, tiles_per_seq=tps, nb=nbs, nt=nt, final=final)

    def prompt_step(i):
        return jnp.maximum(i - (CAST_STEPS + 1), 0)

    def seq_of(i):
        return prompt_step(i) // tps

    def tile_of(i):
        return prompt_step(i) % tps

    grid_spec = pltpu.PrefetchScalarGridSpec(
        num_scalar_prefetch=1,
        grid=(CAST_STEPS + 1 + nb * tps,),
        in_specs=[
            pl.BlockSpec((1, tile, D_MODEL), lambda i, l: (seq_of(i), tile_of(i), 0)),
            _layer_spec((1, D_MODEL)),
            *[_chunk_spec(r, c) for r, c in _WEIGHT_SHAPES],
            _layer_spec((CONV_K, CONV_W)),
            _layer_spec((1, CONV_W)),
            _const_spec((1, D_MODEL)),
            _layer_spec((3, N_KV_HEADS, GROUP * CHUNK, KEY_SLOTS)),
            _const_spec((GROUP * CHUNK, GQ_W)),
            pl.BlockSpec((None, 1, N_MEM, X_W), lambda i, l: (l[0], seq_of(i), 0, 0)),
            pl.BlockSpec((None, 1, X_HEADS, N_MEM, 2 * X_HEAD_DIM), lambda i, l: (l[0], seq_of(i), 0, 0, 0)),
            _const_spec((rows, D_MODEL)),
            _layer_spec((N_KV_HEADS, GROUP * nt, KEY_SLOTS)),
            _const_spec((GROUP * nt, GQ_W)),
            _layer_spec((nbs, WINDOW, KV_W)),
            _layer_spec((nbs, WINDOW, KV_W)),
            _layer_spec((nbs, CONV_K - 1, CONV_W)),
            _layer_spec((nbs, N_MEM * X_HEADS, X_HEAD_DIM)),
            _layer_spec((nbs, N_MEM * X_HEADS, X_HEAD_DIM)),
        ],
        out_specs=[
            pl.BlockSpec((1, tile, D_MODEL), lambda i, l: (seq_of(i), tile_of(i), 0)),
            pl.BlockSpec((1, WINDOW, KV_W), lambda i, l: (seq_of(i), 0, 0)),
            pl.BlockSpec((1, WINDOW, KV_W), lambda i, l: (seq_of(i), 0, 0)),
            pl.BlockSpec((1, CONV_K - 1, CONV_W), lambda i, l: (seq_of(i), 0, 0)),
            pl.BlockSpec((rows, D_MODEL), lambda i, l: (0, 0)),
            pl.BlockSpec((rows, KV_W), lambda i, l: (0, 0)),
            pl.BlockSpec((rows, KV_W), lambda i, l: (0, 0)),
            pl.BlockSpec((nbs, CONV_K - 1, CONV_W), lambda i, l: (0, 0, 0)),
        ],
        scratch_shapes=[
            *[pltpu.VMEM(shape, BF16) for shape in _RESIDENT_SHAPES],
            pltpu.VMEM((N_KV_HEADS, WINDOW + tile, GQ_W), BF16),
            pltpu.VMEM((N_KV_HEADS, WINDOW + tile, 2 * LANES), BF16),
            pltpu.VMEM((CONV_PAD + tile, CONV_W), F32),
            pltpu.VMEM((tile, ATT_W), BF16),
            pltpu.VMEM((tile, ATT_W), F32),
            pltpu.VMEM((nbs // DECODE_PARTS, CONV_PAD + nt, CONV_W), F32),
            pltpu.VMEM((rows // DECODE_PARTS, ATT_W), F32),
            pltpu.VMEM((rows // DECODE_PARTS, CONV_W), F32),
            pltpu.VMEM((rows // DECODE_PARTS, X_W), F32),
        ],
    )
    return pl.pallas_call(
        kern,
        grid_spec=grid_spec,
        out_shape=[
            jax.ShapeDtypeStruct((nb, seq, D_MODEL), F32),
            jax.ShapeDtypeStruct((nb, WINDOW, KV_W), F32),
            jax.ShapeDtypeStruct((nb, WINDOW, KV_W), F32),
            jax.ShapeDtypeStruct((nb, CONV_K - 1, CONV_W), F32),
            jax.ShapeDtypeStruct((rows, D_MODEL), F32),
            jax.ShapeDtypeStruct((rows, KV_W), F32),
            jax.ShapeDtypeStruct((rows, KV_W), F32),
            jax.ShapeDtypeStruct((nbs, CONV_K - 1, CONV_W), F32),
        ],
        compiler_params=pltpu.CompilerParams(
            dimension_semantics=("arbitrary",), vmem_limit_bytes=VMEM_LIMIT),
        name="layer",
    )(layer, x, g, win, wpa, wpb, wpx, wout, cw, cb, fg, bias, bd, mkb, mvx,
      xs, sbias, sbd, ck, cv, cc, cmk, cmv)


def _alibi_slopes():
    return np.power(2.0, -8.0 * np.arange(1, N_HEADS + 1, dtype=np.float32) / N_HEADS).astype(np.float32)


def _alibi_bias(nq, nk, masked):
    dist = np.abs(np.arange(nq)[:, None] + WINDOW - np.arange(nk)[None, :]).astype(np.float32)
    slope = _alibi_slopes().reshape(N_KV_HEADS, GROUP)
    bias = -(slope[:, :, None, None] * dist[None, None])
    bias = np.where(np.arange(nk)[None, None, None, :] < masked, -np.inf, bias)
    return bias.reshape(N_KV_HEADS, GROUP * nq, nk).astype(np.float32)


def _bias_with_sink(alibi, attn_sink, nq):
    nk = alibi.shape[-1]
    lead = alibi.shape[:-3]
    sink = jnp.repeat(attn_sink.reshape(DEPTH, N_KV_HEADS, GROUP), nq, axis=2)[..., None]
    sink = jnp.broadcast_to(sink.reshape((DEPTH,) + (1,) * len(lead) + sink.shape[1:]),
                            (DEPTH,) + alibi.shape[:-1] + (1,))
    real = jnp.broadcast_to(jnp.asarray(alibi), (DEPTH,) + alibi.shape)
    unused = jnp.full((DEPTH,) + alibi.shape[:-1] + (KEY_SLOTS - nk - 1,), -jnp.inf, F32)
    return jnp.concatenate([real, sink, unused], axis=-1)


def _block_diag_mask(nq):
    rows = np.arange(GROUP * nq)[:, None] // nq
    cols = np.arange(GQ_W)[None, :] // HEAD_DIM
    return (rows == cols).astype(np.float32)


def kernel(x_prompt, x_sample, cache_attn_k, cache_attn_v, cache_conv, cache_mem_k, cache_mem_v,
           mem_prompt, norm_g, w_in, attn_sink, w_pa, conv_w, conv_b, w_pb, mem_norm_g,
           w_mk, w_mv, w_px, w_out, final_g):
    bp, seq, _ = x_prompt.shape
    bs, nt, _ = x_sample.shape
    win_len = cache_attn_k.shape[2]
    assert win_len == WINDOW and nt <= CHUNK and seq >= WINDOW

    bias_p = _bias_with_sink(np.stack([_alibi_bias(CHUNK, BAND, m) for m in (WINDOW, CHUNK, 0)]), attn_sink, CHUNK)
    bias_s = _bias_with_sink(_alibi_bias(nt, WINDOW + nt, 0), attn_sink, nt)
    bd_p = jnp.asarray(_block_diag_mask(CHUNK), BF16)
    bd_s = jnp.asarray(_block_diag_mask(nt), BF16)
    fg = final_g.reshape(1, D_MODEL)

    mk, mv, mkb, mvx = _memory_kv(mem_prompt, mem_norm_g, w_mk, w_mv)

    xp = x_prompt
    xs = x_sample.reshape(bs * nt, D_MODEL)
    ck = cache_attn_k.reshape(DEPTH, bs, WINDOW, KV_W)
    cv = cache_attn_v.reshape(DEPTH, bs, WINDOW, KV_W)
    g = norm_g.reshape(DEPTH, 1, D_MODEL)
    cb = conv_b.reshape(DEPTH, 1, CONV_W)
    cmk = cache_mem_k.reshape(DEPTH, bs, N_MEM * X_HEADS, X_HEAD_DIM)
    cmv = cache_mem_v.reshape(DEPTH, bs, N_MEM * X_HEADS, X_HEAD_DIM)
    pk, pv, pc, sk, sv, sc = [], [], [], [], [], []
    for l in range(DEPTH):
        final = l == DEPTH - 1
        layer = jnp.full((1,), l, jnp.int32)
        xp, kl, vl, ul, xs, ko, vo, uo = _layer(
            layer, xp, xs, g, w_in, w_pa, w_pb, w_px, w_out, conv_w, cb, fg, bias_p, bd_p, mkb, mvx,
            bias_s, bd_s, ck, cv, cache_conv, cmk, cmv, nt=nt, final=final)
        pk.append(kl)
        pv.append(vl)
        pc.append(ul)
        sk.append(ko)
        sv.append(vo)
        sc.append(uo)

    kv_p = (DEPTH, bp, WINDOW, N_KV_HEADS, HEAD_DIM)
    kv_s = (DEPTH, bs, nt, N_KV_HEADS, HEAD_DIM)
    mem_shape = (DEPTH, bp, N_MEM, X_HEADS, X_HEAD_DIM)
    return (xp, xs.reshape(bs, nt, D_MODEL),
            jnp.stack(pk).reshape(kv_p), jnp.stack(pv).reshape(kv_p), jnp.stack(pc),
            mk.reshape(mem_shape), mv.reshape(mem_shape),
            jnp.stack(sk).reshape(kv_s), jnp.stack(sv).reshape(kv_s), jnp.stack(sc))
```

```python
import functools

import numpy as np
import jax
import jax.numpy as jnp
from jax import lax
from jax.experimental import pallas as pl
from jax.experimental.pallas import tpu as pltpu

D_MODEL = 1024
DEPTH = 4
CHUNK = 64
WINDOW = 128
N_HEADS = 8
N_KV_HEADS = 2
GROUP = N_HEADS // N_KV_HEADS
HEAD_DIM = 64
ATT_W = N_HEADS * HEAD_DIM
KV_W = N_KV_HEADS * HEAD_DIM
CONV_W = 512
CONV_K = 3
X_HEADS = 4
X_HEAD_DIM = 128
X_W = X_HEADS * X_HEAD_DIM
N_MEM = 256
EPS = 1e-6

OFF_Q = 0
OFF_K = OFF_Q + ATT_W
OFF_V = OFF_K + KV_W
OFF_GA = OFF_V + KV_W
OFF_BB = OFF_GA + ATT_W
OFF_CC = OFF_BB + CONV_W
OFF_HH = OFF_CC + CONV_W
OFF_GB = OFF_HH + CONV_W
OFF_XQ = OFF_GB + CONV_W
OFF_GX = OFF_XQ + X_W
OFF_MG = OFF_GX + X_W
IN_W = OFF_MG + 3 * D_MODEL
W_IN_WIDTH = {OFF_Q: ATT_W, OFF_K: 2 * KV_W, OFF_GA: ATT_W, OFF_BB: CONV_W, OFF_CC: CONV_W, OFF_HH: CONV_W,
              OFF_GB: CONV_W, OFF_XQ: X_W, OFF_GX: X_W,
              OFF_MG: D_MODEL, OFF_MG + D_MODEL: D_MODEL, OFF_MG + 2 * D_MODEL: D_MODEL}

LANES = 128
GQ_W = GROUP * HEAD_DIM
BAND = WINDOW + CHUNK
KEY_SLOTS = 256
CONV_PAD = 8
PROMPT_TILE = 512
DECODE_PARTS = 2
Q_ROWS = 128
CAST_STEPS = 16
CAST_COLS = 256
VMEM_LIMIT = 60 * 1024 * 1024

F32 = jnp.float32
BF16 = jnp.bfloat16
_NT = (((1,), (1,)), ((), ()))


def _rms_scale(x):
    return x * lax.rsqrt(jnp.mean(x * x, axis=-1, keepdims=True) + EPS)


def _silu(x):
    return x * jax.nn.sigmoid(x)


def _dot(a, b):
    return jnp.dot(a, b, preferred_element_type=F32)


def _w_in(win_ref, off):
    return win_ref[:, off:off + W_IN_WIDTH[off]]


def _dup_heads(a):
    lane = lax.broadcasted_iota(jnp.int32, a.shape, 1)
    sw = pltpu.roll(a, HEAD_DIM, 1)
    lo = lane < HEAD_DIM
    return jnp.where(lo, a, sw), jnp.where(lo, sw, a)


def _key_tails(nk):
    shape = (KEY_SLOTS - nk, 2 * LANES)
    row = lax.broadcasted_iota(jnp.int32, shape, 0)
    col = lax.broadcasted_iota(jnp.int32, shape, 1)
    vtail = jnp.where((row == 0) & (col >= LANES), 1.0, 0.0).astype(BF16)
    return jnp.zeros(shape, BF16), vtail


def _band_attention(qbd, kb, vb, bias):
    s = lax.dot_general(qbd, kb, _NT, preferred_element_type=F32) + bias
    p = jnp.exp(s - jnp.max(s, axis=-1, keepdims=True)).astype(BF16)
    oe = _dot(p, vb)
    return oe[:, 0:LANES] / oe[:, LANES:2 * LANES]


def _ungroup(on, nq):
    lo = lax.broadcasted_iota(jnp.int32, (nq, LANES), 1) < HEAD_DIM
    return (jnp.where(lo, on[0:nq], on[nq:2 * nq]),
            jnp.where(lo, on[2 * nq:3 * nq], on[3 * nq:4 * nq]))


def _cross_head(xq_h, mk_h, mvx_h):
    s = lax.dot_general(xq_h, mk_h, _NT, preferred_element_type=F32)
    p = jnp.exp(s - jnp.max(s, axis=-1, keepdims=True)).astype(BF16)
    oe = _dot(p, mvx_h)
    return oe[:, 0:X_HEAD_DIM] / oe[:, X_HEAD_DIM:2 * X_HEAD_DIM]


def _memkv_kernel(mem_ref, g_ref, wk_ref, wv_ref, mk_ref, mv_ref, mkb_ref, mvx_ref):
    nb = mem_ref.shape[0]
    mn = (_rms_scale(mem_ref[...].reshape(nb * N_MEM, D_MODEL)) * g_ref[0]).astype(BF16)
    mk = _dot(mn, wk_ref[0].astype(BF16))
    mv = _dot(mn, wv_ref[0].astype(BF16))
    ones = jnp.ones((N_MEM, X_HEAD_DIM), BF16)
    for b in range(nb):
        rows = slice(b * N_MEM, (b + 1) * N_MEM)
        mkb_ref[0, b] = mk[rows].astype(BF16)
        for h in range(X_HEADS):
            cols = slice(h * X_HEAD_DIM, (h + 1) * X_HEAD_DIM)
            mk_ref[0, b, pl.ds(h, N_MEM, stride=X_HEADS), :] = mk[rows, cols]
            mv_ref[0, b, pl.ds(h, N_MEM, stride=X_HEADS), :] = mv[rows, cols]
            mvx_ref[0, b, h, :, 0:X_HEAD_DIM] = mv[rows, cols].astype(BF16)
            mvx_ref[0, b, h, :, X_HEAD_DIM:2 * X_HEAD_DIM] = ones


def _memory_kv(mem, g, wk, wv):
    nb = mem.shape[0]
    return pl.pallas_call(
        _memkv_kernel,
        grid=(DEPTH,),
        in_specs=[
            pl.BlockSpec((nb, N_MEM, D_MODEL), lambda l: (0, 0, 0)),
            pl.BlockSpec((1, 1, D_MODEL), lambda l: (l, 0, 0)),
            pl.BlockSpec((1, D_MODEL, X_W), lambda l: (l, 0, 0)),
            pl.BlockSpec((1, D_MODEL, X_W), lambda l: (l, 0, 0)),
        ],
        out_specs=[
            pl.BlockSpec((1, nb, N_MEM * X_HEADS, X_HEAD_DIM), lambda l: (l, 0, 0, 0)),
            pl.BlockSpec((1, nb, N_MEM * X_HEADS, X_HEAD_DIM), lambda l: (l, 0, 0, 0)),
            pl.BlockSpec((1, nb, N_MEM, X_W), lambda l: (l, 0, 0, 0)),
            pl.BlockSpec((1, nb, X_HEADS, N_MEM, 2 * X_HEAD_DIM), lambda l: (l, 0, 0, 0, 0)),
        ],
        out_shape=[
            jax.ShapeDtypeStruct((DEPTH, nb, N_MEM * X_HEADS, X_HEAD_DIM), F32),
            jax.ShapeDtypeStruct((DEPTH, nb, N_MEM * X_HEADS, X_HEAD_DIM), F32),
            jax.ShapeDtypeStruct((DEPTH, nb, N_MEM, X_W), BF16),
            jax.ShapeDtypeStruct((DEPTH, nb, X_HEADS, N_MEM, 2 * X_HEAD_DIM), BF16),
        ],
        name="memory_kv",
    )(mem, g.reshape(DEPTH, 1, D_MODEL), wk, wv)


def _prompt_tile(t, x_ref, g_ref, win_ref, wpa_ref, wpb_ref, wpx_ref, wout_ref, cw_ref, cb_ref,
                 bias_ref, bd_ref, mk_ref, mvx_ref, fg_ref,
                 xo_ref, kl_ref, vl_ref, ul_ref,
                 kr_ref, vv_ref, uc_ref, q_ref, attn_ref, *, tile, final):
    @pl.when(t == 0)
    def _():
        kr_ref[:, 0:WINDOW, :] = jnp.zeros((N_KV_HEADS, WINDOW, GQ_W), BF16)
        vv_ref[:, 0:WINDOW, 0:LANES] = jnp.zeros((N_KV_HEADS, WINDOW, LANES), BF16)
        vv_ref[:, :, LANES:2 * LANES] = jnp.ones((N_KV_HEADS, WINDOW + tile, LANES), BF16)
        uc_ref[0:CONV_PAD, :] = jnp.zeros((CONV_PAD, CONV_W), F32)

    x = x_ref[0]
    xn = (_rms_scale(x) * g_ref[...]).astype(BF16)

    def proj(off):
        return _dot(xn, _w_in(win_ref, off))

    w_q = _w_in(win_ref, OFF_Q)
    for r in range(0, tile, Q_ROWS):
        q_ref[r:r + Q_ROWS, :] = (_dot(xn[r:r + Q_ROWS], w_q) * (HEAD_DIM ** -0.5)).astype(BF16)
    half = tile // 2
    w_kv = _w_in(win_ref, OFF_K)
    kv = jnp.concatenate([_dot(xn[0:half], w_kv), _dot(xn[half:tile], w_kv)], axis=0)
    k = kv[:, 0:KV_W]
    v = kv[:, KV_W:2 * KV_W]
    ga = proj(OFF_GA)
    kl_ref[0] = k[tile - WINDOW:tile]
    vl_ref[0] = v[tile - WINDOW:tile]
    for j, (kj, vj) in enumerate(zip(_dup_heads(k), _dup_heads(v))):
        kj = kj.astype(BF16)
        kr_ref[j, WINDOW:WINDOW + tile, 0:LANES] = kj
        kr_ref[j, WINDOW:WINDOW + tile, LANES:2 * LANES] = kj
        vv_ref[j, WINDOW:WINDOW + tile, 0:LANES] = vj.astype(BF16)

    fillers = [functools.partial(proj, off) for off in (OFF_BB, OFF_CC, OFF_HH, OFF_GB, OFF_XQ, OFF_GX)]
    fillers += [lambda i=i: jax.nn.sigmoid(proj(OFF_MG + i * D_MODEL)) for i in (1, 2)]
    filled = []
    n_chunks = tile // CHUNK
    ktail, vtail = _key_tails(BAND)

    for c in range(n_chunks):
        r0 = c * CHUNK
        qc = q_ref[r0:r0 + CHUNK, :]
        variant = jnp.where(t == 0, c, 2) if c < 2 else 2
        for j in range(N_KV_HEADS):
            qg = qc[:, j * GQ_W:(j + 1) * GQ_W]
            qbd = jnp.concatenate([qg] * GROUP, axis=0) * bd_ref[...]
            on = _band_attention(qbd, jnp.concatenate([kr_ref[j, r0:r0 + BAND, :], ktail], axis=0),
                                 jnp.concatenate([vv_ref[j, r0:r0 + BAND, :], vtail], axis=0),
                                 bias_ref[variant, j])
            lo, hi = _ungroup(on, CHUNK)
            attn_ref[r0:r0 + CHUNK, j * GQ_W:j * GQ_W + LANES] = lo
            attn_ref[r0:r0 + CHUNK, j * GQ_W + LANES:(j + 1) * GQ_W] = hi
        while len(filled) * n_chunks < (c + 1) * len(fillers):
            filled.append(fillers[len(filled)]())
    bb, cc, hh, gb, xq, gx, gate_b, gate_x = filled
    kr_ref[:, 0:WINDOW, :] = kr_ref[:, tile:tile + WINDOW, :]
    vv_ref[:, 0:WINDOW, 0:LANES] = vv_ref[:, tile:tile + WINDOW, 0:LANES]

    u = cc * hh
    uc_ref[CONV_PAD:CONV_PAD + tile, :] = u
    ul_ref[0] = u[tile - (CONV_K - 1):tile]
    conv = (uc_ref[CONV_PAD - 2:CONV_PAD - 2 + tile, :] * cw_ref[0:1, :]
            + uc_ref[CONV_PAD - 1:CONV_PAD - 1 + tile, :] * cw_ref[1:2, :]
            + u * cw_ref[2:3, :]) + cb_ref[...]
    uc_ref[0:CONV_PAD, :] = uc_ref[tile:tile + CONV_PAD, :]
    zb = (bb * conv * _silu(gb)).astype(BF16)
    za = (attn_ref[...] * _silu(ga)).astype(BF16)

    xq = (xq * (X_HEAD_DIM ** -0.5)).astype(BF16)
    merge_work = [lambda: _dot(za, wpa_ref[...]),
                  lambda: jax.nn.sigmoid(proj(OFF_MG)),
                  lambda: _dot(zb, wpb_ref[...])]
    merged = []
    heads = []
    for h in range(X_HEADS):
        heads.append(_cross_head(xq[:, h * X_HEAD_DIM:(h + 1) * X_HEAD_DIM],
                                 mk_ref[0, :, h * X_HEAD_DIM:(h + 1) * X_HEAD_DIM], mvx_ref[0, h]))
        if h < len(merge_work):
            merged.append(merge_work[h]())
    ya, gate_a, yb = merged
    zx = (jnp.concatenate(heads, axis=1) * _silu(gx)).astype(BF16)
    acc = gate_a * ya + gate_b * yb + gate_x * _dot(zx, wpx_ref[...])
    out = x + _dot(acc.astype(BF16), wout_ref[...])
    if final:
        out = _rms_scale(out) * fg_ref[...]
    xo_ref[0] = out


def _const_spec(shape):
    nd = len(shape)
    return pl.BlockSpec(shape, lambda *_: (0,) * nd, pipeline_mode=pl.Buffered(1))


def _layer_spec(shape):
    nd = len(shape)
    return pl.BlockSpec((None,) + tuple(shape), lambda *a: (a[-1][0],) + (0,) * nd, pipeline_mode=pl.Buffered(1))


def _sample_step(x_ref, g_ref, win_ref, wpa_ref, wpb_ref, wpx_ref, wout_ref, cw_ref, cb_ref,
                 bias_ref, bd_ref, ck_ref, cv_ref, cc_ref, cmk_ref, cmv_ref, fg_ref,
                 xo_ref, ko_ref, vo_ref, uo_ref,
                 uc_ref, attn_ref, conv_ref, xat_ref, *, nb, nt, final):
    nbp = nb // DECODE_PARTS
    rows_p = nbp * nt

    def part(pi, carry):
        b0 = pi * nbp
        part_rows = pl.ds(pl.multiple_of(pi * rows_p, rows_p), rows_p)
        x = x_ref[part_rows, :]
        xn = (_rms_scale(x) * g_ref[...]).astype(BF16)

        def proj(off):
            return _dot(xn, _w_in(win_ref, off))

        q = (proj(OFF_Q) * (HEAD_DIM ** -0.5)).astype(BF16)
        kv = proj(OFF_K)
        k = kv[:, 0:KV_W]
        v = kv[:, KV_W:2 * KV_W]
        ga = proj(OFF_GA)
        ko_ref[part_rows, :] = k
        vo_ref[part_rows, :] = v
        ones = jnp.ones((WINDOW + nt, LANES), BF16)
        ktail, vtail = _key_tails(WINDOW + nt)
        fillers = [functools.partial(proj, off) for off in (OFF_BB, OFF_CC, OFF_HH, OFF_GB, OFF_XQ, OFF_GX)]
        filled = []
        for b in range(nbp):
            rows = slice(b * nt, (b + 1) * nt)
            kall = _dup_heads(jnp.concatenate([ck_ref[b0 + b], k[rows]], axis=0))
            vall = _dup_heads(jnp.concatenate([cv_ref[b0 + b], v[rows]], axis=0))
            for j in range(N_KV_HEADS):
                kj = kall[j].astype(BF16)
                vj = vall[j].astype(BF16)
                qg = q[rows, j * GQ_W:(j + 1) * GQ_W]
                qbd = jnp.concatenate([qg] * GROUP, axis=0) * bd_ref[...]
                on = _band_attention(qbd, jnp.concatenate([jnp.concatenate([kj, kj], axis=1), ktail], axis=0),
                                     jnp.concatenate([jnp.concatenate([vj, ones], axis=1), vtail], axis=0),
                                     bias_ref[j])
                lo, hi = _ungroup(on, nt)
                attn_ref[rows, j * GQ_W:j * GQ_W + LANES] = lo
                attn_ref[rows, j * GQ_W + LANES:(j + 1) * GQ_W] = hi
            while len(filled) * nbp < (b + 1) * len(fillers):
                filled.append(fillers[len(filled)]())
        bb, cc, hh, gb, p3q, gx = filled
        za = attn_ref[...] * _silu(ga)

        u = cc * hh
        for b in range(nbp):
            rows = slice(b * nt, (b + 1) * nt)
            ub = u[rows]
            uc_ref[b, CONV_PAD - (CONV_K - 1):CONV_PAD, :] = cc_ref[b0 + b]
            uc_ref[b, CONV_PAD:CONV_PAD + nt, :] = ub
            uo_ref[b0 + b] = ub[nt - (CONV_K - 1):nt]
            conv_ref[rows, :] = (uc_ref[b, CONV_PAD - 2:CONV_PAD - 2 + nt, :] * cw_ref[0:1, :]
                                 + uc_ref[b, CONV_PAD - 1:CONV_PAD - 1 + nt, :] * cw_ref[1:2, :]
                                 + ub * cw_ref[2:3, :]) + cb_ref[...]
        zb = bb * conv_ref[...] * _silu(gb)

        xq = (p3q * (X_HEAD_DIM ** -0.5)).astype(BF16)
        mones = jnp.ones((N_MEM, X_HEAD_DIM), BF16)
        merge_work = [lambda: _dot(za.astype(BF16), wpa_ref[...]),
                      lambda: jax.nn.sigmoid(proj(OFF_MG)),
                      lambda: _dot(zb.astype(BF16), wpb_ref[...]),
                      lambda: jax.nn.sigmoid(proj(OFF_MG + D_MODEL)),
                      lambda: jax.nn.sigmoid(proj(OFF_MG + 2 * D_MODEL))]
        merged = []
        for b in range(nbp):
            rows = slice(b * nt, (b + 1) * nt)
            for h in range(X_HEADS):
                cols = slice(h * X_HEAD_DIM, (h + 1) * X_HEAD_DIM)
                head_rows = pl.ds(h, N_MEM, stride=X_HEADS)
                mvx = jnp.concatenate([cmv_ref[b0 + b, head_rows, :].astype(BF16), mones], axis=1)
                xat_ref[rows, cols] = _cross_head(xq[rows, cols], cmk_ref[b0 + b, head_rows, :].astype(BF16), mvx)
            while len(merged) * nbp < (b + 1) * len(merge_work):
                merged.append(merge_work[len(merged)]())
        ya, gate_a, yb, gate_b, gate_x = merged
        zx = xat_ref[...] * _silu(gx)

        acc = gate_a * ya + gate_b * yb + gate_x * _dot(zx.astype(BF16), wpx_ref[...])
        out = x + _dot(acc.astype(BF16), wout_ref[...])
        if final:
            out = _rms_scale(out) * fg_ref[...]
        xo_ref[part_rows, :] = out
        return carry

    lax.fori_loop(0, DECODE_PARTS, part, 0)


def _layer_kernel(layer_ref, x_ref, g_ref, win32_ref, wpa32_ref, wpb32_ref, wpx32_ref, wout32_ref,
                  cw_ref, cb_ref, fg_ref, bias_ref, bd_ref, mk_ref, mvx_ref,
                  xs_ref, sbias_ref, sbd_ref, ck_ref, cv_ref, cc_ref, cmk_ref, cmv_ref,
                  xo_ref, kl_ref, vl_ref, ul_ref, xso_ref, ko_ref, vo_ref, uo_ref,
                  win_ref, wpa_ref, wpb_ref, wpx_ref, wout_ref,
                  kr_ref, vv_ref, uc_ref, q_ref, attn_ref, suc_ref, sattn_ref, sconv_ref, sxat_ref,
                  *, tile, tiles_per_seq, nb, nt, final):
    i = pl.program_id(0)
    weights = (win_ref, wpa_ref, wpb_ref, wpx_ref, wout_ref)

    @pl.when(i < CAST_STEPS)
    def _():
        def chunk_rows(src):
            rows = src.shape[0]
            return pl.ds(pl.multiple_of(i * rows, rows), rows)

        def cast_cols(c, carry):
            cols = pl.ds(pl.multiple_of(c * CAST_COLS, CAST_COLS), CAST_COLS)
            win_ref[chunk_rows(win32_ref), cols] = win32_ref[:, cols].astype(BF16)
            return carry

        lax.fori_loop(0, IN_W // CAST_COLS, cast_cols, 0)
        for src, dst in zip((wpa32_ref, wpb32_ref, wpx32_ref, wout32_ref), weights[1:]):
            dst[chunk_rows(src), :] = src[...].astype(BF16)

    @pl.when(i == CAST_STEPS)
    def _():
        _sample_step(xs_ref, g_ref, *weights, cw_ref, cb_ref,
                     sbias_ref, sbd_ref, ck_ref, cv_ref, cc_ref, cmk_ref, cmv_ref, fg_ref,
                     xso_ref, ko_ref, vo_ref, uo_ref, suc_ref, sattn_ref, sconv_ref, sxat_ref,
                     nb=nb, nt=nt, final=final)

    @pl.when(i > CAST_STEPS)
    def _():
        _prompt_tile(lax.rem(i - (CAST_STEPS + 1), tiles_per_seq), x_ref, g_ref, *weights, cw_ref, cb_ref,
                     bias_ref, bd_ref, mk_ref, mvx_ref, fg_ref,
                     xo_ref, kl_ref, vl_ref, ul_ref, kr_ref, vv_ref, uc_ref, q_ref, attn_ref,
                     tile=tile, final=final)


def _chunk_spec(rows, cols):
    assert rows % CAST_STEPS == 0
    return pl.BlockSpec((None, rows // CAST_STEPS, cols),
                        lambda i, l: (l[0], jnp.minimum(i, CAST_STEPS - 1), 0))


_WEIGHT_SHAPES = ((D_MODEL, IN_W), (ATT_W, D_MODEL), (CONV_W, D_MODEL), (X_W, D_MODEL), (D_MODEL, D_MODEL))


def _layer(layer, x, xs, g, win, wpa, wpb, wpx, wout, cw, cb, fg, bias, bd, mkb, mvx,
           sbias, sbd, ck, cv, cc, cmk, cmv, *, nt, final):
    nb, seq, _ = x.shape
    rows = xs.shape[0]
    nbs = rows // nt
    tile = PROMPT_TILE
    assert seq % tile == 0 and tile % WINDOW == 0
    tps = seq // tile
    kern = functools.partial(_layer_kernel, tile=tile, tiles_per_seq=tps, nb=nbs, nt=nt, final=final)

    def prompt_step(i):
        return jnp.maximum(i - (CAST_STEPS + 1), 0)

    def seq_of(i):
        return prompt_step(i) // tps

    def tile_of(i):
        return prompt_step(i) % tps

    grid_spec = pltpu.PrefetchScalarGridSpec(
        num_scalar_prefetch=1,
        grid=(CAST_STEPS + 1 + nb * tps,),
        in_specs=[
            pl.BlockSpec((1, tile, D_MODEL), lambda i, l: (seq_of(i), tile_of(i), 0)),
            _layer_spec((1, D_MODEL)),
            *[_chunk_spec(r, c) for r, c in _WEIGHT_SHAPES],
            _layer_spec((CONV_K, CONV_W)),
            _layer_spec((1, CONV_W)),
            _const_spec((1, D_MODEL)),
            _layer_spec((3, N_KV_HEADS, GROUP * CHUNK, KEY_SLOTS)),
            _const_spec((GROUP * CHUNK, GQ_W)),
            pl.BlockSpec((None, 1, N_MEM, X_W), lambda i, l: (l[0], seq_of(i), 0, 0)),
            pl.BlockSpec((None, 1, X_HEADS, N_MEM, 2 * X_HEAD_DIM), lambda i, l: (l[0], seq_of(i), 0, 0, 0)),
            _const_spec((rows, D_MODEL)),
            _layer_spec((N_KV_HEADS, GROUP * nt, KEY_SLOTS)),
            _const_spec((GROUP * nt, GQ_W)),
            _layer_spec((nbs, WINDOW, KV_W)),
            _layer_spec((nbs, WINDOW, KV_W)),
            _layer_spec((nbs, CONV_K - 1, CONV_W)),
            _layer_spec((nbs, N_MEM * X_HEADS, X_HEAD_DIM)),
            _layer_spec((nbs, N_MEM * X_HEADS, X_HEAD_DIM)),
        ],
        out_specs=[
            pl.BlockSpec((1, tile, D_MODEL), lambda i, l: (seq_of(i), tile_of(i), 0)),
            pl.BlockSpec((1, WINDOW, KV_W), lambda i, l: (seq_of(i), 0, 0)),
            pl.BlockSpec((1, WINDOW, KV_W), lambda i, l: (seq_of(i), 0, 0)),
            pl.BlockSpec((1, CONV_K - 1, CONV_W), lambda i, l: (seq_of(i), 0, 0)),
            pl.BlockSpec((rows, D_MODEL), lambda i, l: (0, 0)),
            pl.BlockSpec((rows, KV_W), lambda i, l: (0, 0)),
            pl.BlockSpec((rows, KV_W), lambda i, l: (0, 0)),
            pl.BlockSpec((nbs, CONV_K - 1, CONV_W), lambda i, l: (0, 0, 0)),
        ],
        scratch_shapes=[
            *[pltpu.VMEM(shape, BF16) for shape in _WEIGHT_SHAPES],
            pltpu.VMEM((N_KV_HEADS, WINDOW + tile, GQ_W), BF16),
            pltpu.VMEM((N_KV_HEADS, WINDOW + tile, 2 * LANES), BF16),
            pltpu.VMEM((CONV_PAD + tile, CONV_W), F32),
            pltpu.VMEM((tile, ATT_W), BF16),
            pltpu.VMEM((tile, ATT_W), F32),
            pltpu.VMEM((nbs // DECODE_PARTS, CONV_PAD + nt, CONV_W), F32),
            pltpu.VMEM((rows // DECODE_PARTS, ATT_W), F32),
            pltpu.VMEM((rows // DECODE_PARTS, CONV_W), F32),
            pltpu.VMEM((rows // DECODE_PARTS, X_W), F32),
        ],
    )
    return pl.pallas_call(
        kern,
        grid_spec=grid_spec,
        out_shape=[
            jax.ShapeDtypeStruct((nb, seq, D_MODEL), F32),
            jax.ShapeDtypeStruct((nb, WINDOW, KV_W), F32),
            jax.ShapeDtypeStruct((nb, WINDOW, KV_W), F32),
            jax.ShapeDtypeStruct((nb, CONV_K - 1, CONV_W), F32),
            jax.ShapeDtypeStruct((rows, D_MODEL), F32),
            jax.ShapeDtypeStruct((rows, KV_W), F32),
            jax.ShapeDtypeStruct((rows, KV_W), F32),
            jax.ShapeDtypeStruct((nbs, CONV_K - 1, CONV_W), F32),
        ],
        compiler_params=pltpu.CompilerParams(
            dimension_semantics=("arbitrary",), vmem_limit_bytes=VMEM_LIMIT),
        name="layer",
    )(layer, x, g, win, wpa, wpb, wpx, wout, cw, cb, fg, bias, bd, mkb, mvx,
      xs, sbias, sbd, ck, cv, cc, cmk, cmv)


def _alibi_slopes():
    return np.power(2.0, -8.0 * np.arange(1, N_HEADS + 1, dtype=np.float32) / N_HEADS).astype(np.float32)


def _alibi_bias(nq, nk, masked):
    dist = np.abs(np.arange(nq)[:, None] + WINDOW - np.arange(nk)[None, :]).astype(np.float32)
    slope = _alibi_slopes().reshape(N_KV_HEADS, GROUP)
    bias = -(slope[:, :, None, None] * dist[None, None])
    bias = np.where(np.arange(nk)[None, None, None, :] < masked, -np.inf, bias)
    return bias.reshape(N_KV_HEADS, GROUP * nq, nk).astype(np.float32)


def _bias_with_sink(alibi, attn_sink, nq):
    nk = alibi.shape[-1]
    lead = alibi.shape[:-3]
    sink = jnp.repeat(attn_sink.reshape(DEPTH, N_KV_HEADS, GROUP), nq, axis=2)[..., None]
    sink = jnp.broadcast_to(sink.reshape((DEPTH,) + (1,) * len(lead) + sink.shape[1:]),
                            (DEPTH,) + alibi.shape[:-1] + (1,))
    real = jnp.broadcast_to(jnp.asarray(alibi), (DEPTH,) + alibi.shape)
    unused = jnp.full((DEPTH,) + alibi.shape[:-1] + (KEY_SLOTS - nk - 1,), -jnp.inf, F32)
    return jnp.concatenate([real, sink, unused], axis=-1)


def _block_diag_mask(nq):
    rows = np.arange(GROUP * nq)[:, None] // nq
    cols = np.arange(GQ_W)[None, :] // HEAD_DIM
    return (rows == cols).astype(np.float32)


def kernel(x_prompt, x_sample, cache_attn_k, cache_attn_v, cache_conv, cache_mem_k, cache_mem_v,
           mem_prompt, norm_g, w_in, attn_sink, w_pa, conv_w, conv_b, w_pb, mem_norm_g,
           w_mk, w_mv, w_px, w_out, final_g):
    bp, seq, _ = x_prompt.shape
    bs, nt, _ = x_sample.shape
    win_len = cache_attn_k.shape[2]
    assert win_len == WINDOW and nt <= CHUNK and seq >= WINDOW

    bias_p = _bias_with_sink(np.stack([_alibi_bias(CHUNK, BAND, m) for m in (WINDOW, CHUNK, 0)]), attn_sink, CHUNK)
    bias_s = _bias_with_sink(_alibi_bias(nt, WINDOW + nt, 0), attn_sink, nt)
    bd_p = jnp.asarray(_block_diag_mask(CHUNK), BF16)
    bd_s = jnp.asarray(_block_diag_mask(nt), BF16)
    fg = final_g.reshape(1, D_MODEL)

    mk, mv, mkb, mvx = _memory_kv(mem_prompt, mem_norm_g, w_mk, w_mv)

    xp = x_prompt
    xs = x_sample.reshape(bs * nt, D_MODEL)
    ck = cache_attn_k.reshape(DEPTH, bs, WINDOW, KV_W)
    cv = cache_attn_v.reshape(DEPTH, bs, WINDOW, KV_W)
    g = norm_g.reshape(DEPTH, 1, D_MODEL)
    cb = conv_b.reshape(DEPTH, 1, CONV_W)
    cmk = cache_mem_k.reshape(DEPTH, bs, N_MEM * X_HEADS, X_HEAD_DIM)
    cmv = cache_mem_v.reshape(DEPTH, bs, N_MEM * X_HEADS, X_HEAD_DIM)
    pk, pv, pc, sk, sv, sc = [], [], [], [], [], []
    for l in range(DEPTH):
        final = l == DEPTH - 1
        layer = jnp.full((1,), l, jnp.int32)
        xp, kl, vl, ul, xs, ko, vo, uo = _layer(
            layer, xp, xs, g, w_in, w_pa, w_pb, w_px, w_out, conv_w, cb, fg, bias_p, bd_p, mkb, mvx,
            bias_s, bd_s, ck, cv, cache_conv, cmk, cmv, nt=nt, final=final)
        pk.append(kl)
        pv.append(vl)
        pc.append(ul)
        sk.append(ko)
        sv.append(vo)
        sc.append(uo)

    kv_p = (DEPTH, bp, WINDOW, N_KV_HEADS, HEAD_DIM)
    kv_s = (DEPTH, bs, nt, N_KV_HEADS, HEAD_DIM)
    mem_shape = (DEPTH, bp, N_MEM, X_HEADS, X_HEAD_DIM)
    return (xp, xs.reshape(bs, nt, D_MODEL),
            jnp.stack(pk).reshape(kv_p), jnp.stack(pv).reshape(kv_p), jnp.stack(pc),
            mk.reshape(mem_shape), mv.reshape(mem_shape),
            jnp.stack(sk).reshape(kv_s), jnp.stack(sv).reshape(kv_s), jnp.stack(sc))
```

```python
import functools

import numpy as np
import jax
import jax.numpy as jnp
from jax import lax
from jax.experimental import pallas as pl
from jax.experimental.pallas import tpu as pltpu

D_MODEL = 1024
DEPTH = 4
CHUNK = 64
WINDOW = 128
N_HEADS = 8
N_KV_HEADS = 2
GROUP = N_HEADS // N_KV_HEADS
HEAD_DIM = 64
ATT_W = N_HEADS * HEAD_DIM
KV_W = N_KV_HEADS * HEAD_DIM
CONV_W = 512
CONV_K = 3
X_HEADS = 4
X_HEAD_DIM = 128
X_W = X_HEADS * X_HEAD_DIM
N_MEM = 256
EPS = 1e-6

OFF_Q = 0
OFF_K = OFF_Q + ATT_W
OFF_V = OFF_K + KV_W
OFF_GA = OFF_V + KV_W
OFF_BB = OFF_GA + ATT_W
OFF_CC = OFF_BB + CONV_W
OFF_HH = OFF_CC + CONV_W
OFF_GB = OFF_HH + CONV_W
OFF_XQ = OFF_GB + CONV_W
OFF_GX = OFF_XQ + X_W
OFF_MG = OFF_GX + X_W
IN_W = OFF_MG + 3 * D_MODEL
W_IN_WIDTH = {OFF_Q: ATT_W, OFF_K: 2 * KV_W, OFF_GA: ATT_W, OFF_BB: CONV_W, OFF_CC: CONV_W, OFF_HH: CONV_W,
              OFF_GB: CONV_W, OFF_XQ: X_W, OFF_GX: X_W,
              OFF_MG: D_MODEL, OFF_MG + D_MODEL: D_MODEL, OFF_MG + 2 * D_MODEL: D_MODEL}

LANES = 128
GQ_W = GROUP * HEAD_DIM
BAND = WINDOW + CHUNK
KEY_SLOTS = 256
CONV_PAD = 8
PROMPT_TILE = 512
DECODE_PARTS = 2
Q_ROWS = 128
CAST_STEPS = 8
CAST_COLS = 256
VMEM_LIMIT = 61 * 1024 * 1024

F32 = jnp.float32
BF16 = jnp.bfloat16
_NT = (((1,), (1,)), ((), ()))


def _rms_scale(x):
    return x * lax.rsqrt(jnp.mean(x * x, axis=-1, keepdims=True) + EPS)


def _silu(x):
    return x * jax.nn.sigmoid(x)


def _dot(a, b):
    return jnp.dot(a, b, preferred_element_type=F32)


def _w_in(win_ref, off):
    return win_ref[:, off:off + W_IN_WIDTH[off]]


def _dup_heads(a):
    lane = lax.broadcasted_iota(jnp.int32, a.shape, 1)
    sw = pltpu.roll(a, HEAD_DIM, 1)
    lo = lane < HEAD_DIM
    return jnp.where(lo, a, sw), jnp.where(lo, sw, a)


def _key_tails(nk):
    shape = (KEY_SLOTS - nk, 2 * LANES)
    row = lax.broadcasted_iota(jnp.int32, shape, 0)
    col = lax.broadcasted_iota(jnp.int32, shape, 1)
    vtail = jnp.where((row == 0) & (col >= LANES), 1.0, 0.0).astype(BF16)
    return jnp.zeros(shape, BF16), vtail


def _band_attention(qbd, kb, vb, bias):
    s = lax.dot_general(qbd, kb, _NT, preferred_element_type=F32) + bias
    p = jnp.exp(s - jnp.max(s, axis=-1, keepdims=True)).astype(BF16)
    oe = _dot(p, vb)
    return oe[:, 0:LANES] / oe[:, LANES:2 * LANES]


def _ungroup(on, nq):
    lo = lax.broadcasted_iota(jnp.int32, (nq, LANES), 1) < HEAD_DIM
    return (jnp.where(lo, on[0:nq], on[nq:2 * nq]),
            jnp.where(lo, on[2 * nq:3 * nq], on[3 * nq:4 * nq]))


def _cross_head(xq_h, mk_h, mvx_h):
    s = lax.dot_general(xq_h, mk_h, _NT, preferred_element_type=F32)
    p = jnp.exp(s - jnp.max(s, axis=-1, keepdims=True)).astype(BF16)
    oe = _dot(p, mvx_h)
    return oe[:, 0:X_HEAD_DIM] / oe[:, X_HEAD_DIM:2 * X_HEAD_DIM]


def _memkv_kernel(mem_ref, g_ref, wk_ref, wv_ref, mk_ref, mv_ref, mkb_ref, mvx_ref):
    nb = mem_ref.shape[0]
    mn = (_rms_scale(mem_ref[...].reshape(nb * N_MEM, D_MODEL)) * g_ref[0]).astype(BF16)
    mk = _dot(mn, wk_ref[0].astype(BF16))
    mv = _dot(mn, wv_ref[0].astype(BF16))
    ones = jnp.ones((N_MEM, X_HEAD_DIM), BF16)
    for b in range(nb):
        rows = slice(b * N_MEM, (b + 1) * N_MEM)
        mkb_ref[0, b] = mk[rows].astype(BF16)
        for h in range(X_HEADS):
            cols = slice(h * X_HEAD_DIM, (h + 1) * X_HEAD_DIM)
            mk_ref[0, b, pl.ds(h, N_MEM, stride=X_HEADS), :] = mk[rows, cols]
            mv_ref[0, b, pl.ds(h, N_MEM, stride=X_HEADS), :] = mv[rows, cols]
            mvx_ref[0, b, h, :, 0:X_HEAD_DIM] = mv[rows, cols].astype(BF16)
            mvx_ref[0, b, h, :, X_HEAD_DIM:2 * X_HEAD_DIM] = ones


def _memory_kv(mem, g, wk, wv):
    nb = mem.shape[0]
    return pl.pallas_call(
        _memkv_kernel,
        grid=(DEPTH,),
        in_specs=[
            pl.BlockSpec((nb, N_MEM, D_MODEL), lambda l: (0, 0, 0)),
            pl.BlockSpec((1, 1, D_MODEL), lambda l: (l, 0, 0)),
            pl.BlockSpec((1, D_MODEL, X_W), lambda l: (l, 0, 0)),
            pl.BlockSpec((1, D_MODEL, X_W), lambda l: (l, 0, 0)),
        ],
        out_specs=[
            pl.BlockSpec((1, nb, N_MEM * X_HEADS, X_HEAD_DIM), lambda l: (l, 0, 0, 0)),
            pl.BlockSpec((1, nb, N_MEM * X_HEADS, X_HEAD_DIM), lambda l: (l, 0, 0, 0)),
            pl.BlockSpec((1, nb, N_MEM, X_W), lambda l: (l, 0, 0, 0)),
            pl.BlockSpec((1, nb, X_HEADS, N_MEM, 2 * X_HEAD_DIM), lambda l: (l, 0, 0, 0, 0)),
        ],
        out_shape=[
            jax.ShapeDtypeStruct((DEPTH, nb, N_MEM * X_HEADS, X_HEAD_DIM), F32),
            jax.ShapeDtypeStruct((DEPTH, nb, N_MEM * X_HEADS, X_HEAD_DIM), F32),
            jax.ShapeDtypeStruct((DEPTH, nb, N_MEM, X_W), BF16),
            jax.ShapeDtypeStruct((DEPTH, nb, X_HEADS, N_MEM, 2 * X_HEAD_DIM), BF16),
        ],
        name="memory_kv",
    )(mem, g.reshape(DEPTH, 1, D_MODEL), wk, wv)


def _prompt_tile(t, x_ref, g_ref, win_ref, wpa_ref, wpb_ref, wpx_ref, wout_ref, cw_ref, cb_ref,
                 bias_ref, bd_ref, mk_ref, mvx_ref, fg_ref,
                 xo_ref, kl_ref, vl_ref, ul_ref,
                 kr_ref, vv_ref, uc_ref, q_ref, attn_ref, *, tile, final):
    @pl.when(t == 0)
    def _():
        kr_ref[:, 0:WINDOW, :] = jnp.zeros((N_KV_HEADS, WINDOW, GQ_W), BF16)
        vv_ref[:, 0:WINDOW, 0:LANES] = jnp.zeros((N_KV_HEADS, WINDOW, LANES), BF16)
        vv_ref[:, :, LANES:2 * LANES] = jnp.ones((N_KV_HEADS, WINDOW + tile, LANES), BF16)
        uc_ref[0:CONV_PAD, :] = jnp.zeros((CONV_PAD, CONV_W), F32)

    x = x_ref[0]
    xn = (_rms_scale(x) * g_ref[...]).astype(BF16)

    def proj(off):
        return _dot(xn, _w_in(win_ref, off))

    w_q = _w_in(win_ref, OFF_Q)
    for r in range(0, tile, Q_ROWS):
        q_ref[r:r + Q_ROWS, :] = (_dot(xn[r:r + Q_ROWS], w_q) * (HEAD_DIM ** -0.5)).astype(BF16)
    half = tile // 2
    w_kv = _w_in(win_ref, OFF_K)
    kv = jnp.concatenate([_dot(xn[0:half], w_kv), _dot(xn[half:tile], w_kv)], axis=0)
    k = kv[:, 0:KV_W]
    v = kv[:, KV_W:2 * KV_W]
    ga = proj(OFF_GA)
    kl_ref[0] = k[tile - WINDOW:tile]
    vl_ref[0] = v[tile - WINDOW:tile]
    for j, (kj, vj) in enumerate(zip(_dup_heads(k), _dup_heads(v))):
        kj = kj.astype(BF16)
        kr_ref[j, WINDOW:WINDOW + tile, 0:LANES] = kj
        kr_ref[j, WINDOW:WINDOW + tile, LANES:2 * LANES] = kj
        vv_ref[j, WINDOW:WINDOW + tile, 0:LANES] = vj.astype(BF16)

    fillers = [functools.partial(proj, off) for off in (OFF_BB, OFF_CC, OFF_HH, OFF_GB, OFF_XQ, OFF_GX)]
    fillers += [lambda i=i: jax.nn.sigmoid(proj(OFF_MG + i * D_MODEL)) for i in (1, 2)]
    filled = []
    n_chunks = tile // CHUNK
    ktail, vtail = _key_tails(BAND)

    for c in range(n_chunks):
        r0 = c * CHUNK
        qc = q_ref[r0:r0 + CHUNK, :]
        variant = jnp.where(t == 0, c, 2) if c < 2 else 2
        for j in range(N_KV_HEADS):
            qg = qc[:, j * GQ_W:(j + 1) * GQ_W]
            qbd = jnp.concatenate([qg] * GROUP, axis=0) * bd_ref[...]
            on = _band_attention(qbd, jnp.concatenate([kr_ref[j, r0:r0 + BAND, :], ktail], axis=0),
                                 jnp.concatenate([vv_ref[j, r0:r0 + BAND, :], vtail], axis=0),
                                 bias_ref[variant, j])
            lo, hi = _ungroup(on, CHUNK)
            attn_ref[r0:r0 + CHUNK, j * GQ_W:j * GQ_W + LANES] = lo
            attn_ref[r0:r0 + CHUNK, j * GQ_W + LANES:(j + 1) * GQ_W] = hi
        while len(filled) * n_chunks < (c + 1) * len(fillers):
            filled.append(fillers[len(filled)]())
    bb, cc, hh, gb, xq, gx, gate_b, gate_x = filled
    kr_ref[:, 0:WINDOW, :] = kr_ref[:, tile:tile + WINDOW, :]
    vv_ref[:, 0:WINDOW, 0:LANES] = vv_ref[:, tile:tile + WINDOW, 0:LANES]

    u = cc * hh
    uc_ref[CONV_PAD:CONV_PAD + tile, :] = u
    ul_ref[0] = u[tile - (CONV_K - 1):tile]
    conv = (uc_ref[CONV_PAD - 2:CONV_PAD - 2 + tile, :] * cw_ref[0:1, :]
            + uc_ref[CONV_PAD - 1:CONV_PAD - 1 + tile, :] * cw_ref[1:2, :]
            + u * cw_ref[2:3, :]) + cb_ref[...]
    uc_ref[0:CONV_PAD, :] = uc_ref[tile:tile + CONV_PAD, :]
    zb = (bb * conv * _silu(gb)).astype(BF16)
    za = (attn_ref[...] * _silu(ga)).astype(BF16)

    xq = (xq * (X_HEAD_DIM ** -0.5)).astype(BF16)
    merge_work = [lambda: _dot(za, wpa_ref[...]),
                  lambda: jax.nn.sigmoid(proj(OFF_MG)),
                  lambda: _dot(zb, wpb_ref[...])]
    merged = []
    heads = []
    for h in range(X_HEADS):
        heads.append(_cross_head(xq[:, h * X_HEAD_DIM:(h + 1) * X_HEAD_DIM],
                                 mk_ref[0, :, h * X_HEAD_DIM:(h + 1) * X_HEAD_DIM], mvx_ref[0, h]))
        if h < len(merge_work):
            merged.append(merge_work[h]())
    ya, gate_a, yb = merged
    zx = (jnp.concatenate(heads, axis=1) * _silu(gx)).astype(BF16)
    acc = gate_a * ya + gate_b * yb + gate_x * _dot(zx, wpx_ref[...])
    out = x + _dot(acc.astype(BF16), wout_ref[...])
    if final:
        out = _rms_scale(out) * fg_ref[...]
    xo_ref[0] = out


def _const_spec(shape):
    nd = len(shape)
    return pl.BlockSpec(shape, lambda *_: (0,) * nd, pipeline_mode=pl.Buffered(1))


def _layer_spec(shape):
    nd = len(shape)
    return pl.BlockSpec((None,) + tuple(shape), lambda *a: (a[-1][0],) + (0,) * nd, pipeline_mode=pl.Buffered(1))


def _sample_step(x_ref, g_ref, win_ref, wpa_ref, wpb_ref, wpx_ref, wout_ref, cw_ref, cb_ref,
                 bias_ref, bd_ref, ck_ref, cv_ref, cc_ref, cmk_hbm, cmv_hbm, fg_ref,
                 xo_ref, ko_ref, vo_ref, uo_ref,
                 uc_ref, attn_ref, conv_ref, xat_ref, cmem_ref, cmem_sem, *, layer, nb, nt, final):
    nbp = nb // DECODE_PARTS
    rows_p = nbp * nt

    def part(pi, carry):
        b0 = pi * nbp
        part_rows = pl.ds(pl.multiple_of(pi * rows_p, rows_p), rows_p)
        mem_copies = [pltpu.make_async_copy(src.at[layer, pl.ds(b0, nbp)], cmem_ref.at[n], cmem_sem.at[n])
                      for n, src in enumerate((cmk_hbm, cmv_hbm))]
        for copy in mem_copies:
            copy.start()
        x = x_ref[part_rows, :]
        xn = (_rms_scale(x) * g_ref[...]).astype(BF16)

        def proj(off):
            return _dot(xn, _w_in(win_ref, off))

        q = (proj(OFF_Q) * (HEAD_DIM ** -0.5)).astype(BF16)
        kv = proj(OFF_K)
        k = kv[:, 0:KV_W]
        v = kv[:, KV_W:2 * KV_W]
        ga = proj(OFF_GA)
        ko_ref[part_rows, :] = k
        vo_ref[part_rows, :] = v
        ones = jnp.ones((WINDOW + nt, LANES), BF16)
        ktail, vtail = _key_tails(WINDOW + nt)
        fillers = [functools.partial(proj, off) for off in (OFF_BB, OFF_CC, OFF_HH, OFF_GB, OFF_XQ, OFF_GX)]
        filled = []
        for b in range(nbp):
            rows = slice(b * nt, (b + 1) * nt)
            kall = _dup_heads(jnp.concatenate([ck_ref[b0 + b], k[rows]], axis=0))
            vall = _dup_heads(jnp.concatenate([cv_ref[b0 + b], v[rows]], axis=0))
            for j in range(N_KV_HEADS):
                kj = kall[j].astype(BF16)
                vj = vall[j].astype(BF16)
                qg = q[rows, j * GQ_W:(j + 1) * GQ_W]
                qbd = jnp.concatenate([qg] * GROUP, axis=0) * bd_ref[...]
                on = _band_attention(qbd, jnp.concatenate([jnp.concatenate([kj, kj], axis=1), ktail], axis=0),
                                     jnp.concatenate([jnp.concatenate([vj, ones], axis=1), vtail], axis=0),
                                     bias_ref[j])
                lo, hi = _ungroup(on, nt)
                attn_ref[rows, j * GQ_W:j * GQ_W + LANES] = lo
                attn_ref[rows, j * GQ_W + LANES:(j + 1) * GQ_W] = hi
            while len(filled) * nbp < (b + 1) * len(fillers):
                filled.append(fillers[len(filled)]())
        bb, cc, hh, gb, p3q, gx = filled
        za = attn_ref[...] * _silu(ga)
        for copy in mem_copies:
            copy.wait()

        u = cc * hh
        for b in range(nbp):
            rows = slice(b * nt, (b + 1) * nt)
            ub = u[rows]
            uc_ref[b, CONV_PAD - (CONV_K - 1):CONV_PAD, :] = cc_ref[b0 + b]
            uc_ref[b, CONV_PAD:CONV_PAD + nt, :] = ub
            uo_ref[b0 + b] = ub[nt - (CONV_K - 1):nt]
            conv_ref[rows, :] = (uc_ref[b, CONV_PAD - 2:CONV_PAD - 2 + nt, :] * cw_ref[0:1, :]
                                 + uc_ref[b, CONV_PAD - 1:CONV_PAD - 1 + nt, :] * cw_ref[1:2, :]
                                 + ub * cw_ref[2:3, :]) + cb_ref[...]
        zb = bb * conv_ref[...] * _silu(gb)

        xq = (p3q * (X_HEAD_DIM ** -0.5)).astype(BF16)
        mones = jnp.ones((N_MEM, X_HEAD_DIM), BF16)
        merge_work = [lambda: _dot(za.astype(BF16), wpa_ref[...]),
                      lambda: jax.nn.sigmoid(proj(OFF_MG)),
                      lambda: _dot(zb.astype(BF16), wpb_ref[...]),
                      lambda: jax.nn.sigmoid(proj(OFF_MG + D_MODEL)),
                      lambda: jax.nn.sigmoid(proj(OFF_MG + 2 * D_MODEL))]
        merged = []
        for b in range(nbp):
            rows = slice(b * nt, (b + 1) * nt)
            for h in range(X_HEADS):
                cols = slice(h * X_HEAD_DIM, (h + 1) * X_HEAD_DIM)
                head_rows = pl.ds(h, N_MEM, stride=X_HEADS)
                mvx = jnp.concatenate([cmem_ref[1, b, head_rows, :].astype(BF16), mones], axis=1)
                xat_ref[rows, cols] = _cross_head(xq[rows, cols], cmem_ref[0, b, head_rows, :].astype(BF16), mvx)
            while len(merged) * nbp < (b + 1) * len(merge_work):
                merged.append(merge_work[len(merged)]())
        ya, gate_a, yb, gate_b, gate_x = merged
        zx = xat_ref[...] * _silu(gx)

        acc = gate_a * ya + gate_b * yb + gate_x * _dot(zx.astype(BF16), wpx_ref[...])
        out = x + _dot(acc.astype(BF16), wout_ref[...])
        if final:
            out = _rms_scale(out) * fg_ref[...]
        xo_ref[part_rows, :] = out
        return carry

    lax.fori_loop(0, DECODE_PARTS, part, 0)


def _layer_kernel(layer_ref, x_ref, g_ref, win32_ref, wpa32_ref, wpb32_ref, wpx32_ref, wout32_ref,
                  cw_ref, cb_ref, fg_ref, bias_ref, bd_ref, mk_ref, mvx_ref,
                  xs_ref, sbias_ref, sbd_ref, ck_ref, cv_ref, cc_ref, cmk_hbm, cmv_hbm,
                  xo_ref, kl_ref, vl_ref, ul_ref, xso_ref, ko_ref, vo_ref, uo_ref,
                  win_ref, wpa_ref, wpb_ref, wpx_ref, wout_ref,
                  kr_ref, vv_ref, uc_ref, q_ref, attn_ref, suc_ref, sattn_ref, sconv_ref, sxat_ref, cmem_ref, cmem_sem,
                  *, tile, tiles_per_seq, nb, nt, final):
    i = pl.program_id(0)
    weights = (win_ref, wpa_ref, wpb_ref, wpx_ref, wout_ref)

    @pl.when(i < CAST_STEPS)
    def _():
        def chunk_rows(src):
            rows = src.shape[0]
            return pl.ds(pl.multiple_of(i * rows, rows), rows)

        def cast_cols(c, carry):
            cols = pl.ds(pl.multiple_of(c * CAST_COLS, CAST_COLS), CAST_COLS)
            win_ref[chunk_rows(win32_ref), cols] = win32_ref[:, cols].astype(BF16)
            return carry

        lax.fori_loop(0, IN_W // CAST_COLS, cast_cols, 0)
        for src, dst in zip((wpa32_ref, wpb32_ref, wpx32_ref, wout32_ref), weights[1:]):
            dst[chunk_rows(src), :] = src[...].astype(BF16)

    @pl.when(i == CAST_STEPS)
    def _():
        _sample_step(xs_ref, g_ref, *weights, cw_ref, cb_ref,
                     sbias_ref, sbd_ref, ck_ref, cv_ref, cc_ref, cmk_hbm, cmv_hbm, fg_ref,
                     xso_ref, ko_ref, vo_ref, uo_ref, suc_ref, sattn_ref, sconv_ref, sxat_ref, cmem_ref, cmem_sem,
                     layer=layer_ref[0], nb=nb, nt=nt, final=final)

    @pl.when(i > CAST_STEPS)
    def _():
        _prompt_tile(lax.rem(i - (CAST_STEPS + 1), tiles_per_seq), x_ref, g_ref, *weights, cw_ref, cb_ref,
                     bias_ref, bd_ref, mk_ref, mvx_ref, fg_ref,
                     xo_ref, kl_ref, vl_ref, ul_ref, kr_ref, vv_ref, uc_ref, q_ref, attn_ref,
                     tile=tile, final=final)


def _chunk_spec(rows, cols):
    assert rows % CAST_STEPS == 0
    return pl.BlockSpec((None, rows // CAST_STEPS, cols),
                        lambda i, l: (l[0], jnp.minimum(i, CAST_STEPS - 1), 0))


_WEIGHT_SHAPES = ((D_MODEL, IN_W), (ATT_W, D_MODEL), (CONV_W, D_MODEL), (X_W, D_MODEL), (D_MODEL, D_MODEL))


def _layer(layer, x, xs, g, win, wpa, wpb, wpx, wout, cw, cb, fg, bias, bd, mkb, mvx,
           sbias, sbd, ck, cv, cc, cmk, cmv, *, nt, final):
    nb, seq, _ = x.shape
    rows = xs.shape[0]
    nbs = rows // nt
    tile = PROMPT_TILE
    assert seq % tile == 0 and tile % WINDOW == 0
    tps = seq // tile
    kern = functools.partial(_layer_kernel, tile=tile, tiles_per_seq=tps, nb=nbs, nt=nt, final=final)

    def prompt_step(i):
        return jnp.maximum(i - (CAST_STEPS + 1), 0)

    def seq_of(i):
        return prompt_step(i) // tps

    def tile_of(i):
        return prompt_step(i) % tps

    grid_spec = pltpu.PrefetchScalarGridSpec(
        num_scalar_prefetch=1,
        grid=(CAST_STEPS + 1 + nb * tps,),
        in_specs=[
            pl.BlockSpec((1, tile, D_MODEL), lambda i, l: (seq_of(i), tile_of(i), 0)),
            _layer_spec((1, D_MODEL)),
            *[_chunk_spec(r, c) for r, c in _WEIGHT_SHAPES],
            _layer_spec((CONV_K, CONV_W)),
            _layer_spec((1, CONV_W)),
            _const_spec((1, D_MODEL)),
            _layer_spec((3, N_KV_HEADS, GROUP * CHUNK, KEY_SLOTS)),
            _const_spec((GROUP * CHUNK, GQ_W)),
            pl.BlockSpec((None, 1, N_MEM, X_W), lambda i, l: (l[0], seq_of(i), 0, 0)),
            pl.BlockSpec((None, 1, X_HEADS, N_MEM, 2 * X_HEAD_DIM), lambda i, l: (l[0], seq_of(i), 0, 0, 0)),
            _const_spec((rows, D_MODEL)),
            _layer_spec((N_KV_HEADS, GROUP * nt, KEY_SLOTS)),
            _const_spec((GROUP * nt, GQ_W)),
            _layer_spec((nbs, WINDOW, KV_W)),
            _layer_spec((nbs, WINDOW, KV_W)),
            _layer_spec((nbs, CONV_K - 1, CONV_W)),
            pl.BlockSpec(memory_space=pl.ANY),
            pl.BlockSpec(memory_space=pl.ANY),
        ],
        out_specs=[
            pl.BlockSpec((1, tile, D_MODEL), lambda i, l: (seq_of(i), tile_of(i), 0)),
            pl.BlockSpec((1, WINDOW, KV_W), lambda i, l: (seq_of(i), 0, 0)),
            pl.BlockSpec((1, WINDOW, KV_W), lambda i, l: (seq_of(i), 0, 0)),
            pl.BlockSpec((1, CONV_K - 1, CONV_W), lambda i, l: (seq_of(i), 0, 0)),
            pl.BlockSpec((rows, D_MODEL), lambda i, l: (0, 0)),
            pl.BlockSpec((rows, KV_W), lambda i, l: (0, 0)),
            pl.BlockSpec((rows, KV_W), lambda i, l: (0, 0)),
            pl.BlockSpec((nbs, CONV_K - 1, CONV_W), lambda i, l: (0, 0, 0)),
        ],
        scratch_shapes=[
            *[pltpu.VMEM(shape, BF16) for shape in _WEIGHT_SHAPES],
            pltpu.VMEM((N_KV_HEADS, WINDOW + tile, GQ_W), BF16),
            pltpu.VMEM((N_KV_HEADS, WINDOW + tile, 2 * LANES), BF16),
            pltpu.VMEM((CONV_PAD + tile, CONV_W), F32),
            pltpu.VMEM((tile, ATT_W), BF16),
            pltpu.VMEM((tile, ATT_W), F32),
            pltpu.VMEM((nbs // DECODE_PARTS, CONV_PAD + nt, CONV_W), F32),
            pltpu.VMEM((rows // DECODE_PARTS, ATT_W), F32),
            pltpu.VMEM((rows // DECODE_PARTS, CONV_W), F32),
            pltpu.VMEM((rows // DECODE_PARTS, X_W), F32),
            pltpu.VMEM((2, nbs // DECODE_PARTS, N_MEM * X_HEADS, X_HEAD_DIM), F32),
            pltpu.SemaphoreType.DMA((2,)),
        ],
    )
    return pl.pallas_call(
        kern,
        grid_spec=grid_spec,
        out_shape=[
            jax.ShapeDtypeStruct((nb, seq, D_MODEL), F32),
            jax.ShapeDtypeStruct((nb, WINDOW, KV_W), F32),
            jax.ShapeDtypeStruct((nb, WINDOW, KV_W), F32),
            jax.ShapeDtypeStruct((nb, CONV_K - 1, CONV_W), F32),
            jax.ShapeDtypeStruct((rows, D_MODEL), F32),
            jax.ShapeDtypeStruct((rows, KV_W), F32),
            jax.ShapeDtypeStruct((rows, KV_W), F32),
            jax.ShapeDtypeStruct((nbs, CONV_K - 1, CONV_W), F32),
        ],
        compiler_params=pltpu.CompilerParams(
            dimension_semantics=("arbitrary",), vmem_limit_bytes=VMEM_LIMIT),
        name="layer",
    )(layer, x, g, win, wpa, wpb, wpx, wout, cw, cb, fg, bias, bd, mkb, mvx,
      xs, sbias, sbd, ck, cv, cc, cmk, cmv)


def _alibi_slopes():
    return np.power(2.0, -8.0 * np.arange(1, N_HEADS + 1, dtype=np.float32) / N_HEADS).astype(np.float32)


def _alibi_bias(nq, nk, masked):
    dist = np.abs(np.arange(nq)[:, None] + WINDOW - np.arange(nk)[None, :]).astype(np.float32)
    slope = _alibi_slopes().reshape(N_KV_HEADS, GROUP)
    bias = -(slope[:, :, None, None] * dist[None, None])
    bias = np.where(np.arange(nk)[None, None, None, :] < masked, -np.inf, bias)
    return bias.reshape(N_KV_HEADS, GROUP * nq, nk).astype(np.float32)


def _bias_with_sink(alibi, attn_sink, nq):
    nk = alibi.shape[-1]
    lead = alibi.shape[:-3]
    sink = jnp.repeat(attn_sink.reshape(DEPTH, N_KV_HEADS, GROUP), nq, axis=2)[..., None]
    sink = jnp.broadcast_to(sink.reshape((DEPTH,) + (1,) * len(lead) + sink.shape[1:]),
                            (DEPTH,) + alibi.shape[:-1] + (1,))
    real = jnp.broadcast_to(jnp.asarray(alibi), (DEPTH,) + alibi.shape)
    unused = jnp.full((DEPTH,) + alibi.shape[:-1] + (KEY_SLOTS - nk - 1,), -jnp.inf, F32)
    return jnp.concatenate([real, sink, unused], axis=-1)


def _block_diag_mask(nq):
    rows = np.arange(GROUP * nq)[:, None] // nq
    cols = np.arange(GQ_W)[None, :] // HEAD_DIM
    return (rows == cols).astype(np.float32)


def kernel(x_prompt, x_sample, cache_attn_k, cache_attn_v, cache_conv, cache_mem_k, cache_mem_v,
           mem_prompt, norm_g, w_in, attn_sink, w_pa, conv_w, conv_b, w_pb, mem_norm_g,
           w_mk, w_mv, w_px, w_out, final_g):
    bp, seq, _ = x_prompt.shape
    bs, nt, _ = x_sample.shape
    win_len = cache_attn_k.shape[2]
    assert win_len == WINDOW and nt <= CHUNK and seq >= WINDOW

    bias_p = _bias_with_sink(np.stack([_alibi_bias(CHUNK, BAND, m) for m in (WINDOW, CHUNK, 0)]), attn_sink, CHUNK)
    bias_s = _bias_with_sink(_alibi_bias(nt, WINDOW + nt, 0), attn_sink, nt)
    bd_p = jnp.asarray(_block_diag_mask(CHUNK), BF16)
    bd_s = jnp.asarray(_block_diag_mask(nt), BF16)
    fg = final_g.reshape(1, D_MODEL)

    mk, mv, mkb, mvx = _memory_kv(mem_prompt, mem_norm_g, w_mk, w_mv)

    xp = x_prompt
    xs = x_sample.reshape(bs * nt, D_MODEL)
    ck = cache_attn_k.reshape(DEPTH, bs, WINDOW, KV_W)
    cv = cache_attn_v.reshape(DEPTH, bs, WINDOW, KV_W)
    g = norm_g.reshape(DEPTH, 1, D_MODEL)
    cb = conv_b.reshape(DEPTH, 1, CONV_W)
    cmk = cache_mem_k.reshape(DEPTH, bs, N_MEM * X_HEADS, X_HEAD_DIM)
    cmv = cache_mem_v.reshape(DEPTH, bs, N_MEM * X_HEADS, X_HEAD_DIM)
    pk, pv, pc, sk, sv, sc = [], [], [], [], [], []
    for l in range(DEPTH):
        final = l == DEPTH - 1
        layer = jnp.full((1,), l, jnp.int32)
        xp, kl, vl, ul, xs, ko, vo, uo = _layer(
            layer, xp, xs, g, w_in, w_pa, w_pb, w_px, w_out, conv_w, cb, fg, bias_p, bd_p, mkb, mvx,
            bias_s, bd_s, ck, cv, cache_conv, cmk, cmv, nt=nt, final=final)
        pk.append(kl)
        pv.append(vl)
        pc.append(ul)
        sk.append(ko)
        sv.append(vo)
        sc.append(uo)

    kv_p = (DEPTH, bp, WINDOW, N_KV_HEADS, HEAD_DIM)
    kv_s = (DEPTH, bs, nt, N_KV_HEADS, HEAD_DIM)
    mem_shape = (DEPTH, bp, N_MEM, X_HEADS, X_HEAD_DIM)
    return (xp, xs.reshape(bs, nt, D_MODEL),
            jnp.stack(pk).reshape(kv_p), jnp.stack(pv).reshape(kv_p), jnp.stack(pc),
            mk.reshape(mem_shape), mv.reshape(mem_shape),
            jnp.stack(sk).reshape(kv_s), jnp.stack(sv).reshape(kv_s), jnp.stack(sc))
```

```python
import functools

import numpy as np
import jax
import jax.numpy as jnp
from jax import lax
from jax.experimental import pallas as pl
from jax.experimental.pallas import tpu as pltpu

D_MODEL = 1024
DEPTH = 4
CHUNK = 64
WINDOW = 128
N_HEADS = 8
N_KV_HEADS = 2
GROUP = N_HEADS // N_KV_HEADS
HEAD_DIM = 64
ATT_W = N_HEADS * HEAD_DIM
KV_W = N_KV_HEADS * HEAD_DIM
CONV_W = 512
CONV_K = 3
X_HEADS = 4
X_HEAD_DIM = 128
X_W = X_HEADS * X_HEAD_DIM
N_MEM = 256
EPS = 1e-6

OFF_Q = 0
OFF_K = OFF_Q + ATT_W
OFF_V = OFF_K + KV_W
OFF_GA = OFF_V + KV_W
OFF_BB = OFF_GA + ATT_W
OFF_CC = OFF_BB + CONV_W
OFF_HH = OFF_CC + CONV_W
OFF_GB = OFF_HH + CONV_W
OFF_XQ = OFF_GB + CONV_W
OFF_GX = OFF_XQ + X_W
OFF_MG = OFF_GX + X_W
IN_W = OFF_MG + 3 * D_MODEL
W_IN_WIDTH = {OFF_Q: ATT_W, OFF_K: 2 * KV_W, OFF_GA: ATT_W, OFF_BB: CONV_W, OFF_CC: CONV_W, OFF_HH: CONV_W,
              OFF_GB: CONV_W, OFF_XQ: X_W, OFF_GX: X_W,
              OFF_MG: D_MODEL, OFF_MG + D_MODEL: D_MODEL, OFF_MG + 2 * D_MODEL: D_MODEL}

LANES = 128
GQ_W = GROUP * HEAD_DIM
BAND = WINDOW + CHUNK
KEY_SLOTS = 256
CONV_PAD = 8
PROMPT_TILE = 512
DECODE_PARTS = 2
Q_ROWS = 128
CAST_STEPS = 1
CAST_CHUNKS = 8
CAST_COLS = 256
VMEM_LIMIT = 61 * 1024 * 1024

F32 = jnp.float32
BF16 = jnp.bfloat16
_NT = (((1,), (1,)), ((), ()))


def _rms_scale(x):
    return x * lax.rsqrt(jnp.mean(x * x, axis=-1, keepdims=True) + EPS)


def _silu(x):
    return x * jax.nn.sigmoid(x)


def _dot(a, b):
    return jnp.dot(a, b, preferred_element_type=F32)


def _w_in(win_ref, off):
    return win_ref[:, off:off + W_IN_WIDTH[off]]


def _dup_heads(a):
    lane = lax.broadcasted_iota(jnp.int32, a.shape, 1)
    sw = pltpu.roll(a, HEAD_DIM, 1)
    lo = lane < HEAD_DIM
    return jnp.where(lo, a, sw), jnp.where(lo, sw, a)


def _key_tails(nk):
    shape = (KEY_SLOTS - nk, 2 * LANES)
    row = lax.broadcasted_iota(jnp.int32, shape, 0)
    col = lax.broadcasted_iota(jnp.int32, shape, 1)
    vtail = jnp.where((row == 0) & (col >= LANES), 1.0, 0.0).astype(BF16)
    return jnp.zeros(shape, BF16), vtail


def _band_attention(qbd, kb, vb, bias):
    s = lax.dot_general(qbd, kb, _NT, preferred_element_type=F32) + bias
    p = jnp.exp(s - jnp.max(s, axis=-1, keepdims=True)).astype(BF16)
    oe = _dot(p, vb)
    return oe[:, 0:LANES] / oe[:, LANES:2 * LANES]


def _ungroup(on, nq):
    lo = lax.broadcasted_iota(jnp.int32, (nq, LANES), 1) < HEAD_DIM
    return (jnp.where(lo, on[0:nq], on[nq:2 * nq]),
            jnp.where(lo, on[2 * nq:3 * nq], on[3 * nq:4 * nq]))


def _cross_head(xq_h, mk_h, mvx_h):
    s = lax.dot_general(xq_h, mk_h, _NT, preferred_element_type=F32)
    p = jnp.exp(s - jnp.max(s, axis=-1, keepdims=True)).astype(BF16)
    oe = _dot(p, mvx_h)
    return oe[:, 0:X_HEAD_DIM] / oe[:, X_HEAD_DIM:2 * X_HEAD_DIM]


def _memkv_kernel(mem_ref, g_ref, wk_ref, wv_ref, mk_ref, mv_ref, mkb_ref, mvx_ref):
    nb = mem_ref.shape[0]
    mn = (_rms_scale(mem_ref[...].reshape(nb * N_MEM, D_MODEL)) * g_ref[0]).astype(BF16)
    mk = _dot(mn, wk_ref[0].astype(BF16))
    mv = _dot(mn, wv_ref[0].astype(BF16))
    ones = jnp.ones((N_MEM, X_HEAD_DIM), BF16)
    for b in range(nb):
        rows = slice(b * N_MEM, (b + 1) * N_MEM)
        mkb_ref[0, b] = mk[rows].astype(BF16)
        for h in range(X_HEADS):
            cols = slice(h * X_HEAD_DIM, (h + 1) * X_HEAD_DIM)
            mk_ref[0, b, pl.ds(h, N_MEM, stride=X_HEADS), :] = mk[rows, cols]
            mv_ref[0, b, pl.ds(h, N_MEM, stride=X_HEADS), :] = mv[rows, cols]
            mvx_ref[0, b, h, :, 0:X_HEAD_DIM] = mv[rows, cols].astype(BF16)
            mvx_ref[0, b, h, :, X_HEAD_DIM:2 * X_HEAD_DIM] = ones


def _memory_kv(mem, g, wk, wv):
    nb = mem.shape[0]
    return pl.pallas_call(
        _memkv_kernel,
        grid=(DEPTH,),
        in_specs=[
            pl.BlockSpec((nb, N_MEM, D_MODEL), lambda l: (0, 0, 0)),
            pl.BlockSpec((1, 1, D_MODEL), lambda l: (l, 0, 0)),
            pl.BlockSpec((1, D_MODEL, X_W), lambda l: (l, 0, 0)),
            pl.BlockSpec((1, D_MODEL, X_W), lambda l: (l, 0, 0)),
        ],
        out_specs=[
            pl.BlockSpec((1, nb, N_MEM * X_HEADS, X_HEAD_DIM), lambda l: (l, 0, 0, 0)),
            pl.BlockSpec((1, nb, N_MEM * X_HEADS, X_HEAD_DIM), lambda l: (l, 0, 0, 0)),
            pl.BlockSpec((1, nb, N_MEM, X_W), lambda l: (l, 0, 0, 0)),
            pl.BlockSpec((1, nb, X_HEADS, N_MEM, 2 * X_HEAD_DIM), lambda l: (l, 0, 0, 0, 0)),
        ],
        out_shape=[
            jax.ShapeDtypeStruct((DEPTH, nb, N_MEM * X_HEADS, X_HEAD_DIM), F32),
            jax.ShapeDtypeStruct((DEPTH, nb, N_MEM * X_HEADS, X_HEAD_DIM), F32),
            jax.ShapeDtypeStruct((DEPTH, nb, N_MEM, X_W), BF16),
            jax.ShapeDtypeStruct((DEPTH, nb, X_HEADS, N_MEM, 2 * X_HEAD_DIM), BF16),
        ],
        name="memory_kv",
    )(mem, g.reshape(DEPTH, 1, D_MODEL), wk, wv)


def _prompt_tile(t, x_ref, g_ref, win_ref, wpa_ref, wpb_ref, wpx_ref, wout_ref, cw_ref, cb_ref,
                 bias_ref, bd_ref, mk_ref, mvx_ref, fg_ref,
                 xo_ref, kl_ref, vl_ref, ul_ref,
                 kr_ref, vv_ref, uc_ref, q_ref, attn_ref, *, tile, final):
    @pl.when(t == 0)
    def _():
        kr_ref[:, 0:WINDOW, :] = jnp.zeros((N_KV_HEADS, WINDOW, GQ_W), BF16)
        vv_ref[:, 0:WINDOW, 0:LANES] = jnp.zeros((N_KV_HEADS, WINDOW, LANES), BF16)
        vv_ref[:, :, LANES:2 * LANES] = jnp.ones((N_KV_HEADS, WINDOW + tile, LANES), BF16)
        uc_ref[0:CONV_PAD, :] = jnp.zeros((CONV_PAD, CONV_W), F32)

    x = x_ref[0]
    xn = (_rms_scale(x) * g_ref[...]).astype(BF16)

    def proj(off):
        return _dot(xn, _w_in(win_ref, off))

    w_q = _w_in(win_ref, OFF_Q)
    for r in range(0, tile, Q_ROWS):
        q_ref[r:r + Q_ROWS, :] = (_dot(xn[r:r + Q_ROWS], w_q) * (HEAD_DIM ** -0.5)).astype(BF16)
    half = tile // 2
    w_kv = _w_in(win_ref, OFF_K)
    kv = jnp.concatenate([_dot(xn[0:half], w_kv), _dot(xn[half:tile], w_kv)], axis=0)
    k = kv[:, 0:KV_W]
    v = kv[:, KV_W:2 * KV_W]
    ga = proj(OFF_GA)
    kl_ref[0] = k[tile - WINDOW:tile]
    vl_ref[0] = v[tile - WINDOW:tile]
    for j, (kj, vj) in enumerate(zip(_dup_heads(k), _dup_heads(v))):
        kj = kj.astype(BF16)
        kr_ref[j, WINDOW:WINDOW + tile, 0:LANES] = kj
        kr_ref[j, WINDOW:WINDOW + tile, LANES:2 * LANES] = kj
        vv_ref[j, WINDOW:WINDOW + tile, 0:LANES] = vj.astype(BF16)

    fillers = [functools.partial(proj, off) for off in (OFF_BB, OFF_CC, OFF_HH, OFF_GB, OFF_XQ, OFF_GX)]
    fillers += [lambda i=i: jax.nn.sigmoid(proj(OFF_MG + i * D_MODEL)) for i in (1, 2)]
    filled = []
    n_chunks = tile // CHUNK
    ktail, vtail = _key_tails(BAND)

    for c in range(n_chunks):
        r0 = c * CHUNK
        qc = q_ref[r0:r0 + CHUNK, :]
        variant = jnp.where(t == 0, c, 2) if c < 2 else 2
        for j in range(N_KV_HEADS):
            qg = qc[:, j * GQ_W:(j + 1) * GQ_W]
            qbd = jnp.concatenate([qg] * GROUP, axis=0) * bd_ref[...]
            on = _band_attention(qbd, jnp.concatenate([kr_ref[j, r0:r0 + BAND, :], ktail], axis=0),
                                 jnp.concatenate([vv_ref[j, r0:r0 + BAND, :], vtail], axis=0),
                                 bias_ref[variant, j])
            lo, hi = _ungroup(on, CHUNK)
            attn_ref[r0:r0 + CHUNK, j * GQ_W:j * GQ_W + LANES] = lo
            attn_ref[r0:r0 + CHUNK, j * GQ_W + LANES:(j + 1) * GQ_W] = hi
        while len(filled) * n_chunks < (c + 1) * len(fillers):
            filled.append(fillers[len(filled)]())
    bb, cc, hh, gb, xq, gx, gate_b, gate_x = filled
    kr_ref[:, 0:WINDOW, :] = kr_ref[:, tile:tile + WINDOW, :]
    vv_ref[:, 0:WINDOW, 0:LANES] = vv_ref[:, tile:tile + WINDOW, 0:LANES]

    u = cc * hh
    uc_ref[CONV_PAD:CONV_PAD + tile, :] = u
    ul_ref[0] = u[tile - (CONV_K - 1):tile]
    conv = (uc_ref[CONV_PAD - 2:CONV_PAD - 2 + tile, :] * cw_ref[0:1, :]
            + uc_ref[CONV_PAD - 1:CONV_PAD - 1 + tile, :] * cw_ref[1:2, :]
            + u * cw_ref[2:3, :]) + cb_ref[...]
    uc_ref[0:CONV_PAD, :] = uc_ref[tile:tile + CONV_PAD, :]
    zb = (bb * conv * _silu(gb)).astype(BF16)
    za = (attn_ref[...] * _silu(ga)).astype(BF16)

    xq = (xq * (X_HEAD_DIM ** -0.5)).astype(BF16)
    merge_work = [lambda: _dot(za, wpa_ref[...]),
                  lambda: jax.nn.sigmoid(proj(OFF_MG)),
                  lambda: _dot(zb, wpb_ref[...])]
    merged = []
    heads = []
    for h in range(X_HEADS):
        heads.append(_cross_head(xq[:, h * X_HEAD_DIM:(h + 1) * X_HEAD_DIM],
                                 mk_ref[0, :, h * X_HEAD_DIM:(h + 1) * X_HEAD_DIM], mvx_ref[0, h]))
        if h < len(merge_work):
            merged.append(merge_work[h]())
    ya, gate_a, yb = merged
    zx = (jnp.concatenate(heads, axis=1) * _silu(gx)).astype(BF16)
    acc = gate_a * ya + gate_b * yb + gate_x * _dot(zx, wpx_ref[...])
    out = x + _dot(acc.astype(BF16), wout_ref[...])
    if final:
        out = _rms_scale(out) * fg_ref[...]
    xo_ref[0] = out


def _const_spec(shape):
    nd = len(shape)
    return pl.BlockSpec(shape, lambda *_: (0,) * nd, pipeline_mode=pl.Buffered(1))


def _layer_spec(shape):
    nd = len(shape)
    return pl.BlockSpec((None,) + tuple(shape), lambda *a: (a[-1][0],) + (0,) * nd, pipeline_mode=pl.Buffered(1))


def _sample_step(x_ref, g_ref, win_ref, wpa_ref, wpb_ref, wpx_ref, wout_ref, cw_ref, cb_ref,
                 bias_ref, bd_ref, ck_ref, cv_ref, cc_ref, cmk_hbm, cmv_hbm, fg_ref,
                 xo_ref, ko_ref, vo_ref, uo_ref,
                 uc_ref, attn_ref, conv_ref, xat_ref, cmem_ref, cmem_sem, *, layer, nb, nt, final):
    nbp = nb // DECODE_PARTS
    rows_p = nbp * nt

    def part(pi, carry):
        b0 = pi * nbp
        part_rows = pl.ds(pl.multiple_of(pi * rows_p, rows_p), rows_p)
        mem_copies = [pltpu.make_async_copy(src.at[layer, pl.ds(b0, nbp)], cmem_ref.at[n], cmem_sem.at[n])
                      for n, src in enumerate((cmk_hbm, cmv_hbm))]
        for copy in mem_copies:
            copy.start()
        x = x_ref[part_rows, :]
        xn = (_rms_scale(x) * g_ref[...]).astype(BF16)

        def proj(off):
            return _dot(xn, _w_in(win_ref, off))

        q = (proj(OFF_Q) * (HEAD_DIM ** -0.5)).astype(BF16)
        kv = proj(OFF_K)
        k = kv[:, 0:KV_W]
        v = kv[:, KV_W:2 * KV_W]
        ga = proj(OFF_GA)
        ko_ref[part_rows, :] = k
        vo_ref[part_rows, :] = v
        ones = jnp.ones((WINDOW + nt, LANES), BF16)
        ktail, vtail = _key_tails(WINDOW + nt)
        fillers = [functools.partial(proj, off) for off in (OFF_BB, OFF_CC, OFF_HH, OFF_GB, OFF_XQ, OFF_GX)]
        filled = []
        for b in range(nbp):
            rows = slice(b * nt, (b + 1) * nt)
            kall = _dup_heads(jnp.concatenate([ck_ref[b0 + b], k[rows]], axis=0))
            vall = _dup_heads(jnp.concatenate([cv_ref[b0 + b], v[rows]], axis=0))
            for j in range(N_KV_HEADS):
                kj = kall[j].astype(BF16)
                vj = vall[j].astype(BF16)
                qg = q[rows, j * GQ_W:(j + 1) * GQ_W]
                qbd = jnp.concatenate([qg] * GROUP, axis=0) * bd_ref[...]
                on = _band_attention(qbd, jnp.concatenate([jnp.concatenate([kj, kj], axis=1), ktail], axis=0),
                                     jnp.concatenate([jnp.concatenate([vj, ones], axis=1), vtail], axis=0),
                                     bias_ref[j])
                lo, hi = _ungroup(on, nt)
                attn_ref[rows, j * GQ_W:j * GQ_W + LANES] = lo
                attn_ref[rows, j * GQ_W + LANES:(j + 1) * GQ_W] = hi
            while len(filled) * nbp < (b + 1) * len(fillers):
                filled.append(fillers[len(filled)]())
        bb, cc, hh, gb, p3q, gx = filled
        za = attn_ref[...] * _silu(ga)
        for copy in mem_copies:
            copy.wait()

        u = cc * hh
        for b in range(nbp):
            rows = slice(b * nt, (b + 1) * nt)
            ub = u[rows]
            uc_ref[b, CONV_PAD - (CONV_K - 1):CONV_PAD, :] = cc_ref[b0 + b]
            uc_ref[b, CONV_PAD:CONV_PAD + nt, :] = ub
            uo_ref[b0 + b] = ub[nt - (CONV_K - 1):nt]
            conv_ref[rows, :] = (uc_ref[b, CONV_PAD - 2:CONV_PAD - 2 + nt, :] * cw_ref[0:1, :]
                                 + uc_ref[b, CONV_PAD - 1:CONV_PAD - 1 + nt, :] * cw_ref[1:2, :]
                                 + ub * cw_ref[2:3, :]) + cb_ref[...]
        zb = bb * conv_ref[...] * _silu(gb)

        xq = (p3q * (X_HEAD_DIM ** -0.5)).astype(BF16)
        mones = jnp.ones((N_MEM, X_HEAD_DIM), BF16)
        merge_work = [lambda: _dot(za.astype(BF16), wpa_ref[...]),
                      lambda: jax.nn.sigmoid(proj(OFF_MG)),
                      lambda: _dot(zb.astype(BF16), wpb_ref[...]),
                      lambda: jax.nn.sigmoid(proj(OFF_MG + D_MODEL)),
                      lambda: jax.nn.sigmoid(proj(OFF_MG + 2 * D_MODEL))]
        merged = []
        for b in range(nbp):
            rows = slice(b * nt, (b + 1) * nt)
            for h in range(X_HEADS):
                cols = slice(h * X_HEAD_DIM, (h + 1) * X_HEAD_DIM)
                head_rows = pl.ds(h, N_MEM, stride=X_HEADS)
                mvx = jnp.concatenate([cmem_ref[1, b, head_rows, :].astype(BF16), mones], axis=1)
                xat_ref[rows, cols] = _cross_head(xq[rows, cols], cmem_ref[0, b, head_rows, :].astype(BF16), mvx)
            while len(merged) * nbp < (b + 1) * len(merge_work):
                merged.append(merge_work[len(merged)]())
        ya, gate_a, yb, gate_b, gate_x = merged
        zx = xat_ref[...] * _silu(gx)

        acc = gate_a * ya + gate_b * yb + gate_x * _dot(zx.astype(BF16), wpx_ref[...])
        out = x + _dot(acc.astype(BF16), wout_ref[...])
        if final:
            out = _rms_scale(out) * fg_ref[...]
        xo_ref[part_rows, :] = out
        return carry

    lax.fori_loop(0, DECODE_PARTS, part, 0)


def _layer_kernel(layer_ref, x_ref, g_ref, win_hbm, wpa_hbm, wpb_hbm, wpx_hbm, wout_hbm,
                  cw_ref, cb_ref, fg_ref, bias_ref, bd_ref, mk_ref, mvx_ref,
                  xs_ref, sbias_ref, sbd_ref, ck_ref, cv_ref, cc_ref, cmk_hbm, cmv_hbm,
                  xo_ref, kl_ref, vl_ref, ul_ref, xso_ref, ko_ref, vo_ref, uo_ref,
                  win_ref, wpa_ref, wpb_ref, wpx_ref, wout_ref,
                  kr_ref, vv_ref, uc_ref, q_ref, attn_ref, suc_ref, sattn_ref, sconv_ref, sxat_ref, cmem_ref, cmem_sem,
                  sin_ref, spa_ref, spb_ref, spx_ref, sout_ref, wsem,
                  *, tile, tiles_per_seq, nb, nt, final):
    i = pl.program_id(0)
    weights = (win_ref, wpa_ref, wpb_ref, wpx_ref, wout_ref)

    @pl.when(i < CAST_STEPS)
    def _():
        layer = layer_ref[0]
        sources = (win_hbm, wpa_hbm, wpb_hbm, wpx_hbm, wout_hbm)
        stages = (sin_ref, spa_ref, spb_ref, spx_ref, sout_ref)

        def chunk_copies(c, slot):
            return [pltpu.make_async_copy(src.at[layer, pl.ds(c * stage.shape[1], stage.shape[1])],
                                          stage.at[slot], wsem.at[slot, n])
                    for n, (src, stage) in enumerate(zip(sources, stages))]

        for copy in chunk_copies(0, 0):
            copy.start()

        def cast_chunk(c, carry):
            slot = lax.rem(c, 2)

            @pl.when(c + 1 < CAST_CHUNKS)
            def _():
                for copy in chunk_copies(c + 1, 1 - slot):
                    copy.start()

            for copy in chunk_copies(c, slot):
                copy.wait()

            def chunk_rows(stage):
                rows = stage.shape[1]
                return pl.ds(pl.multiple_of(c * rows, rows), rows)

            def cast_cols(t, inner):
                cols = pl.ds(pl.multiple_of(t * CAST_COLS, CAST_COLS), CAST_COLS)
                win_ref[chunk_rows(sin_ref), cols] = sin_ref[slot, :, cols].astype(BF16)
                return inner

            lax.fori_loop(0, IN_W // CAST_COLS, cast_cols, 0)
            for stage, dst in zip(stages[1:], weights[1:]):
                dst[chunk_rows(stage), :] = stage[slot].astype(BF16)
            return carry

        lax.fori_loop(0, CAST_CHUNKS, cast_chunk, 0)

    @pl.when(i == CAST_STEPS)
    def _():
        _sample_step(xs_ref, g_ref, *weights, cw_ref, cb_ref,
                     sbias_ref, sbd_ref, ck_ref, cv_ref, cc_ref, cmk_hbm, cmv_hbm, fg_ref,
                     xso_ref, ko_ref, vo_ref, uo_ref, suc_ref, sattn_ref, sconv_ref, sxat_ref, cmem_ref, cmem_sem,
                     layer=layer_ref[0], nb=nb, nt=nt, final=final)

    @pl.when(i > CAST_STEPS)
    def _():
        _prompt_tile(lax.rem(i - (CAST_STEPS + 1), tiles_per_seq), x_ref, g_ref, *weights, cw_ref, cb_ref,
                     bias_ref, bd_ref, mk_ref, mvx_ref, fg_ref,
                     xo_ref, kl_ref, vl_ref, ul_ref, kr_ref, vv_ref, uc_ref, q_ref, attn_ref,
                     tile=tile, final=final)


_WEIGHT_SHAPES = ((D_MODEL, IN_W), (ATT_W, D_MODEL), (CONV_W, D_MODEL), (X_W, D_MODEL), (D_MODEL, D_MODEL))


def _layer(layer, x, xs, g, win, wpa, wpb, wpx, wout, cw, cb, fg, bias, bd, mkb, mvx,
           sbias, sbd, ck, cv, cc, cmk, cmv, *, nt, final):
    nb, seq, _ = x.shape
    rows = xs.shape[0]
    nbs = rows // nt
    tile = PROMPT_TILE
    assert seq % tile == 0 and tile % WINDOW == 0
    tps = seq // tile
    kern = functools.partial(_layer_kernel, tile=tile, tiles_per_seq=tps, nb=nbs, nt=nt, final=final)

    def prompt_step(i):
        return jnp.maximum(i - (CAST_STEPS + 1), 0)

    def seq_of(i):
        return prompt_step(i) // tps

    def tile_of(i):
        return prompt_step(i) % tps

    grid_spec = pltpu.PrefetchScalarGridSpec(
        num_scalar_prefetch=1,
        grid=(CAST_STEPS + 1 + nb * tps,),
        in_specs=[
            pl.BlockSpec((1, tile, D_MODEL), lambda i, l: (seq_of(i), tile_of(i), 0)),
            _layer_spec((1, D_MODEL)),
            *[pl.BlockSpec(memory_space=pl.ANY) for _ in _WEIGHT_SHAPES],
            _layer_spec((CONV_K, CONV_W)),
            _layer_spec((1, CONV_W)),
            _const_spec((1, D_MODEL)),
            _layer_spec((3, N_KV_HEADS, GROUP * CHUNK, KEY_SLOTS)),
            _const_spec((GROUP * CHUNK, GQ_W)),
            pl.BlockSpec((None, 1, N_MEM, X_W), lambda i, l: (l[0], seq_of(i), 0, 0)),
            pl.BlockSpec((None, 1, X_HEADS, N_MEM, 2 * X_HEAD_DIM), lambda i, l: (l[0], seq_of(i), 0, 0, 0)),
            _const_spec((rows, D_MODEL)),
            _layer_spec((N_KV_HEADS, GROUP * nt, KEY_SLOTS)),
            _const_spec((GROUP * nt, GQ_W)),
            _layer_spec((nbs, WINDOW, KV_W)),
            _layer_spec((nbs, WINDOW, KV_W)),
            _layer_spec((nbs, CONV_K - 1, CONV_W)),
            pl.BlockSpec(memory_space=pl.ANY),
            pl.BlockSpec(memory_space=pl.ANY),
        ],
        out_specs=[
            pl.BlockSpec((1, tile, D_MODEL), lambda i, l: (seq_of(i), tile_of(i), 0)),
            pl.BlockSpec((1, WINDOW, KV_W), lambda i, l: (seq_of(i), 0, 0)),
            pl.BlockSpec((1, WINDOW, KV_W), lambda i, l: (seq_of(i), 0, 0)),
            pl.BlockSpec((1, CONV_K - 1, CONV_W), lambda i, l: (seq_of(i), 0, 0)),
            pl.BlockSpec((rows, D_MODEL), lambda i, l: (0, 0)),
            pl.BlockSpec((rows, KV_W), lambda i, l: (0, 0)),
            pl.BlockSpec((rows, KV_W), lambda i, l: (0, 0)),
            pl.BlockSpec((nbs, CONV_K - 1, CONV_W), lambda i, l: (0, 0, 0)),
        ],
        scratch_shapes=[
            *[pltpu.VMEM(shape, BF16) for shape in _WEIGHT_SHAPES],
            pltpu.VMEM((N_KV_HEADS, WINDOW + tile, GQ_W), BF16),
            pltpu.VMEM((N_KV_HEADS, WINDOW + tile, 2 * LANES), BF16),
            pltpu.VMEM((CONV_PAD + tile, CONV_W), F32),
            pltpu.VMEM((tile, ATT_W), BF16),
            pltpu.VMEM((tile, ATT_W), F32),
            pltpu.VMEM((nbs // DECODE_PARTS, CONV_PAD + nt, CONV_W), F32),
            pltpu.VMEM((rows // DECODE_PARTS, ATT_W), F32),
            pltpu.VMEM((rows // DECODE_PARTS, CONV_W), F32),
            pltpu.VMEM((rows // DECODE_PARTS, X_W), F32),
            pltpu.VMEM((2, nbs // DECODE_PARTS, N_MEM * X_HEADS, X_HEAD_DIM), F32),
            pltpu.SemaphoreType.DMA((2,)),
            *[pltpu.VMEM((2, r // CAST_CHUNKS, c), F32) for r, c in _WEIGHT_SHAPES],
            pltpu.SemaphoreType.DMA((2, len(_WEIGHT_SHAPES))),
        ],
    )
    return pl.pallas_call(
        kern,
        grid_spec=grid_spec,
        out_shape=[
            jax.ShapeDtypeStruct((nb, seq, D_MODEL), F32),
            jax.ShapeDtypeStruct((nb, WINDOW, KV_W), F32),
            jax.ShapeDtypeStruct((nb, WINDOW, KV_W), F32),
            jax.ShapeDtypeStruct((nb, CONV_K - 1, CONV_W), F32),
            jax.ShapeDtypeStruct((rows, D_MODEL), F32),
            jax.ShapeDtypeStruct((rows, KV_W), F32),
            jax.ShapeDtypeStruct((rows, KV_W), F32),
            jax.ShapeDtypeStruct((nbs, CONV_K - 1, CONV_W), F32),
        ],
        compiler_params=pltpu.CompilerParams(
            dimension_semantics=("arbitrary",), vmem_limit_bytes=VMEM_LIMIT),
        name="layer",
    )(layer, x, g, win, wpa, wpb, wpx, wout, cw, cb, fg, bias, bd, mkb, mvx,
      xs, sbias, sbd, ck, cv, cc, cmk, cmv)


def _alibi_slopes():
    return np.power(2.0, -8.0 * np.arange(1, N_HEADS + 1, dtype=np.float32) / N_HEADS).astype(np.float32)


def _alibi_bias(nq, nk, masked):
    dist = np.abs(np.arange(nq)[:, None] + WINDOW - np.arange(nk)[None, :]).astype(np.float32)
    slope = _alibi_slopes().reshape(N_KV_HEADS, GROUP)
    bias = -(slope[:, :, None, None] * dist[None, None])
    bias = np.where(np.arange(nk)[None, None, None, :] < masked, -np.inf, bias)
    return bias.reshape(N_KV_HEADS, GROUP * nq, nk).astype(np.float32)


def _bias_with_sink(alibi, attn_sink, nq):
    nk = alibi.shape[-1]
    lead = alibi.shape[:-3]
    sink = jnp.repeat(attn_sink.reshape(DEPTH, N_KV_HEADS, GROUP), nq, axis=2)[..., None]
    sink = jnp.broadcast_to(sink.reshape((DEPTH,) + (1,) * len(lead) + sink.shape[1:]),
                            (DEPTH,) + alibi.shape[:-1] + (1,))
    real = jnp.broadcast_to(jnp.asarray(alibi), (DEPTH,) + alibi.shape)
    unused = jnp.full((DEPTH,) + alibi.shape[:-1] + (KEY_SLOTS - nk - 1,), -jnp.inf, F32)
    return jnp.concatenate([real, sink, unused], axis=-1)


def _block_diag_mask(nq):
    rows = np.arange(GROUP * nq)[:, None] // nq
    cols = np.arange(GQ_W)[None, :] // HEAD_DIM
    return (rows == cols).astype(np.float32)


def kernel(x_prompt, x_sample, cache_attn_k, cache_attn_v, cache_conv, cache_mem_k, cache_mem_v,
           mem_prompt, norm_g, w_in, attn_sink, w_pa, conv_w, conv_b, w_pb, mem_norm_g,
           w_mk, w_mv, w_px, w_out, final_g):
    bp, seq, _ = x_prompt.shape
    bs, nt, _ = x_sample.shape
    win_len = cache_attn_k.shape[2]
    assert win_len == WINDOW and nt <= CHUNK and seq >= WINDOW

    bias_p = _bias_with_sink(np.stack([_alibi_bias(CHUNK, BAND, m) for m in (WINDOW, CHUNK, 0)]), attn_sink, CHUNK)
    bias_s = _bias_with_sink(_alibi_bias(nt, WINDOW + nt, 0), attn_sink, nt)
    bd_p = jnp.asarray(_block_diag_mask(CHUNK), BF16)
    bd_s = jnp.asarray(_block_diag_mask(nt), BF16)
    fg = final_g.reshape(1, D_MODEL)

    mk, mv, mkb, mvx = _memory_kv(mem_prompt, mem_norm_g, w_mk, w_mv)

    xp = x_prompt
    xs = x_sample.reshape(bs * nt, D_MODEL)
    ck = cache_attn_k.reshape(DEPTH, bs, WINDOW, KV_W)
    cv = cache_attn_v.reshape(DEPTH, bs, WINDOW, KV_W)
    g = norm_g.reshape(DEPTH, 1, D_MODEL)
    cb = conv_b.reshape(DEPTH, 1, CONV_W)
    cmk = cache_mem_k.reshape(DEPTH, bs, N_MEM * X_HEADS, X_HEAD_DIM)
    cmv = cache_mem_v.reshape(DEPTH, bs, N_MEM * X_HEADS, X_HEAD_DIM)
    pk, pv, pc, sk, sv, sc = [], [], [], [], [], []
    for l in range(DEPTH):
        final = l == DEPTH - 1
        layer = jnp.full((1,), l, jnp.int32)
        xp, kl, vl, ul, xs, ko, vo, uo = _layer(
            layer, xp, xs, g, w_in, w_pa, w_pb, w_px, w_out, conv_w, cb, fg, bias_p, bd_p, mkb, mvx,
            bias_s, bd_s, ck, cv, cache_conv, cmk, cmv, nt=nt, final=final)
        pk.append(kl)
        pv.append(vl)
        pc.append(ul)
        sk.append(ko)
        sv.append(vo)
        sc.append(uo)

    kv_p = (DEPTH, bp, WINDOW, N_KV_HEADS, HEAD_DIM)
    kv_s = (DEPTH, bs, nt, N_KV_HEADS, HEAD_DIM)
    mem_shape = (DEPTH, bp, N_MEM, X_HEADS, X_HEAD_DIM)
    return (xp, xs.reshape(bs, nt, D_MODEL),
            jnp.stack(pk).reshape(kv_p), jnp.stack(pv).reshape(kv_p), jnp.stack(pc),
            mk.reshape(mem_shape), mv.reshape(mem_shape),
            jnp.stack(sk).reshape(kv_s), jnp.stack(sv).reshape(kv_s), jnp.stack(sc))
```

```python
import functools

import numpy as np
import jax
import jax.numpy as jnp
from jax import lax
from jax.experimental import pallas as pl
from jax.experimental.pallas import tpu as pltpu

D_MODEL = 1024
DEPTH = 4
CHUNK = 64
WINDOW = 128
N_HEADS = 8
N_KV_HEADS = 2
GROUP = N_HEADS // N_KV_HEADS
HEAD_DIM = 64
ATT_W = N_HEADS * HEAD_DIM
KV_W = N_KV_HEADS * HEAD_DIM
CONV_W = 512
CONV_K = 3
X_HEADS = 4
X_HEAD_DIM = 128
X_W = X_HEADS * X_HEAD_DIM
N_MEM = 256
EPS = 1e-6

OFF_Q = 0
OFF_K = OFF_Q + ATT_W
OFF_V = OFF_K + KV_W
OFF_GA = OFF_V + KV_W
OFF_BB = OFF_GA + ATT_W
OFF_CC = OFF_BB + CONV_W
OFF_HH = OFF_CC + CONV_W
OFF_GB = OFF_HH + CONV_W
OFF_XQ = OFF_GB + CONV_W
OFF_GX = OFF_XQ + X_W
OFF_MG = OFF_GX + X_W
IN_W = OFF_MG + 3 * D_MODEL
W_IN_WIDTH = {OFF_Q: ATT_W, OFF_K: 2 * KV_W, OFF_GA: ATT_W, OFF_BB: CONV_W, OFF_CC: CONV_W, OFF_HH: CONV_W,
              OFF_GB: CONV_W, OFF_XQ: X_W, OFF_GX: X_W,
              OFF_MG: D_MODEL, OFF_MG + D_MODEL: D_MODEL, OFF_MG + 2 * D_MODEL: D_MODEL}

LANES = 128
GQ_W = GROUP * HEAD_DIM
BAND = WINDOW + CHUNK
KEY_SLOTS = 256
CONV_PAD = 8
PROMPT_TILE = 512
DECODE_PARTS = 2
Q_ROWS = 128
CAST_STEPS = 1
CAST_CHUNKS = 8
CAST_COLS = 256
VMEM_LIMIT = 61 * 1024 * 1024

F32 = jnp.float32
BF16 = jnp.bfloat16
_NT = (((1,), (1,)), ((), ()))


def _rms_scale(x):
    return x * lax.rsqrt(jnp.mean(x * x, axis=-1, keepdims=True) + EPS)


def _silu(x):
    return x * jax.nn.sigmoid(x)


def _dot(a, b):
    return jnp.dot(a, b, preferred_element_type=F32)


def _w_in(win_ref, off):
    return win_ref[:, off:off + W_IN_WIDTH[off]]


def _dup_heads(a):
    lane = lax.broadcasted_iota(jnp.int32, a.shape, 1)
    sw = pltpu.roll(a, HEAD_DIM, 1)
    lo = lane < HEAD_DIM
    return jnp.where(lo, a, sw), jnp.where(lo, sw, a)


def _key_tails(nk):
    shape = (KEY_SLOTS - nk, 2 * LANES)
    row = lax.broadcasted_iota(jnp.int32, shape, 0)
    col = lax.broadcasted_iota(jnp.int32, shape, 1)
    vtail = jnp.where((row == 0) & (col >= LANES), 1.0, 0.0).astype(BF16)
    return jnp.zeros(shape, BF16), vtail


def _band_attention(qbd, kb, vb, bias):
    s = lax.dot_general(qbd, kb, _NT, preferred_element_type=F32) + bias
    p = jnp.exp(s - jnp.max(s, axis=-1, keepdims=True)).astype(BF16)
    oe = _dot(p, vb)
    return oe[:, 0:LANES] / oe[:, LANES:2 * LANES]


def _ungroup(on, nq):
    lo = lax.broadcasted_iota(jnp.int32, (nq, LANES), 1) < HEAD_DIM
    return (jnp.where(lo, on[0:nq], on[nq:2 * nq]),
            jnp.where(lo, on[2 * nq:3 * nq], on[3 * nq:4 * nq]))


def _cross_head(xq_h, mk_h, mvx_h):
    s = lax.dot_general(xq_h, mk_h, _NT, preferred_element_type=F32)
    p = jnp.exp(s - jnp.max(s, axis=-1, keepdims=True)).astype(BF16)
    oe = _dot(p, mvx_h)
    return oe[:, 0:X_HEAD_DIM] / oe[:, X_HEAD_DIM:2 * X_HEAD_DIM]


def _memkv_kernel(mem_ref, g_ref, wk_ref, wv_ref, mk_ref, mv_ref, mkb_ref, mvx_ref):
    nb = mem_ref.shape[0]
    mn = (_rms_scale(mem_ref[...].reshape(nb * N_MEM, D_MODEL)) * g_ref[0]).astype(BF16)
    mk = _dot(mn, wk_ref[0].astype(BF16))
    mv = _dot(mn, wv_ref[0].astype(BF16))
    ones = jnp.ones((N_MEM, X_HEAD_DIM), BF16)
    for b in range(nb):
        rows = slice(b * N_MEM, (b + 1) * N_MEM)
        mkb_ref[0, b] = mk[rows].astype(BF16)
        for h in range(X_HEADS):
            cols = slice(h * X_HEAD_DIM, (h + 1) * X_HEAD_DIM)
            mk_ref[0, b, pl.ds(h, N_MEM, stride=X_HEADS), :] = mk[rows, cols]
            mv_ref[0, b, pl.ds(h, N_MEM, stride=X_HEADS), :] = mv[rows, cols]
            mvx_ref[0, b, h, :, 0:X_HEAD_DIM] = mv[rows, cols].astype(BF16)
            mvx_ref[0, b, h, :, X_HEAD_DIM:2 * X_HEAD_DIM] = ones


def _memory_kv(mem, g, wk, wv):
    nb = mem.shape[0]
    return pl.pallas_call(
        _memkv_kernel,
        grid=(DEPTH,),
        in_specs=[
            pl.BlockSpec((nb, N_MEM, D_MODEL), lambda l: (0, 0, 0)),
            pl.BlockSpec((1, 1, D_MODEL), lambda l: (l, 0, 0)),
            pl.BlockSpec((1, D_MODEL, X_W), lambda l: (l, 0, 0)),
            pl.BlockSpec((1, D_MODEL, X_W), lambda l: (l, 0, 0)),
        ],
        out_specs=[
            pl.BlockSpec((1, nb, N_MEM * X_HEADS, X_HEAD_DIM), lambda l: (l, 0, 0, 0)),
            pl.BlockSpec((1, nb, N_MEM * X_HEADS, X_HEAD_DIM), lambda l: (l, 0, 0, 0)),
            pl.BlockSpec((1, nb, N_MEM, X_W), lambda l: (l, 0, 0, 0)),
            pl.BlockSpec((1, nb, X_HEADS, N_MEM, 2 * X_HEAD_DIM), lambda l: (l, 0, 0, 0, 0)),
        ],
        out_shape=[
            jax.ShapeDtypeStruct((DEPTH, nb, N_MEM * X_HEADS, X_HEAD_DIM), F32),
            jax.ShapeDtypeStruct((DEPTH, nb, N_MEM * X_HEADS, X_HEAD_DIM), F32),
            jax.ShapeDtypeStruct((DEPTH, nb, N_MEM, X_W), BF16),
            jax.ShapeDtypeStruct((DEPTH, nb, X_HEADS, N_MEM, 2 * X_HEAD_DIM), BF16),
        ],
        name="memory_kv",
    )(mem, g.reshape(DEPTH, 1, D_MODEL), wk, wv)


def _prompt_tile(t, x_ref, g_ref, win_ref, wpa_ref, wpb_ref, wpx_ref, wout_ref, cw_ref, cb_ref,
                 bias_ref, bd_ref, mk_ref, mvx_ref, fg_ref,
                 xo_ref, kl_ref, vl_ref, ul_ref,
                 kr_ref, vv_ref, uc_ref, q_ref, attn_ref, *, tile, final):
    @pl.when(t == 0)
    def _():
        kr_ref[:, 0:WINDOW, :] = jnp.zeros((N_KV_HEADS, WINDOW, GQ_W), BF16)
        vv_ref[:, 0:WINDOW, 0:LANES] = jnp.zeros((N_KV_HEADS, WINDOW, LANES), BF16)
        vv_ref[:, :, LANES:2 * LANES] = jnp.ones((N_KV_HEADS, WINDOW + tile, LANES), BF16)
        uc_ref[0:CONV_PAD, :] = jnp.zeros((CONV_PAD, CONV_W), F32)

    x = x_ref[...]
    xn = (_rms_scale(x) * g_ref[...]).astype(BF16)

    def proj(off):
        return _dot(xn, _w_in(win_ref, off))

    w_q = _w_in(win_ref, OFF_Q)
    for r in range(0, tile, Q_ROWS):
        q_ref[r:r + Q_ROWS, :] = (_dot(xn[r:r + Q_ROWS], w_q) * (HEAD_DIM ** -0.5)).astype(BF16)
    half = tile // 2
    w_kv = _w_in(win_ref, OFF_K)
    kv = jnp.concatenate([_dot(xn[0:half], w_kv), _dot(xn[half:tile], w_kv)], axis=0)
    k = kv[:, 0:KV_W]
    v = kv[:, KV_W:2 * KV_W]
    ga = proj(OFF_GA)
    kl_ref[0] = k[tile - WINDOW:tile]
    vl_ref[0] = v[tile - WINDOW:tile]
    for j, (kj, vj) in enumerate(zip(_dup_heads(k), _dup_heads(v))):
        kj = kj.astype(BF16)
        kr_ref[j, WINDOW:WINDOW + tile, 0:LANES] = kj
        kr_ref[j, WINDOW:WINDOW + tile, LANES:2 * LANES] = kj
        vv_ref[j, WINDOW:WINDOW + tile, 0:LANES] = vj.astype(BF16)

    fillers = [functools.partial(proj, off) for off in (OFF_BB, OFF_CC, OFF_HH, OFF_GB, OFF_XQ, OFF_GX)]
    fillers += [lambda i=i: jax.nn.sigmoid(proj(OFF_MG + i * D_MODEL)) for i in (1, 2)]
    filled = []
    n_chunks = tile // CHUNK
    ktail, vtail = _key_tails(BAND)

    for c in range(n_chunks):
        r0 = c * CHUNK
        qc = q_ref[r0:r0 + CHUNK, :]
        variant = jnp.where(t == 0, c, 2) if c < 2 else 2
        for j in range(N_KV_HEADS):
            qg = qc[:, j * GQ_W:(j + 1) * GQ_W]
            qbd = jnp.concatenate([qg] * GROUP, axis=0) * bd_ref[...]
            on = _band_attention(qbd, jnp.concatenate([kr_ref[j, r0:r0 + BAND, :], ktail], axis=0),
                                 jnp.concatenate([vv_ref[j, r0:r0 + BAND, :], vtail], axis=0),
                                 bias_ref[variant, j])
            lo, hi = _ungroup(on, CHUNK)
            attn_ref[r0:r0 + CHUNK, j * GQ_W:j * GQ_W + LANES] = lo
            attn_ref[r0:r0 + CHUNK, j * GQ_W + LANES:(j + 1) * GQ_W] = hi
        while len(filled) * n_chunks < (c + 1) * len(fillers):
            filled.append(fillers[len(filled)]())
    bb, cc, hh, gb, xq, gx, gate_b, gate_x = filled
    kr_ref[:, 0:WINDOW, :] = kr_ref[:, tile:tile + WINDOW, :]
    vv_ref[:, 0:WINDOW, 0:LANES] = vv_ref[:, tile:tile + WINDOW, 0:LANES]

    u = cc * hh
    uc_ref[CONV_PAD:CONV_PAD + tile, :] = u
    ul_ref[0] = u[tile - (CONV_K - 1):tile]
    conv = (uc_ref[CONV_PAD - 2:CONV_PAD - 2 + tile, :] * cw_ref[0:1, :]
            + uc_ref[CONV_PAD - 1:CONV_PAD - 1 + tile, :] * cw_ref[1:2, :]
            + u * cw_ref[2:3, :]) + cb_ref[...]
    uc_ref[0:CONV_PAD, :] = uc_ref[tile:tile + CONV_PAD, :]
    zb = (bb * conv * _silu(gb)).astype(BF16)
    za = (attn_ref[...] * _silu(ga)).astype(BF16)

    xq = (xq * (X_HEAD_DIM ** -0.5)).astype(BF16)
    merge_work = [lambda: _dot(za, wpa_ref[...]),
                  lambda: jax.nn.sigmoid(proj(OFF_MG)),
                  lambda: _dot(zb, wpb_ref[...])]
    merged = []
    heads = []
    for h in range(X_HEADS):
        heads.append(_cross_head(xq[:, h * X_HEAD_DIM:(h + 1) * X_HEAD_DIM],
                                 mk_ref[0, :, h * X_HEAD_DIM:(h + 1) * X_HEAD_DIM], mvx_ref[0, h]))
        if h < len(merge_work):
            merged.append(merge_work[h]())
    ya, gate_a, yb = merged
    zx = (jnp.concatenate(heads, axis=1) * _silu(gx)).astype(BF16)
    acc = gate_a * ya + gate_b * yb + gate_x * _dot(zx, wpx_ref[...])
    out = x + _dot(acc.astype(BF16), wout_ref[...])
    if final:
        out = _rms_scale(out) * fg_ref[...]
    xo_ref[...] = out


def _const_spec(shape):
    nd = len(shape)
    return pl.BlockSpec(shape, lambda *_: (0,) * nd, pipeline_mode=pl.Buffered(1))


def _layer_spec(shape):
    nd = len(shape)
    return pl.BlockSpec((None,) + tuple(shape), lambda *a: (a[-1][0],) + (0,) * nd, pipeline_mode=pl.Buffered(1))


def _sample_step(x_ref, g_ref, win_ref, wpa_ref, wpb_ref, wpx_ref, wout_ref, cw_ref, cb_ref,
                 bias_ref, bd_ref, ck_ref, cv_ref, cc_ref, cmk_hbm, cmv_hbm, fg_ref,
                 xo_ref, ko_ref, vo_ref, uo_ref,
                 uc_ref, attn_ref, conv_ref, xat_ref, cmem_ref, cmem_sem, *, layer, nb, nt, final):
    nbp = nb // DECODE_PARTS
    rows_p = nbp * nt

    def part(pi, carry):
        b0 = pi * nbp
        part_rows = pl.ds(pl.multiple_of(pi * rows_p, rows_p), rows_p)
        mem_copies = [pltpu.make_async_copy(src.at[layer, pl.ds(b0, nbp)], cmem_ref.at[n], cmem_sem.at[n])
                      for n, src in enumerate((cmk_hbm, cmv_hbm))]
        for copy in mem_copies:
            copy.start()
        x = x_ref[part_rows, :]
        xn = (_rms_scale(x) * g_ref[...]).astype(BF16)

        def proj(off):
            return _dot(xn, _w_in(win_ref, off))

        q = (proj(OFF_Q) * (HEAD_DIM ** -0.5)).astype(BF16)
        kv = proj(OFF_K)
        k = kv[:, 0:KV_W]
        v = kv[:, KV_W:2 * KV_W]
        ga = proj(OFF_GA)
        ko_ref[part_rows, :] = k
        vo_ref[part_rows, :] = v
        ones = jnp.ones((WINDOW + nt, LANES), BF16)
        ktail, vtail = _key_tails(WINDOW + nt)
        fillers = [functools.partial(proj, off) for off in (OFF_BB, OFF_CC, OFF_HH, OFF_GB, OFF_XQ, OFF_GX)]
        filled = []
        for b in range(nbp):
            rows = slice(b * nt, (b + 1) * nt)
            kall = _dup_heads(jnp.concatenate([ck_ref[b0 + b], k[rows]], axis=0))
            vall = _dup_heads(jnp.concatenate([cv_ref[b0 + b], v[rows]], axis=0))
            for j in range(N_KV_HEADS):
                kj = kall[j].astype(BF16)
                vj = vall[j].astype(BF16)
                qg = q[rows, j * GQ_W:(j + 1) * GQ_W]
                qbd = jnp.concatenate([qg] * GROUP, axis=0) * bd_ref[...]
                on = _band_attention(qbd, jnp.concatenate([jnp.concatenate([kj, kj], axis=1), ktail], axis=0),
                                     jnp.concatenate([jnp.concatenate([vj, ones], axis=1), vtail], axis=0),
                                     bias_ref[j])
                lo, hi = _ungroup(on, nt)
                attn_ref[rows, j * GQ_W:j * GQ_W + LANES] = lo
                attn_ref[rows, j * GQ_W + LANES:(j + 1) * GQ_W] = hi
            while len(filled) * nbp < (b + 1) * len(fillers):
                filled.append(fillers[len(filled)]())
        bb, cc, hh, gb, p3q, gx = filled
        za = attn_ref[...] * _silu(ga)
        for copy in mem_copies:
            copy.wait()

        u = cc * hh
        for b in range(nbp):
            rows = slice(b * nt, (b + 1) * nt)
            ub = u[rows]
            uc_ref[b, CONV_PAD - (CONV_K - 1):CONV_PAD, :] = cc_ref[b0 + b]
            uc_ref[b, CONV_PAD:CONV_PAD + nt, :] = ub
            uo_ref[b0 + b] = ub[nt - (CONV_K - 1):nt]
            conv_ref[rows, :] = (uc_ref[b, CONV_PAD - 2:CONV_PAD - 2 + nt, :] * cw_ref[0:1, :]
                                 + uc_ref[b, CONV_PAD - 1:CONV_PAD - 1 + nt, :] * cw_ref[1:2, :]
                                 + ub * cw_ref[2:3, :]) + cb_ref[...]
        zb = bb * conv_ref[...] * _silu(gb)

        xq = (p3q * (X_HEAD_DIM ** -0.5)).astype(BF16)
        mones = jnp.ones((N_MEM, X_HEAD_DIM), BF16)
        merge_work = [lambda: _dot(za.astype(BF16), wpa_ref[...]),
                      lambda: jax.nn.sigmoid(proj(OFF_MG)),
                      lambda: _dot(zb.astype(BF16), wpb_ref[...]),
                      lambda: jax.nn.sigmoid(proj(OFF_MG + D_MODEL)),
                      lambda: jax.nn.sigmoid(proj(OFF_MG + 2 * D_MODEL))]
        merged = []
        for b in range(nbp):
            rows = slice(b * nt, (b + 1) * nt)
            for h in range(X_HEADS):
                cols = slice(h * X_HEAD_DIM, (h + 1) * X_HEAD_DIM)
                head_rows = pl.ds(h, N_MEM, stride=X_HEADS)
                mvx = jnp.concatenate([cmem_ref[1, b, head_rows, :].astype(BF16), mones], axis=1)
                xat_ref[rows, cols] = _cross_head(xq[rows, cols], cmem_ref[0, b, head_rows, :].astype(BF16), mvx)
            while len(merged) * nbp < (b + 1) * len(merge_work):
                merged.append(merge_work[len(merged)]())
        ya, gate_a, yb, gate_b, gate_x = merged
        zx = xat_ref[...] * _silu(gx)

        acc = gate_a * ya + gate_b * yb + gate_x * _dot(zx.astype(BF16), wpx_ref[...])
        out = x + _dot(acc.astype(BF16), wout_ref[...])
        if final:
            out = _rms_scale(out) * fg_ref[...]
        xo_ref[part_rows, :] = out
        return carry

    lax.fori_loop(0, DECODE_PARTS, part, 0)


def _layer_kernel(layer_ref, x_hbm, g_ref, win_hbm, wpa_hbm, wpb_hbm, wpx_hbm, wout_hbm,
                  cw_ref, cb_ref, fg_ref, bias_ref, bd_ref, mk_ref, mvx_ref,
                  xs_ref, sbias_ref, sbd_ref, ck_ref, cv_ref, cc_ref, cmk_hbm, cmv_hbm,
                  xo_hbm, kl_ref, vl_ref, ul_ref, xso_ref, ko_ref, vo_ref, uo_ref,
                  win_ref, wpa_ref, wpb_ref, wpx_ref, wout_ref, xin_ref, xout_ref, xin_sem, xout_sem,
                  kr_ref, vv_ref, uc_ref, q_ref, attn_ref, suc_ref, sattn_ref, sconv_ref, sxat_ref, cmem_ref, cmem_sem,
                  sin_ref, spa_ref, spb_ref, spx_ref, sout_ref, wsem,
                  *, tile, tiles_per_seq, nb, nt, final):
    i = pl.program_id(0)
    weights = (win_ref, wpa_ref, wpb_ref, wpx_ref, wout_ref)

    @pl.when(i < CAST_STEPS)
    def _():
        layer = layer_ref[0]
        sources = (win_hbm, wpa_hbm, wpb_hbm, wpx_hbm, wout_hbm)
        stages = (sin_ref, spa_ref, spb_ref, spx_ref, sout_ref)

        def chunk_copies(c, slot):
            return [pltpu.make_async_copy(src.at[layer, pl.ds(c * stage.shape[1], stage.shape[1])],
                                          stage.at[slot], wsem.at[slot, n])
                    for n, (src, stage) in enumerate(zip(sources, stages))]

        for copy in chunk_copies(0, 0):
            copy.start()

        def cast_chunk(c, carry):
            slot = lax.rem(c, 2)

            @pl.when(c + 1 < CAST_CHUNKS)
            def _():
                for copy in chunk_copies(c + 1, 1 - slot):
                    copy.start()

            for copy in chunk_copies(c, slot):
                copy.wait()

            def chunk_rows(stage):
                rows = stage.shape[1]
                return pl.ds(pl.multiple_of(c * rows, rows), rows)

            def cast_cols(t, inner):
                cols = pl.ds(pl.multiple_of(t * CAST_COLS, CAST_COLS), CAST_COLS)
                win_ref[chunk_rows(sin_ref), cols] = sin_ref[slot, :, cols].astype(BF16)
                return inner

            lax.fori_loop(0, IN_W // CAST_COLS, cast_cols, 0)
            for stage, dst in zip(stages[1:], weights[1:]):
                dst[chunk_rows(stage), :] = stage[slot].astype(BF16)
            return carry

        lax.fori_loop(0, CAST_CHUNKS, cast_chunk, 0)

    @pl.when(i == CAST_STEPS)
    def _():
        _sample_step(xs_ref, g_ref, *weights, cw_ref, cb_ref,
                     sbias_ref, sbd_ref, ck_ref, cv_ref, cc_ref, cmk_hbm, cmv_hbm, fg_ref,
                     xso_ref, ko_ref, vo_ref, uo_ref, suc_ref, sattn_ref, sconv_ref, sxat_ref, cmem_ref, cmem_sem,
                     layer=layer_ref[0], nb=nb, nt=nt, final=final)

    @pl.when(i > CAST_STEPS)
    def _():
        b = i - (CAST_STEPS + 1)

        def tile_rows(t):
            return pl.ds(pl.multiple_of(t * tile, tile), tile)

        def in_copy(t, slot):
            return pltpu.make_async_copy(x_hbm.at[b, tile_rows(t)], xin_ref.at[slot], xin_sem.at[slot])

        def out_copy(t, slot):
            return pltpu.make_async_copy(xout_ref.at[slot], xo_hbm.at[b, tile_rows(t)], xout_sem.at[slot])

        in_copy(0, 0).start()

        def tile_step(t, carry):
            slot = lax.rem(t, 2)

            @pl.when(t + 1 < tiles_per_seq)
            def _():
                in_copy(t + 1, 1 - slot).start()

            in_copy(t, slot).wait()

            @pl.when(t >= 2)
            def _():
                out_copy(t - 2, slot).wait()

            _prompt_tile(t, xin_ref.at[slot], g_ref, *weights, cw_ref, cb_ref,
                         bias_ref, bd_ref, mk_ref, mvx_ref, fg_ref,
                         xout_ref.at[slot], kl_ref, vl_ref, ul_ref, kr_ref, vv_ref, uc_ref, q_ref, attn_ref,
                         tile=tile, final=final)
            out_copy(t, slot).start()
            return carry

        lax.fori_loop(0, tiles_per_seq, tile_step, 0)
        for t in range(max(tiles_per_seq - 2, 0), tiles_per_seq):
            out_copy(t, t % 2).wait()


_WEIGHT_SHAPES = ((D_MODEL, IN_W), (ATT_W, D_MODEL), (CONV_W, D_MODEL), (X_W, D_MODEL), (D_MODEL, D_MODEL))


def _layer(layer, x, xs, g, win, wpa, wpb, wpx, wout, cw, cb, fg, bias, bd, mkb, mvx,
           sbias, sbd, ck, cv, cc, cmk, cmv, *, nt, final):
    nb, seq, _ = x.shape
    rows = xs.shape[0]
    nbs = rows // nt
    tile = PROMPT_TILE
    assert seq % tile == 0 and tile % WINDOW == 0
    tps = seq // tile
    kern = functools.partial(_layer_kernel, tile=tile, tiles_per_seq=tps, nb=nbs, nt=nt, final=final)

    def seq_of(i):
        return jnp.maximum(i - (CAST_STEPS + 1), 0)

    grid_spec = pltpu.PrefetchScalarGridSpec(
        num_scalar_prefetch=1,
        grid=(CAST_STEPS + 1 + nb,),
        in_specs=[
            pl.BlockSpec(memory_space=pl.ANY),
            _layer_spec((1, D_MODEL)),
            *[pl.BlockSpec(memory_space=pl.ANY) for _ in _WEIGHT_SHAPES],
            _layer_spec((CONV_K, CONV_W)),
            _layer_spec((1, CONV_W)),
            _const_spec((1, D_MODEL)),
            _layer_spec((3, N_KV_HEADS, GROUP * CHUNK, KEY_SLOTS)),
            _const_spec((GROUP * CHUNK, GQ_W)),
            pl.BlockSpec((None, 1, N_MEM, X_W), lambda i, l: (l[0], seq_of(i), 0, 0)),
            pl.BlockSpec((None, 1, X_HEADS, N_MEM, 2 * X_HEAD_DIM), lambda i, l: (l[0], seq_of(i), 0, 0, 0)),
            _const_spec((rows, D_MODEL)),
            _layer_spec((N_KV_HEADS, GROUP * nt, KEY_SLOTS)),
            _const_spec((GROUP * nt, GQ_W)),
            _layer_spec((nbs, WINDOW, KV_W)),
            _layer_spec((nbs, WINDOW, KV_W)),
            _layer_spec((nbs, CONV_K - 1, CONV_W)),
            pl.BlockSpec(memory_space=pl.ANY),
            pl.BlockSpec(memory_space=pl.ANY),
        ],
        out_specs=[
            pl.BlockSpec(memory_space=pl.ANY),
            pl.BlockSpec((1, WINDOW, KV_W), lambda i, l: (seq_of(i), 0, 0)),
            pl.BlockSpec((1, WINDOW, KV_W), lambda i, l: (seq_of(i), 0, 0)),
            pl.BlockSpec((1, CONV_K - 1, CONV_W), lambda i, l: (seq_of(i), 0, 0)),
            pl.BlockSpec((rows, D_MODEL), lambda i, l: (0, 0)),
            pl.BlockSpec((rows, KV_W), lambda i, l: (0, 0)),
            pl.BlockSpec((rows, KV_W), lambda i, l: (0, 0)),
            pl.BlockSpec((nbs, CONV_K - 1, CONV_W), lambda i, l: (0, 0, 0)),
        ],
        scratch_shapes=[
            *[pltpu.VMEM(shape, BF16) for shape in _WEIGHT_SHAPES],
            pltpu.VMEM((2, tile, D_MODEL), F32),
            pltpu.VMEM((2, tile, D_MODEL), F32),
            pltpu.SemaphoreType.DMA((2,)),
            pltpu.SemaphoreType.DMA((2,)),
            pltpu.VMEM((N_KV_HEADS, WINDOW + tile, GQ_W), BF16),
            pltpu.VMEM((N_KV_HEADS, WINDOW + tile, 2 * LANES), BF16),
            pltpu.VMEM((CONV_PAD + tile, CONV_W), F32),
            pltpu.VMEM((tile, ATT_W), BF16),
            pltpu.VMEM((tile, ATT_W), F32),
            pltpu.VMEM((nbs // DECODE_PARTS, CONV_PAD + nt, CONV_W), F32),
            pltpu.VMEM((rows // DECODE_PARTS, ATT_W), F32),
            pltpu.VMEM((rows // DECODE_PARTS, CONV_W), F32),
            pltpu.VMEM((rows // DECODE_PARTS, X_W), F32),
            pltpu.VMEM((2, nbs // DECODE_PARTS, N_MEM * X_HEADS, X_HEAD_DIM), F32),
            pltpu.SemaphoreType.DMA((2,)),
            *[pltpu.VMEM((2, r // CAST_CHUNKS, c), F32) for r, c in _WEIGHT_SHAPES],
            pltpu.SemaphoreType.DMA((2, len(_WEIGHT_SHAPES))),
        ],
    )
    return pl.pallas_call(
        kern,
        grid_spec=grid_spec,
        out_shape=[
            jax.ShapeDtypeStruct((nb, seq, D_MODEL), F32),
            jax.ShapeDtypeStruct((nb, WINDOW, KV_W), F32),
            jax.ShapeDtypeStruct((nb, WINDOW, KV_W), F32),
            jax.ShapeDtypeStruct((nb, CONV_K - 1, CONV_W), F32),
            jax.ShapeDtypeStruct((rows, D_MODEL), F32),
            jax.ShapeDtypeStruct((rows, KV_W), F32),
            jax.ShapeDtypeStruct((rows, KV_W), F32),
            jax.ShapeDtypeStruct((nbs, CONV_K - 1, CONV_W), F32),
        ],
        compiler_params=pltpu.CompilerParams(
            dimension_semantics=("arbitrary",), vmem_limit_bytes=VMEM_LIMIT),
        name="layer",
    )(layer, x, g, win, wpa, wpb, wpx, wout, cw, cb, fg, bias, bd, mkb, mvx,
      xs, sbias, sbd, ck, cv, cc, cmk, cmv)


def _alibi_slopes():
    return np.power(2.0, -8.0 * np.arange(1, N_HEADS + 1, dtype=np.float32) / N_HEADS).astype(np.float32)


def _alibi_bias(nq, nk, masked):
    dist = np.abs(np.arange(nq)[:, None] + WINDOW - np.arange(nk)[None, :]).astype(np.float32)
    slope = _alibi_slopes().reshape(N_KV_HEADS, GROUP)
    bias = -(slope[:, :, None, None] * dist[None, None])
    bias = np.where(np.arange(nk)[None, None, None, :] < masked, -np.inf, bias)
    return bias.reshape(N_KV_HEADS, GROUP * nq, nk).astype(np.float32)


def _bias_with_sink(alibi, attn_sink, nq):
    nk = alibi.shape[-1]
    lead = alibi.shape[:-3]
    sink = jnp.repeat(attn_sink.reshape(DEPTH, N_KV_HEADS, GROUP), nq, axis=2)[..., None]
    sink = jnp.broadcast_to(sink.reshape((DEPTH,) + (1,) * len(lead) + sink.shape[1:]),
                            (DEPTH,) + alibi.shape[:-1] + (1,))
    real = jnp.broadcast_to(jnp.asarray(alibi), (DEPTH,) + alibi.shape)
    unused = jnp.full((DEPTH,) + alibi.shape[:-1] + (KEY_SLOTS - nk - 1,), -jnp.inf, F32)
    return jnp.concatenate([real, sink, unused], axis=-1)


def _block_diag_mask(nq):
    rows = np.arange(GROUP * nq)[:, None] // nq
    cols = np.arange(GQ_W)[None, :] // HEAD_DIM
    return (rows == cols).astype(np.float32)


def kernel(x_prompt, x_sample, cache_attn_k, cache_attn_v, cache_conv, cache_mem_k, cache_mem_v,
           mem_prompt, norm_g, w_in, attn_sink, w_pa, conv_w, conv_b, w_pb, mem_norm_g,
           w_mk, w_mv, w_px, w_out, final_g):
    bp, seq, _ = x_prompt.shape
    bs, nt, _ = x_sample.shape
    win_len = cache_attn_k.shape[2]
    assert win_len == WINDOW and nt <= CHUNK and seq >= WINDOW

    bias_p = _bias_with_sink(np.stack([_alibi_bias(CHUNK, BAND, m) for m in (WINDOW, CHUNK, 0)]), attn_sink, CHUNK)
    bias_s = _bias_with_sink(_alibi_bias(nt, WINDOW + nt, 0), attn_sink, nt)
    bd_p = jnp.asarray(_block_diag_mask(CHUNK), BF16)
    bd_s = jnp.asarray(_block_diag_mask(nt), BF16)
    fg = final_g.reshape(1, D_MODEL)

    mk, mv, mkb, mvx = _memory_kv(mem_prompt, mem_norm_g, w_mk, w_mv)

    xp = x_prompt
    xs = x_sample.reshape(bs * nt, D_MODEL)
    ck = cache_attn_k.reshape(DEPTH, bs, WINDOW, KV_W)
    cv = cache_attn_v.reshape(DEPTH, bs, WINDOW, KV_W)
    g = norm_g.reshape(DEPTH, 1, D_MODEL)
    cb = conv_b.reshape(DEPTH, 1, CONV_W)
    cmk = cache_mem_k.reshape(DEPTH, bs, N_MEM * X_HEADS, X_HEAD_DIM)
    cmv = cache_mem_v.reshape(DEPTH, bs, N_MEM * X_HEADS, X_HEAD_DIM)
    pk, pv, pc, sk, sv, sc = [], [], [], [], [], []
    for l in range(DEPTH):
        final = l == DEPTH - 1
        layer = jnp.full((1,), l, jnp.int32)
        xp, kl, vl, ul, xs, ko, vo, uo = _layer(
            layer, xp, xs, g, w_in, w_pa, w_pb, w_px, w_out, conv_w, cb, fg, bias_p, bd_p, mkb, mvx,
            bias_s, bd_s, ck, cv, cache_conv, cmk, cmv, nt=nt, final=final)
        pk.append(kl)
        pv.append(vl)
        pc.append(ul)
        sk.append(ko)
        sv.append(vo)
        sc.append(uo)

    kv_p = (DEPTH, bp, WINDOW, N_KV_HEADS, HEAD_DIM)
    kv_s = (DEPTH, bs, nt, N_KV_HEADS, HEAD_DIM)
    mem_shape = (DEPTH, bp, N_MEM, X_HEADS, X_HEAD_DIM)
    return (xp, xs.reshape(bs, nt, D_MODEL),
            jnp.stack(pk).reshape(kv_p), jnp.stack(pv).reshape(kv_p), jnp.stack(pc),
            mk.reshape(mem_shape), mv.reshape(mem_shape),
            jnp.stack(sk).reshape(kv_s), jnp.stack(sv).reshape(kv_s), jnp.stack(sc))
```

```python
import functools

import numpy as np
import jax
import jax.numpy as jnp
from jax import lax
from jax.experimental import pallas as pl
from jax.experimental.pallas import tpu as pltpu

D_MODEL = 1024
DEPTH = 4
CHUNK = 64
WINDOW = 128
N_HEADS = 8
N_KV_HEADS = 2
GROUP = N_HEADS // N_KV_HEADS
HEAD_DIM = 64
ATT_W = N_HEADS * HEAD_DIM
KV_W = N_KV_HEADS * HEAD_DIM
CONV_W = 512
CONV_K = 3
X_HEADS = 4
X_HEAD_DIM = 128
X_W = X_HEADS * X_HEAD_DIM
N_MEM = 256
EPS = 1e-6

OFF_Q = 0
OFF_K = OFF_Q + ATT_W
OFF_V = OFF_K + KV_W
OFF_GA = OFF_V + KV_W
OFF_BB = OFF_GA + ATT_W
OFF_CC = OFF_BB + CONV_W
OFF_HH = OFF_CC + CONV_W
OFF_GB = OFF_HH + CONV_W
OFF_XQ = OFF_GB + CONV_W
OFF_GX = OFF_XQ + X_W
OFF_MG = OFF_GX + X_W
IN_W = OFF_MG + 3 * D_MODEL
W_IN_WIDTH = {OFF_Q: ATT_W, OFF_K: 2 * KV_W, OFF_GA: ATT_W, OFF_BB: CONV_W, OFF_CC: CONV_W, OFF_HH: CONV_W,
              OFF_GB: CONV_W, OFF_XQ: X_W, OFF_GX: X_W,
              OFF_MG: D_MODEL, OFF_MG + D_MODEL: D_MODEL, OFF_MG + 2 * D_MODEL: D_MODEL}

LANES = 128
GQ_W = GROUP * HEAD_DIM
BAND = WINDOW + CHUNK
KEY_SLOTS = 256
CONV_PAD = 8
PROMPT_TILE = 512
DECODE_PARTS = 2
Q_ROWS = 128
CAST_STEPS = 1
CAST_CHUNKS = 8
CAST_COLS = 256
VMEM_LIMIT = 61 * 1024 * 1024

F32 = jnp.float32
BF16 = jnp.bfloat16
_NT = (((1,), (1,)), ((), ()))


def _rms_scale(x):
    return x * lax.rsqrt(jnp.mean(x * x, axis=-1, keepdims=True) + EPS)


def _silu(x):
    return x * jax.nn.sigmoid(x)


def _dot(a, b):
    return jnp.dot(a, b, preferred_element_type=F32)


def _w_in(win_ref, off):
    return win_ref[:, off:off + W_IN_WIDTH[off]]


def _dup_heads(a):
    lane = lax.broadcasted_iota(jnp.int32, a.shape, 1)
    sw = pltpu.roll(a, HEAD_DIM, 1)
    lo = lane < HEAD_DIM
    return jnp.where(lo, a, sw), jnp.where(lo, sw, a)


def _key_tails(nk):
    shape = (KEY_SLOTS - nk, 2 * LANES)
    row = lax.broadcasted_iota(jnp.int32, shape, 0)
    col = lax.broadcasted_iota(jnp.int32, shape, 1)
    vtail = jnp.where((row == 0) & (col >= LANES), 1.0, 0.0).astype(BF16)
    return jnp.zeros(shape, BF16), vtail


def _band_attention(qbd, kb, vb, bias):
    s = lax.dot_general(qbd, kb, _NT, preferred_element_type=F32) + bias
    p = jnp.exp(s - jnp.max(s, axis=-1, keepdims=True)).astype(BF16)
    oe = _dot(p, vb)
    return oe[:, 0:LANES] / oe[:, LANES:2 * LANES]


def _ungroup(on, nq):
    lo = lax.broadcasted_iota(jnp.int32, (nq, LANES), 1) < HEAD_DIM
    return (jnp.where(lo, on[0:nq], on[nq:2 * nq]),
            jnp.where(lo, on[2 * nq:3 * nq], on[3 * nq:4 * nq]))


def _cross_head(xq_h, mk_h, mvx_h):
    s = lax.dot_general(xq_h, mk_h, _NT, preferred_element_type=F32)
    p = jnp.exp(s - jnp.max(s, axis=-1, keepdims=True)).astype(BF16)
    oe = _dot(p, mvx_h)
    return oe[:, 0:X_HEAD_DIM] / oe[:, X_HEAD_DIM:2 * X_HEAD_DIM]


def _memkv_kernel(mem_ref, g_ref, wk_ref, wv_ref, mk_ref, mv_ref, mkb_ref, mvx_ref):
    nb = mem_ref.shape[0]
    mn = (_rms_scale(mem_ref[...].reshape(nb * N_MEM, D_MODEL)) * g_ref[0]).astype(BF16)
    mk = _dot(mn, wk_ref[0].astype(BF16))
    mv = _dot(mn, wv_ref[0].astype(BF16))
    ones = jnp.ones((N_MEM, X_HEAD_DIM), BF16)
    for b in range(nb):
        rows = slice(b * N_MEM, (b + 1) * N_MEM)
        mkb_ref[0, b] = mk[rows].astype(BF16)
        for h in range(X_HEADS):
            cols = slice(h * X_HEAD_DIM, (h + 1) * X_HEAD_DIM)
            mk_ref[0, b, pl.ds(h, N_MEM, stride=X_HEADS), :] = mk[rows, cols]
            mv_ref[0, b, pl.ds(h, N_MEM, stride=X_HEADS), :] = mv[rows, cols]
            mvx_ref[0, b, h, :, 0:X_HEAD_DIM] = mv[rows, cols].astype(BF16)
            mvx_ref[0, b, h, :, X_HEAD_DIM:2 * X_HEAD_DIM] = ones


def _memory_kv(mem, g, wk, wv):
    nb = mem.shape[0]
    return pl.pallas_call(
        _memkv_kernel,
        grid=(DEPTH,),
        in_specs=[
            pl.BlockSpec((nb, N_MEM, D_MODEL), lambda l: (0, 0, 0)),
            pl.BlockSpec((1, 1, D_MODEL), lambda l: (l, 0, 0)),
            pl.BlockSpec((1, D_MODEL, X_W), lambda l: (l, 0, 0)),
            pl.BlockSpec((1, D_MODEL, X_W), lambda l: (l, 0, 0)),
        ],
        out_specs=[
            pl.BlockSpec((1, nb, N_MEM * X_HEADS, X_HEAD_DIM), lambda l: (l, 0, 0, 0)),
            pl.BlockSpec((1, nb, N_MEM * X_HEADS, X_HEAD_DIM), lambda l: (l, 0, 0, 0)),
            pl.BlockSpec((1, nb, N_MEM, X_W), lambda l: (l, 0, 0, 0)),
            pl.BlockSpec((1, nb, X_HEADS, N_MEM, 2 * X_HEAD_DIM), lambda l: (l, 0, 0, 0, 0)),
        ],
        out_shape=[
            jax.ShapeDtypeStruct((DEPTH, nb, N_MEM * X_HEADS, X_HEAD_DIM), F32),
            jax.ShapeDtypeStruct((DEPTH, nb, N_MEM * X_HEADS, X_HEAD_DIM), F32),
            jax.ShapeDtypeStruct((DEPTH, nb, N_MEM, X_W), BF16),
            jax.ShapeDtypeStruct((DEPTH, nb, X_HEADS, N_MEM, 2 * X_HEAD_DIM), BF16),
        ],
        name="memory_kv",
    )(mem, g.reshape(DEPTH, 1, D_MODEL), wk, wv)


def _prompt_tile(t, x_ref, g_ref, win_ref, wpa_ref, wpb_ref, wpx_ref, wout_ref, cw_ref, cb_ref,
                 bias_ref, bd_ref, mk_ref, mvx_ref, fg_ref,
                 xo_ref, kl_ref, vl_ref, ul_ref,
                 kr_ref, vv_ref, uc_ref, q_ref, attn_ref, *, tile, final):
    @pl.when(t == 0)
    def _():
        kr_ref[:, 0:WINDOW, :] = jnp.zeros((N_KV_HEADS, WINDOW, GQ_W), BF16)
        vv_ref[:, 0:WINDOW, 0:LANES] = jnp.zeros((N_KV_HEADS, WINDOW, LANES), BF16)
        vv_ref[:, :, LANES:2 * LANES] = jnp.ones((N_KV_HEADS, WINDOW + tile, LANES), BF16)
        uc_ref[0:CONV_PAD, :] = jnp.zeros((CONV_PAD, CONV_W), F32)

    x = x_ref[...]
    xn = (_rms_scale(x) * g_ref[...]).astype(BF16)

    def proj(off):
        return _dot(xn, _w_in(win_ref, off))

    w_q = _w_in(win_ref, OFF_Q)
    for r in range(0, tile, Q_ROWS):
        q_ref[r:r + Q_ROWS, :] = (_dot(xn[r:r + Q_ROWS], w_q) * (HEAD_DIM ** -0.5)).astype(BF16)
    half = tile // 2
    w_kv = _w_in(win_ref, OFF_K)
    kv = jnp.concatenate([_dot(xn[0:half], w_kv), _dot(xn[half:tile], w_kv)], axis=0)
    k = kv[:, 0:KV_W]
    v = kv[:, KV_W:2 * KV_W]
    ga = proj(OFF_GA)
    kl_ref[0] = k[tile - WINDOW:tile]
    vl_ref[0] = v[tile - WINDOW:tile]
    for j, (kj, vj) in enumerate(zip(_dup_heads(k), _dup_heads(v))):
        kj = kj.astype(BF16)
        kr_ref[j, WINDOW:WINDOW + tile, 0:LANES] = kj
        kr_ref[j, WINDOW:WINDOW + tile, LANES:2 * LANES] = kj
        vv_ref[j, WINDOW:WINDOW + tile, 0:LANES] = vj.astype(BF16)

    fillers = [functools.partial(proj, off) for off in (OFF_BB, OFF_CC, OFF_HH, OFF_GB, OFF_XQ, OFF_GX)]
    fillers += [lambda i=i: jax.nn.sigmoid(proj(OFF_MG + i * D_MODEL)) for i in (1, 2)]
    filled = []
    n_chunks = tile // CHUNK
    ktail, vtail = _key_tails(BAND)

    for c in range(n_chunks):
        r0 = c * CHUNK
        qc = q_ref[r0:r0 + CHUNK, :]
        variant = jnp.where(t == 0, c, 2) if c < 2 else 2
        for j in range(N_KV_HEADS):
            qg = qc[:, j * GQ_W:(j + 1) * GQ_W]
            qbd = jnp.concatenate([qg] * GROUP, axis=0) * bd_ref[...]
            on = _band_attention(qbd, jnp.concatenate([kr_ref[j, r0:r0 + BAND, :], ktail], axis=0),
                                 jnp.concatenate([vv_ref[j, r0:r0 + BAND, :], vtail], axis=0),
                                 bias_ref[variant, j])
            lo, hi = _ungroup(on, CHUNK)
            attn_ref[r0:r0 + CHUNK, j * GQ_W:j * GQ_W + LANES] = lo
            attn_ref[r0:r0 + CHUNK, j * GQ_W + LANES:(j + 1) * GQ_W] = hi
        while len(filled) * n_chunks < (c + 1) * len(fillers):
            filled.append(fillers[len(filled)]())
    bb, cc, hh, gb, xq, gx, gate_b, gate_x = filled
    kr_ref[:, 0:WINDOW, :] = kr_ref[:, tile:tile + WINDOW, :]
    vv_ref[:, 0:WINDOW, 0:LANES] = vv_ref[:, tile:tile + WINDOW, 0:LANES]

    u = cc * hh
    uc_ref[CONV_PAD:CONV_PAD + tile, :] = u
    ul_ref[0] = u[tile - (CONV_K - 1):tile]
    conv = (uc_ref[CONV_PAD - 2:CONV_PAD - 2 + tile, :] * cw_ref[0:1, :]
            + uc_ref[CONV_PAD - 1:CONV_PAD - 1 + tile, :] * cw_ref[1:2, :]
            + u * cw_ref[2:3, :]) + cb_ref[...]
    uc_ref[0:CONV_PAD, :] = uc_ref[tile:tile + CONV_PAD, :]
    zb = (bb * conv * _silu(gb)).astype(BF16)
    za = (attn_ref[...] * _silu(ga)).astype(BF16)

    xq = (xq * (X_HEAD_DIM ** -0.5)).astype(BF16)
    merge_work = [lambda: _dot(za, wpa_ref[...]),
                  lambda: jax.nn.sigmoid(proj(OFF_MG)),
                  lambda: _dot(zb, wpb_ref[...])]
    merged = []
    heads = []
    for h in range(X_HEADS):
        heads.append(_cross_head(xq[:, h * X_HEAD_DIM:(h + 1) * X_HEAD_DIM],
                                 mk_ref[0, :, h * X_HEAD_DIM:(h + 1) * X_HEAD_DIM], mvx_ref[0, h]))
        if h < len(merge_work):
            merged.append(merge_work[h]())
    ya, gate_a, yb = merged
    zx = (jnp.concatenate(heads, axis=1) * _silu(gx)).astype(BF16)
    acc = gate_a * ya + gate_b * yb + gate_x * _dot(zx, wpx_ref[...])
    out = x + _dot(acc.astype(BF16), wout_ref[...])
    if final:
        out = _rms_scale(out) * fg_ref[...]
    xo_ref[...] = out


def _const_spec(shape):
    nd = len(shape)
    return pl.BlockSpec(shape, lambda *_: (0,) * nd, pipeline_mode=pl.Buffered(1))


def _layer_spec(shape):
    nd = len(shape)
    return pl.BlockSpec((None,) + tuple(shape), lambda *a: (a[-1][0],) + (0,) * nd, pipeline_mode=pl.Buffered(1))


def _sample_step(x_ref, g_ref, win_ref, wpa_ref, wpb_ref, wpx_ref, wout_ref, cw_ref, cb_ref,
                 bias_ref, bd_ref, ck_ref, cv_ref, cc_ref, cmk_hbm, cmv_hbm, fg_ref,
                 xo_ref, ko_ref, vo_ref, uo_ref,
                 uc_ref, attn_ref, conv_ref, xat_ref, cmem_ref, cmem_sem, *, layer, nb, nt, final):
    nbp = nb // DECODE_PARTS
    rows_p = nbp * nt

    def part(pi, carry):
        b0 = pi * nbp
        part_rows = pl.ds(pl.multiple_of(pi * rows_p, rows_p), rows_p)
        mem_copies = [pltpu.make_async_copy(src.at[layer, pl.ds(b0, nbp)], cmem_ref.at[n], cmem_sem.at[n])
                      for n, src in enumerate((cmk_hbm, cmv_hbm))]
        for copy in mem_copies:
            copy.start()
        x = x_ref[part_rows, :]
        xn = (_rms_scale(x) * g_ref[...]).astype(BF16)

        def proj(off):
            return _dot(xn, _w_in(win_ref, off))

        q = (proj(OFF_Q) * (HEAD_DIM ** -0.5)).astype(BF16)
        kv = proj(OFF_K)
        k = kv[:, 0:KV_W]
        v = kv[:, KV_W:2 * KV_W]
        ga = proj(OFF_GA)
        ko_ref[part_rows, :] = k
        vo_ref[part_rows, :] = v
        ones = jnp.ones((WINDOW + nt, LANES), BF16)
        ktail, vtail = _key_tails(WINDOW + nt)
        fillers = [functools.partial(proj, off) for off in (OFF_BB, OFF_CC, OFF_HH, OFF_GB, OFF_XQ, OFF_GX)]
        filled = []
        for b in range(nbp):
            rows = slice(b * nt, (b + 1) * nt)
            kall = _dup_heads(jnp.concatenate([ck_ref[b0 + b], k[rows]], axis=0))
            vall = _dup_heads(jnp.concatenate([cv_ref[b0 + b], v[rows]], axis=0))
            for j in range(N_KV_HEADS):
                kj = kall[j].astype(BF16)
                vj = vall[j].astype(BF16)
                qg = q[rows, j * GQ_W:(j + 1) * GQ_W]
                qbd = jnp.concatenate([qg] * GROUP, axis=0) * bd_ref[...]
                on = _band_attention(qbd, jnp.concatenate([jnp.concatenate([kj, kj], axis=1), ktail], axis=0),
                                     jnp.concatenate([jnp.concatenate([vj, ones], axis=1), vtail], axis=0),
                                     bias_ref[j])
                lo, hi = _ungroup(on, nt)
                attn_ref[rows, j * GQ_W:j * GQ_W + LANES] = lo
                attn_ref[rows, j * GQ_W + LANES:(j + 1) * GQ_W] = hi
            while len(filled) * nbp < (b + 1) * len(fillers):
                filled.append(fillers[len(filled)]())
        bb, cc, hh, gb, p3q, gx = filled
        za = attn_ref[...] * _silu(ga)
        for copy in mem_copies:
            copy.wait()

        u = cc * hh
        for b in range(nbp):
            rows = slice(b * nt, (b + 1) * nt)
            ub = u[rows]
            uc_ref[b, CONV_PAD - (CONV_K - 1):CONV_PAD, :] = cc_ref[b0 + b]
            uc_ref[b, CONV_PAD:CONV_PAD + nt, :] = ub
            uo_ref[b0 + b] = ub[nt - (CONV_K - 1):nt]
            conv_ref[rows, :] = (uc_ref[b, CONV_PAD - 2:CONV_PAD - 2 + nt, :] * cw_ref[0:1, :]
                                 + uc_ref[b, CONV_PAD - 1:CONV_PAD - 1 + nt, :] * cw_ref[1:2, :]
                                 + ub * cw_ref[2:3, :]) + cb_ref[...]
        zb = bb * conv_ref[...] * _silu(gb)

        xq = (p3q * (X_HEAD_DIM ** -0.5)).astype(BF16)
        mones = jnp.ones((N_MEM, X_HEAD_DIM), BF16)
        merge_work = [lambda: _dot(za.astype(BF16), wpa_ref[...]),
                      lambda: jax.nn.sigmoid(proj(OFF_MG)),
                      lambda: _dot(zb.astype(BF16), wpb_ref[...]),
                      lambda: jax.nn.sigmoid(proj(OFF_MG + D_MODEL)),
                      lambda: jax.nn.sigmoid(proj(OFF_MG + 2 * D_MODEL))]
        merged = []
        for b in range(nbp):
            rows = slice(b * nt, (b + 1) * nt)
            for h in range(X_HEADS):
                cols = slice(h * X_HEAD_DIM, (h + 1) * X_HEAD_DIM)
                head_rows = pl.ds(h, N_MEM, stride=X_HEADS)
                mvx = jnp.concatenate([cmem_ref[1, b, head_rows, :].astype(BF16), mones], axis=1)
                xat_ref[rows, cols] = _cross_head(xq[rows, cols], cmem_ref[0, b, head_rows, :].astype(BF16), mvx)
            while len(merged) * nbp < (b + 1) * len(merge_work):
                merged.append(merge_work[len(merged)]())
        ya, gate_a, yb, gate_b, gate_x = merged
        zx = xat_ref[...] * _silu(gx)

        acc = gate_a * ya + gate_b * yb + gate_x * _dot(zx.astype(BF16), wpx_ref[...])
        out = x + _dot(acc.astype(BF16), wout_ref[...])
        if final:
            out = _rms_scale(out) * fg_ref[...]
        xo_ref[part_rows, :] = out
        return carry

    lax.fori_loop(0, DECODE_PARTS, part, 0)


def _layer_kernel(layer_ref, x_hbm, g_ref, win_hbm, wpa_hbm, wpb_hbm, wpx_hbm, wout_hbm,
                  cw_ref, cb_ref, fg_ref, bias_ref, bd_ref, mk_ref, mvx_ref,
                  xs_ref, sbias_ref, sbd_ref, ck_ref, cv_ref, cc_ref, cmk_hbm, cmv_hbm,
                  xo_hbm, kl_ref, vl_ref, ul_ref, xso_ref, ko_ref, vo_ref, uo_ref,
                  win_ref, wpa_ref, wpb_ref, wpx_ref, wout_ref, xin_ref, xout_ref, xin_sem, xout_sem,
                  kr_ref, vv_ref, uc_ref, q_ref, attn_ref, suc_ref, sattn_ref, sconv_ref, sxat_ref, cmem_ref, cmem_sem,
                  sin_ref, spa_ref, spb_ref, spx_ref, sout_ref, wsem,
                  *, tile, tiles_per_seq, nb, nt, final):
    i = pl.program_id(0)
    weights = (win_ref, wpa_ref, wpb_ref, wpx_ref, wout_ref)

    @pl.when(i < CAST_STEPS)
    def _():
        layer = layer_ref[0]
        sources = (win_hbm, wpa_hbm, wpb_hbm, wpx_hbm, wout_hbm)
        stages = (sin_ref, spa_ref, spb_ref, spx_ref, sout_ref)

        def chunk_copies(c, slot):
            return [pltpu.make_async_copy(src.at[layer, pl.ds(c * stage.shape[1], stage.shape[1])],
                                          stage.at[slot], wsem.at[slot, n])
                    for n, (src, stage) in enumerate(zip(sources, stages))]

        for copy in chunk_copies(0, 0):
            copy.start()

        def cast_chunk(c, carry):
            slot = lax.rem(c, 2)

            @pl.when(c + 1 < CAST_CHUNKS)
            def _():
                for copy in chunk_copies(c + 1, 1 - slot):
                    copy.start()

            for copy in chunk_copies(c, slot):
                copy.wait()

            def chunk_rows(stage):
                rows = stage.shape[1]
                return pl.ds(pl.multiple_of(c * rows, rows), rows)

            def cast_cols(t, inner):
                cols = pl.ds(pl.multiple_of(t * CAST_COLS, CAST_COLS), CAST_COLS)
                win_ref[chunk_rows(sin_ref), cols] = sin_ref[slot, :, cols].astype(BF16)
                return inner

            lax.fori_loop(0, IN_W // CAST_COLS, cast_cols, 0)
            for stage, dst in zip(stages[1:], weights[1:]):
                dst[chunk_rows(stage), :] = stage[slot].astype(BF16)
            return carry

        lax.fori_loop(0, CAST_CHUNKS, cast_chunk, 0)

    def tile_rows(t):
        return pl.ds(pl.multiple_of(t * tile, tile), tile)

    def in_copy(b, t, slot):
        return pltpu.make_async_copy(x_hbm.at[b, tile_rows(t)], xin_ref.at[slot], xin_sem.at[slot])

    @pl.when(i == CAST_STEPS)
    def _():
        in_copy(0, 0, 0).start()
        _sample_step(xs_ref, g_ref, *weights, cw_ref, cb_ref,
                     sbias_ref, sbd_ref, ck_ref, cv_ref, cc_ref, cmk_hbm, cmv_hbm, fg_ref,
                     xso_ref, ko_ref, vo_ref, uo_ref, suc_ref, sattn_ref, sconv_ref, sxat_ref, cmem_ref, cmem_sem,
                     layer=layer_ref[0], nb=nb, nt=nt, final=final)

    @pl.when(i > CAST_STEPS)
    def _():
        b = i - (CAST_STEPS + 1)
        n_seq = pl.num_programs(0) - (CAST_STEPS + 1)
        first = b * tiles_per_seq

        def out_copy(t, slot):
            return pltpu.make_async_copy(xout_ref.at[slot], xo_hbm.at[b, tile_rows(t)], xout_sem.at[slot])

        def tile_step(t, carry):
            slot = lax.rem(first + t, 2)

            @pl.when(t + 1 < tiles_per_seq)
            def _():
                in_copy(b, t + 1, 1 - slot).start()

            @pl.when(jnp.logical_and(t + 1 == tiles_per_seq, b + 1 < n_seq))
            def _():
                in_copy(b + 1, 0, 1 - slot).start()

            in_copy(b, t, slot).wait()

            @pl.when(t >= 2)
            def _():
                out_copy(t - 2, slot).wait()

            _prompt_tile(t, xin_ref.at[slot], g_ref, *weights, cw_ref, cb_ref,
                         bias_ref, bd_ref, mk_ref, mvx_ref, fg_ref,
                         xout_ref.at[slot], kl_ref, vl_ref, ul_ref, kr_ref, vv_ref, uc_ref, q_ref, attn_ref,
                         tile=tile, final=final)
            out_copy(t, slot).start()
            return carry

        lax.fori_loop(0, tiles_per_seq, tile_step, 0)
        for t in range(max(tiles_per_seq - 2, 0), tiles_per_seq):
            out_copy(t, lax.rem(first + t, 2)).wait()


_WEIGHT_SHAPES = ((D_MODEL, IN_W), (ATT_W, D_MODEL), (CONV_W, D_MODEL), (X_W, D_MODEL), (D_MODEL, D_MODEL))


def _layer(layer, x, xs, g, win, wpa, wpb, wpx, wout, cw, cb, fg, bias, bd, mkb, mvx,
           sbias, sbd, ck, cv, cc, cmk, cmv, *, nt, final):
    nb, seq, _ = x.shape
    rows = xs.shape[0]
    nbs = rows // nt
    tile = PROMPT_TILE
    assert seq % tile == 0 and tile % WINDOW == 0
    tps = seq // tile
    kern = functools.partial(_layer_kernel, tile=tile, tiles_per_seq=tps, nb=nbs, nt=nt, final=final)

    def seq_of(i):
        return jnp.maximum(i - (CAST_STEPS + 1), 0)

    grid_spec = pltpu.PrefetchScalarGridSpec(
        num_scalar_prefetch=1,
        grid=(CAST_STEPS + 1 + nb,),
        in_specs=[
            pl.BlockSpec(memory_space=pl.ANY),
            _layer_spec((1, D_MODEL)),
            *[pl.BlockSpec(memory_space=pl.ANY) for _ in _WEIGHT_SHAPES],
            _layer_spec((CONV_K, CONV_W)),
            _layer_spec((1, CONV_W)),
            _const_spec((1, D_MODEL)),
            _layer_spec((3, N_KV_HEADS, GROUP * CHUNK, KEY_SLOTS)),
            _const_spec((GROUP * CHUNK, GQ_W)),
            pl.BlockSpec((None, 1, N_MEM, X_W), lambda i, l: (l[0], seq_of(i), 0, 0)),
            pl.BlockSpec((None, 1, X_HEADS, N_MEM, 2 * X_HEAD_DIM), lambda i, l: (l[0], seq_of(i), 0, 0, 0)),
            _const_spec((rows, D_MODEL)),
            _layer_spec((N_KV_HEADS, GROUP * nt, KEY_SLOTS)),
            _const_spec((GROUP * nt, GQ_W)),
            _layer_spec((nbs, WINDOW, KV_W)),
            _layer_spec((nbs, WINDOW, KV_W)),
            _layer_spec((nbs, CONV_K - 1, CONV_W)),
            pl.BlockSpec(memory_space=pl.ANY),
            pl.BlockSpec(memory_space=pl.ANY),
        ],
        out_specs=[
            pl.BlockSpec(memory_space=pl.ANY),
            pl.BlockSpec((1, WINDOW, KV_W), lambda i, l: (seq_of(i), 0, 0)),
            pl.BlockSpec((1, WINDOW, KV_W), lambda i, l: (seq_of(i), 0, 0)),
            pl.BlockSpec((1, CONV_K - 1, CONV_W), lambda i, l: (seq_of(i), 0, 0)),
            pl.BlockSpec((rows, D_MODEL), lambda i, l: (0, 0)),
            pl.BlockSpec((rows, KV_W), lambda i, l: (0, 0)),
            pl.BlockSpec((rows, KV_W), lambda i, l: (0, 0)),
            pl.BlockSpec((nbs, CONV_K - 1, CONV_W), lambda i, l: (0, 0, 0)),
        ],
        scratch_shapes=[
            *[pltpu.VMEM(shape, BF16) for shape in _WEIGHT_SHAPES],
            pltpu.VMEM((2, tile, D_MODEL), F32),
            pltpu.VMEM((2, tile, D_MODEL), F32),
            pltpu.SemaphoreType.DMA((2,)),
            pltpu.SemaphoreType.DMA((2,)),
            pltpu.VMEM((N_KV_HEADS, WINDOW + tile, GQ_W), BF16),
            pltpu.VMEM((N_KV_HEADS, WINDOW + tile, 2 * LANES), BF16),
            pltpu.VMEM((CONV_PAD + tile, CONV_W), F32),
            pltpu.VMEM((tile, ATT_W), BF16),
            pltpu.VMEM((tile, ATT_W), F32),
            pltpu.VMEM((nbs // DECODE_PARTS, CONV_PAD + nt, CONV_W), F32),
            pltpu.VMEM((rows // DECODE_PARTS, ATT_W), F32),
            pltpu.VMEM((rows // DECODE_PARTS, CONV_W), F32),
            pltpu.VMEM((rows // DECODE_PARTS, X_W), F32),
            pltpu.VMEM((2, nbs // DECODE_PARTS, N_MEM * X_HEADS, X_HEAD_DIM), F32),
            pltpu.SemaphoreType.DMA((2,)),
            *[pltpu.VMEM((2, r // CAST_CHUNKS, c), F32) for r, c in _WEIGHT_SHAPES],
            pltpu.SemaphoreType.DMA((2, len(_WEIGHT_SHAPES))),
        ],
    )
    return pl.pallas_call(
        kern,
        grid_spec=grid_spec,
        out_shape=[
            jax.ShapeDtypeStruct((nb, seq, D_MODEL), F32),
            jax.ShapeDtypeStruct((nb, WINDOW, KV_W), F32),
            jax.ShapeDtypeStruct((nb, WINDOW, KV_W), F32),
            jax.ShapeDtypeStruct((nb, CONV_K - 1, CONV_W), F32),
            jax.ShapeDtypeStruct((rows, D_MODEL), F32),
            jax.ShapeDtypeStruct((rows, KV_W), F32),
            jax.ShapeDtypeStruct((rows, KV_W), F32),
            jax.ShapeDtypeStruct((nbs, CONV_K - 1, CONV_W), F32),
        ],
        compiler_params=pltpu.CompilerParams(
            dimension_semantics=("arbitrary",), vmem_limit_bytes=VMEM_LIMIT),
        name="layer",
    )(layer, x, g, win, wpa, wpb, wpx, wout, cw, cb, fg, bias, bd, mkb, mvx,
      xs, sbias, sbd, ck, cv, cc, cmk, cmv)


def _alibi_slopes():
    return np.power(2.0, -8.0 * np.arange(1, N_HEADS + 1, dtype=np.float32) / N_HEADS).astype(np.float32)


def _alibi_bias(nq, nk, masked):
    dist = np.abs(np.arange(nq)[:, None] + WINDOW - np.arange(nk)[None, :]).astype(np.float32)
    slope = _alibi_slopes().reshape(N_KV_HEADS, GROUP)
    bias = -(slope[:, :, None, None] * dist[None, None])
    bias = np.where(np.arange(nk)[None, None, None, :] < masked, -np.inf, bias)
    return bias.reshape(N_KV_HEADS, GROUP * nq, nk).astype(np.float32)


def _bias_with_sink(alibi, attn_sink, nq):
    nk = alibi.shape[-1]
    lead = alibi.shape[:-3]
    sink = jnp.repeat(attn_sink.reshape(DEPTH, N_KV_HEADS, GROUP), nq, axis=2)[..., None]
    sink = jnp.broadcast_to(sink.reshape((DEPTH,) + (1,) * len(lead) + sink.shape[1:]),
                            (DEPTH,) + alibi.shape[:-1] + (1,))
    real = jnp.broadcast_to(jnp.asarray(alibi), (DEPTH,) + alibi.shape)
    unused = jnp.full((DEPTH,) + alibi.shape[:-1] + (KEY_SLOTS - nk - 1,), -jnp.inf, F32)
    return jnp.concatenate([real, sink, unused], axis=-1)


def _block_diag_mask(nq):
    rows = np.arange(GROUP * nq)[:, None] // nq
    cols = np.arange(GQ_W)[None, :] // HEAD_DIM
    return (rows == cols).astype(np.float32)


def kernel(x_prompt, x_sample, cache_attn_k, cache_attn_v, cache_conv, cache_mem_k, cache_mem_v,
           mem_prompt, norm_g, w_in, attn_sink, w_pa, conv_w, conv_b, w_pb, mem_norm_g,
           w_mk, w_mv, w_px, w_out, final_g):
    bp, seq, _ = x_prompt.shape
    bs, nt, _ = x_sample.shape
    win_len = cache_attn_k.shape[2]
    assert win_len == WINDOW and nt <= CHUNK and seq >= WINDOW

    bias_p = _bias_with_sink(np.stack([_alibi_bias(CHUNK, BAND, m) for m in (WINDOW, CHUNK, 0)]), attn_sink, CHUNK)
    bias_s = _bias_with_sink(_alibi_bias(nt, WINDOW + nt, 0), attn_sink, nt)
    bd_p = jnp.asarray(_block_diag_mask(CHUNK), BF16)
    bd_s = jnp.asarray(_block_diag_mask(nt), BF16)
    fg = final_g.reshape(1, D_MODEL)

    mk, mv, mkb, mvx = _memory_kv(mem_prompt, mem_norm_g, w_mk, w_mv)

    xp = x_prompt
    xs = x_sample.reshape(bs * nt, D_MODEL)
    ck = cache_attn_k.reshape(DEPTH, bs, WINDOW, KV_W)
    cv = cache_attn_v.reshape(DEPTH, bs, WINDOW, KV_W)
    g = norm_g.reshape(DEPTH, 1, D_MODEL)
    cb = conv_b.reshape(DEPTH, 1, CONV_W)
    cmk = cache_mem_k.reshape(DEPTH, bs, N_MEM * X_HEADS, X_HEAD_DIM)
    cmv = cache_mem_v.reshape(DEPTH, bs, N_MEM * X_HEADS, X_HEAD_DIM)
    pk, pv, pc, sk, sv, sc = [], [], [], [], [], []
    for l in range(DEPTH):
        final = l == DEPTH - 1
        layer = jnp.full((1,), l, jnp.int32)
        xp, kl, vl, ul, xs, ko, vo, uo = _layer(
            layer, xp, xs, g, w_in, w_pa, w_pb, w_px, w_out, conv_w, cb, fg, bias_p, bd_p, mkb, mvx,
            bias_s, bd_s, ck, cv, cache_conv, cmk, cmv, nt=nt, final=final)
        pk.append(kl)
        pv.append(vl)
        pc.append(ul)
        sk.append(ko)
        sv.append(vo)
        sc.append(uo)

    kv_p = (DEPTH, bp, WINDOW, N_KV_HEADS, HEAD_DIM)
    kv_s = (DEPTH, bs, nt, N_KV_HEADS, HEAD_DIM)
    mem_shape = (DEPTH, bp, N_MEM, X_HEADS, X_HEAD_DIM)
    return (xp, xs.reshape(bs, nt, D_MODEL),
            jnp.stack(pk).reshape(kv_p), jnp.stack(pv).reshape(kv_p), jnp.stack(pc),
            mk.reshape(mem_shape), mv.reshape(mem_shape),
            jnp.stack(sk).reshape(kv_s), jnp.stack(sv).reshape(kv_s), jnp.stack(sc))
```

```python
import functools

import numpy as np
import jax
import jax.numpy as jnp
from jax import lax
from jax.experimental import pallas as pl
from jax.experimental.pallas import tpu as pltpu

D_MODEL = 1024
DEPTH = 4
CHUNK = 64
WINDOW = 128
N_HEADS = 8
N_KV_HEADS = 2
GROUP = N_HEADS // N_KV_HEADS
HEAD_DIM = 64
ATT_W = N_HEADS * HEAD_DIM
KV_W = N_KV_HEADS * HEAD_DIM
CONV_W = 512
CONV_K = 3
X_HEADS = 4
X_HEAD_DIM = 128
X_W = X_HEADS * X_HEAD_DIM
N_MEM = 256
EPS = 1e-6

OFF_Q = 0
OFF_K = OFF_Q + ATT_W
OFF_V = OFF_K + KV_W
OFF_GA = OFF_V + KV_W
OFF_BB = OFF_GA + ATT_W
OFF_CC = OFF_BB + CONV_W
OFF_HH = OFF_CC + CONV_W
OFF_GB = OFF_HH + CONV_W
OFF_XQ = OFF_GB + CONV_W
OFF_GX = OFF_XQ + X_W
OFF_MG = OFF_GX + X_W
IN_W = OFF_MG + 3 * D_MODEL
W_IN_WIDTH = {OFF_Q: ATT_W, OFF_K: 2 * KV_W, OFF_GA: ATT_W, OFF_BB: CONV_W, OFF_CC: CONV_W, OFF_HH: CONV_W,
              OFF_GB: CONV_W, OFF_XQ: X_W, OFF_GX: X_W,
              OFF_MG: D_MODEL, OFF_MG + D_MODEL: D_MODEL, OFF_MG + 2 * D_MODEL: D_MODEL}

LANES = 128
GQ_W = GROUP * HEAD_DIM
BAND = WINDOW + CHUNK
KEY_SLOTS = 256
CONV_PAD = 8
PROMPT_TILE = 512
DECODE_PARTS = 2
Q_ROWS = 128
CAST_STEPS = 1
CAST_CHUNKS = 8
CAST_COLS = 256
VMEM_LIMIT = 61 * 1024 * 1024

F32 = jnp.float32
BF16 = jnp.bfloat16
_NT = (((1,), (1,)), ((), ()))


def _rms_scale(x):
    return x * lax.rsqrt(jnp.mean(x * x, axis=-1, keepdims=True) + EPS)


def _silu(x):
    return x * jax.nn.sigmoid(x)


def _dot(a, b):
    return jnp.dot(a, b, preferred_element_type=F32)


def _w_in(win_ref, off):
    return win_ref[:, off:off + W_IN_WIDTH[off]]


def _dup_heads(a):
    lane = lax.broadcasted_iota(jnp.int32, a.shape, 1)
    sw = pltpu.roll(a, HEAD_DIM, 1)
    lo = lane < HEAD_DIM
    return jnp.where(lo, a, sw), jnp.where(lo, sw, a)


def _key_tails(nk):
    shape = (KEY_SLOTS - nk, 2 * LANES)
    row = lax.broadcasted_iota(jnp.int32, shape, 0)
    col = lax.broadcasted_iota(jnp.int32, shape, 1)
    vtail = jnp.where((row == 0) & (col >= LANES), 1.0, 0.0).astype(BF16)
    return jnp.zeros(shape, BF16), vtail


def _band_attention(qbd, kb, vb, bias):
    s = lax.dot_general(qbd, kb, _NT, preferred_element_type=F32) + bias
    p = jnp.exp(s - jnp.max(s, axis=-1, keepdims=True)).astype(BF16)
    oe = _dot(p, vb)
    return oe[:, 0:LANES] / oe[:, LANES:2 * LANES]


def _ungroup(on, nq):
    lo = lax.broadcasted_iota(jnp.int32, (nq, LANES), 1) < HEAD_DIM
    return (jnp.where(lo, on[0:nq], on[nq:2 * nq]),
            jnp.where(lo, on[2 * nq:3 * nq], on[3 * nq:4 * nq]))


def _cross_head(xq_h, mk_h, mvx_h):
    s = lax.dot_general(xq_h, mk_h, _NT, preferred_element_type=F32)
    p = jnp.exp(s - jnp.max(s, axis=-1, keepdims=True)).astype(BF16)
    oe = _dot(p, mvx_h)
    return oe[:, 0:X_HEAD_DIM] / oe[:, X_HEAD_DIM:2 * X_HEAD_DIM]


def _memkv_kernel(mem_ref, g_ref, wk_ref, wv_ref, mk_ref, mv_ref, mkb_ref, mvx_ref):
    nb = mem_ref.shape[0]
    mn = (_rms_scale(mem_ref[...].reshape(nb * N_MEM, D_MODEL)) * g_ref[0]).astype(BF16)
    mk = _dot(mn, wk_ref[0].astype(BF16))
    mv = _dot(mn, wv_ref[0].astype(BF16))
    ones = jnp.ones((N_MEM, X_HEAD_DIM), BF16)
    for b in range(nb):
        rows = slice(b * N_MEM, (b + 1) * N_MEM)
        mkb_ref[0, b] = mk[rows].astype(BF16)
        for h in range(X_HEADS):
            cols = slice(h * X_HEAD_DIM, (h + 1) * X_HEAD_DIM)
            mk_ref[0, b, pl.ds(h, N_MEM, stride=X_HEADS), :] = mk[rows, cols]
            mv_ref[0, b, pl.ds(h, N_MEM, stride=X_HEADS), :] = mv[rows, cols]
            mvx_ref[0, b, h, :, 0:X_HEAD_DIM] = mv[rows, cols].astype(BF16)
            mvx_ref[0, b, h, :, X_HEAD_DIM:2 * X_HEAD_DIM] = ones


def _memory_kv(mem, g, wk, wv):
    nb = mem.shape[0]
    return pl.pallas_call(
        _memkv_kernel,
        grid=(DEPTH,),
        in_specs=[
            pl.BlockSpec((nb, N_MEM, D_MODEL), lambda l: (0, 0, 0)),
            pl.BlockSpec((1, 1, D_MODEL), lambda l: (l, 0, 0)),
            pl.BlockSpec((1, D_MODEL, X_W), lambda l: (l, 0, 0)),
            pl.BlockSpec((1, D_MODEL, X_W), lambda l: (l, 0, 0)),
        ],
        out_specs=[
            pl.BlockSpec((1, nb, N_MEM * X_HEADS, X_HEAD_DIM), lambda l: (l, 0, 0, 0)),
            pl.BlockSpec((1, nb, N_MEM * X_HEADS, X_HEAD_DIM), lambda l: (l, 0, 0, 0)),
            pl.BlockSpec((1, nb, N_MEM, X_W), lambda l: (l, 0, 0, 0)),
            pl.BlockSpec((1, nb, X_HEADS, N_MEM, 2 * X_HEAD_DIM), lambda l: (l, 0, 0, 0, 0)),
        ],
        out_shape=[
            jax.ShapeDtypeStruct((DEPTH, nb, N_MEM * X_HEADS, X_HEAD_DIM), F32),
            jax.ShapeDtypeStruct((DEPTH, nb, N_MEM * X_HEADS, X_HEAD_DIM), F32),
            jax.ShapeDtypeStruct((DEPTH, nb, N_MEM, X_W), BF16),
            jax.ShapeDtypeStruct((DEPTH, nb, X_HEADS, N_MEM, 2 * X_HEAD_DIM), BF16),
        ],
        name="memory_kv",
    )(mem, g.reshape(DEPTH, 1, D_MODEL), wk, wv)


def _prompt_tile(t, x_ref, g_ref, win_ref, wpa_ref, wpb_ref, wpx_ref, wout_ref, cw_ref, cb_ref,
                 bias_ref, bd_ref, mk_ref, mvx_ref, fg_ref,
                 xo_ref, kl_ref, vl_ref, ul_ref,
                 kr_ref, vv_ref, uc_ref, q_ref, attn_ref, *, tile, final):
    @pl.when(t == 0)
    def _():
        kr_ref[:, 0:WINDOW, :] = jnp.zeros((N_KV_HEADS, WINDOW, GQ_W), BF16)
        vv_ref[:, 0:WINDOW, 0:LANES] = jnp.zeros((N_KV_HEADS, WINDOW, LANES), BF16)
        vv_ref[:, :, LANES:2 * LANES] = jnp.ones((N_KV_HEADS, WINDOW + tile, LANES), BF16)
        uc_ref[0:CONV_PAD, :] = jnp.zeros((CONV_PAD, CONV_W), F32)

    x = x_ref[...]
    xn = (_rms_scale(x) * g_ref[...]).astype(BF16)

    def proj(off):
        return _dot(xn, _w_in(win_ref, off))

    w_q = _w_in(win_ref, OFF_Q)
    for r in range(0, tile, Q_ROWS):
        q_ref[r:r + Q_ROWS, :] = (_dot(xn[r:r + Q_ROWS], w_q) * (HEAD_DIM ** -0.5)).astype(BF16)
    half = tile // 2
    w_kv = _w_in(win_ref, OFF_K)
    kv = jnp.concatenate([_dot(xn[0:half], w_kv), _dot(xn[half:tile], w_kv)], axis=0)
    k = kv[:, 0:KV_W]
    v = kv[:, KV_W:2 * KV_W]
    ga = proj(OFF_GA)
    kl_ref[0] = k[tile - WINDOW:tile]
    vl_ref[0] = v[tile - WINDOW:tile]
    for j, (kj, vj) in enumerate(zip(_dup_heads(k), _dup_heads(v))):
        kj = kj.astype(BF16)
        kr_ref[j, WINDOW:WINDOW + tile, 0:LANES] = kj
        kr_ref[j, WINDOW:WINDOW + tile, LANES:2 * LANES] = kj
        vv_ref[j, WINDOW:WINDOW + tile, 0:LANES] = vj.astype(BF16)

    fillers = [functools.partial(proj, off) for off in (OFF_BB, OFF_CC, OFF_HH, OFF_GB, OFF_XQ, OFF_GX)]
    fillers += [lambda i=i: jax.nn.sigmoid(proj(OFF_MG + i * D_MODEL)) for i in (1, 2)]
    filled = []
    n_chunks = tile // CHUNK
    ktail, vtail = _key_tails(BAND)

    for c in range(n_chunks):
        r0 = c * CHUNK
        qc = q_ref[r0:r0 + CHUNK, :]
        variant = jnp.where(t == 0, c, 2) if c < 2 else 2
        for j in range(N_KV_HEADS):
            qg = qc[:, j * GQ_W:(j + 1) * GQ_W]
            qbd = jnp.concatenate([qg] * GROUP, axis=0) * bd_ref[...]
            on = _band_attention(qbd, jnp.concatenate([kr_ref[j, r0:r0 + BAND, :], ktail], axis=0),
                                 jnp.concatenate([vv_ref[j, r0:r0 + BAND, :], vtail], axis=0),
                                 bias_ref[variant, j])
            lo, hi = _ungroup(on, CHUNK)
            attn_ref[r0:r0 + CHUNK, j * GQ_W:j * GQ_W + LANES] = lo
            attn_ref[r0:r0 + CHUNK, j * GQ_W + LANES:(j + 1) * GQ_W] = hi
        while len(filled) * n_chunks < (c + 1) * len(fillers):
            filled.append(fillers[len(filled)]())
    bb, cc, hh, gb, xq, gx, gate_b, gate_x = filled
    kr_ref[:, 0:WINDOW, :] = kr_ref[:, tile:tile + WINDOW, :]
    vv_ref[:, 0:WINDOW, 0:LANES] = vv_ref[:, tile:tile + WINDOW, 0:LANES]

    u = cc * hh
    uc_ref[CONV_PAD:CONV_PAD + tile, :] = u
    ul_ref[0] = u[tile - (CONV_K - 1):tile]
    conv = (uc_ref[CONV_PAD - 2:CONV_PAD - 2 + tile, :] * cw_ref[0:1, :]
            + uc_ref[CONV_PAD - 1:CONV_PAD - 1 + tile, :] * cw_ref[1:2, :]
            + u * cw_ref[2:3, :]) + cb_ref[...]
    uc_ref[0:CONV_PAD, :] = uc_ref[tile:tile + CONV_PAD, :]
    zb = (bb * conv * _silu(gb)).astype(BF16)
    za = (attn_ref[...] * _silu(ga)).astype(BF16)

    xq = (xq * (X_HEAD_DIM ** -0.5)).astype(BF16)
    merge_work = [lambda: _dot(za, wpa_ref[...]),
                  lambda: jax.nn.sigmoid(proj(OFF_MG)),
                  lambda: _dot(zb, wpb_ref[...])]
    merged = []
    heads = []
    for h in range(X_HEADS):
        heads.append(_cross_head(xq[:, h * X_HEAD_DIM:(h + 1) * X_HEAD_DIM],
                                 mk_ref[0, :, h * X_HEAD_DIM:(h + 1) * X_HEAD_DIM], mvx_ref[0, h]))
        if h < len(merge_work):
            merged.append(merge_work[h]())
    ya, gate_a, yb = merged
    zx = (jnp.concatenate(heads, axis=1) * _silu(gx)).astype(BF16)
    acc = gate_a * ya + gate_b * yb + gate_x * _dot(zx, wpx_ref[...])
    out = x + _dot(acc.astype(BF16), wout_ref[...])
    if final:
        out = _rms_scale(out) * fg_ref[...]
    xo_ref[...] = out


def _const_spec(shape):
    nd = len(shape)
    return pl.BlockSpec(shape, lambda *_: (0,) * nd, pipeline_mode=pl.Buffered(1))


def _layer_spec(shape):
    nd = len(shape)
    return pl.BlockSpec((None,) + tuple(shape), lambda *a: (a[-1][0],) + (0,) * nd, pipeline_mode=pl.Buffered(1))


def _sample_step(x_ref, g_ref, win_ref, wpa_ref, wpb_ref, wpx_ref, wout_ref, cw_ref, cb_ref,
                 bias_ref, bd_ref, ck_ref, cv_ref, cc_ref, cmk_hbm, cmv_hbm, fg_ref,
                 xo_ref, ko_ref, vo_ref, uo_ref,
                 uc_ref, attn_ref, conv_ref, xat_ref, cmem_ref, cmem_sem, *, layer, nb, nt, final):
    nbp = nb // DECODE_PARTS
    rows_p = nbp * nt

    def part(pi, carry):
        b0 = pi * nbp
        part_rows = pl.ds(pl.multiple_of(pi * rows_p, rows_p), rows_p)
        mem_copies = [pltpu.make_async_copy(src.at[layer, pl.ds(b0, nbp)], cmem_ref.at[n], cmem_sem.at[n])
                      for n, src in enumerate((cmk_hbm, cmv_hbm))]
        for copy in mem_copies:
            copy.start()
        x = x_ref[part_rows, :]
        xn = (_rms_scale(x) * g_ref[...]).astype(BF16)

        def proj(off):
            return _dot(xn, _w_in(win_ref, off))

        q = (proj(OFF_Q) * (HEAD_DIM ** -0.5)).astype(BF16)
        kv = proj(OFF_K)
        k = kv[:, 0:KV_W]
        v = kv[:, KV_W:2 * KV_W]
        ga = proj(OFF_GA)
        ko_ref[part_rows, :] = k
        vo_ref[part_rows, :] = v
        ones = jnp.ones((WINDOW + nt, LANES), BF16)
        ktail, vtail = _key_tails(WINDOW + nt)
        fillers = [functools.partial(proj, off) for off in (OFF_BB, OFF_CC, OFF_HH, OFF_GB, OFF_XQ, OFF_GX)]
        filled = []
        for b in range(nbp):
            rows = slice(b * nt, (b + 1) * nt)
            kall = _dup_heads(jnp.concatenate([ck_ref[b0 + b], k[rows]], axis=0))
            vall = _dup_heads(jnp.concatenate([cv_ref[b0 + b], v[rows]], axis=0))
            for j in range(N_KV_HEADS):
                kj = kall[j].astype(BF16)
                vj = vall[j].astype(BF16)
                qg = q[rows, j * GQ_W:(j + 1) * GQ_W]
                qbd = jnp.concatenate([qg] * GROUP, axis=0) * bd_ref[...]
                on = _band_attention(qbd, jnp.concatenate([jnp.concatenate([kj, kj], axis=1), ktail], axis=0),
                                     jnp.concatenate([jnp.concatenate([vj, ones], axis=1), vtail], axis=0),
                                     bias_ref[j])
                lo, hi = _ungroup(on, nt)
                attn_ref[rows, j * GQ_W:j * GQ_W + LANES] = lo
                attn_ref[rows, j * GQ_W + LANES:(j + 1) * GQ_W] = hi
            while len(filled) * nbp < (b + 1) * len(fillers):
                filled.append(fillers[len(filled)]())
        bb, cc, hh, gb, p3q, gx = filled
        za = attn_ref[...] * _silu(ga)
        for copy in mem_copies:
            copy.wait()

        u = cc * hh
        for b in range(nbp):
            rows = slice(b * nt, (b + 1) * nt)
            ub = u[rows]
            uc_ref[b, CONV_PAD - (CONV_K - 1):CONV_PAD, :] = cc_ref[b0 + b]
            uc_ref[b, CONV_PAD:CONV_PAD + nt, :] = ub
            uo_ref[b0 + b] = ub[nt - (CONV_K - 1):nt]
            conv_ref[rows, :] = (uc_ref[b, CONV_PAD - 2:CONV_PAD - 2 + nt, :] * cw_ref[0:1, :]
                                 + uc_ref[b, CONV_PAD - 1:CONV_PAD - 1 + nt, :] * cw_ref[1:2, :]
                                 + ub * cw_ref[2:3, :]) + cb_ref[...]
        zb = bb * conv_ref[...] * _silu(gb)

        xq = (p3q * (X_HEAD_DIM ** -0.5)).astype(BF16)
        mones = jnp.ones((N_MEM, X_HEAD_DIM), BF16)
        merge_work = [lambda: _dot(za.astype(BF16), wpa_ref[...]),
                      lambda: jax.nn.sigmoid(proj(OFF_MG)),
                      lambda: _dot(zb.astype(BF16), wpb_ref[...]),
                      lambda: jax.nn.sigmoid(proj(OFF_MG + D_MODEL)),
                      lambda: jax.nn.sigmoid(proj(OFF_MG + 2 * D_MODEL))]
        merged = []
        for b in range(nbp):
            rows = slice(b * nt, (b + 1) * nt)
            for h in range(X_HEADS):
                cols = slice(h * X_HEAD_DIM, (h + 1) * X_HEAD_DIM)
                head_rows = pl.ds(h, N_MEM, stride=X_HEADS)
                mvx = jnp.concatenate([cmem_ref[1, b, head_rows, :].astype(BF16), mones], axis=1)
                xat_ref[rows, cols] = _cross_head(xq[rows, cols], cmem_ref[0, b, head_rows, :].astype(BF16), mvx)
            while len(merged) * nbp < (b + 1) * len(merge_work):
                merged.append(merge_work[len(merged)]())
        ya, gate_a, yb, gate_b, gate_x = merged
        zx = xat_ref[...] * _silu(gx)

        acc = gate_a * ya + gate_b * yb + gate_x * _dot(zx.astype(BF16), wpx_ref[...])
        out = x + _dot(acc.astype(BF16), wout_ref[...])
        if final:
            out = _rms_scale(out) * fg_ref[...]
        xo_ref[part_rows, :] = out
        return carry

    lax.fori_loop(0, DECODE_PARTS, part, 0)


def _layer_kernel(layer_ref, x_hbm, g_ref, win_hbm, wpa_hbm, wpb_hbm, wpx_hbm, wout_hbm,
                  cw_ref, cb_ref, fg_ref, bias_ref, bd_ref, mk_ref, mvx_ref,
                  xs_ref, sbias_ref, sbd_ref, ck_ref, cv_ref, cc_ref, cmk_hbm, cmv_hbm,
                  xo_hbm, kl_ref, vl_ref, ul_ref, xso_ref, ko_ref, vo_ref, uo_ref,
                  win_ref, wpa_ref, wpb_ref, wpx_ref, wout_ref, xin_ref, xout_ref, xin_sem, xout_sem,
                  kr_ref, vv_ref, uc_ref, q_ref, attn_ref, suc_ref, sattn_ref, sconv_ref, sxat_ref, cmem_ref, cmem_sem,
                  sin_ref, spa_ref, spb_ref, spx_ref, sout_ref, wsem,
                  *, tile, tiles_per_seq, nb, nt, final):
    i = pl.program_id(0)
    weights = (win_ref, wpa_ref, wpb_ref, wpx_ref, wout_ref)

    @pl.when(i < CAST_STEPS)
    def _():
        layer = layer_ref[0]
        sources = (win_hbm, wpa_hbm, wpb_hbm, wpx_hbm, wout_hbm)
        stages = (sin_ref, spa_ref, spb_ref, spx_ref, sout_ref)

        def chunk_copies(c, slot):
            return [pltpu.make_async_copy(src.at[layer, pl.ds(c * stage.shape[1], stage.shape[1])],
                                          stage.at[slot], wsem.at[slot, n])
                    for n, (src, stage) in enumerate(zip(sources, stages))]

        for copy in chunk_copies(0, 0):
            copy.start()

        def cast_chunk(c, carry):
            slot = lax.rem(c, 2)

            @pl.when(c + 1 < CAST_CHUNKS)
            def _():
                for copy in chunk_copies(c + 1, 1 - slot):
                    copy.start()

            for copy in chunk_copies(c, slot):
                copy.wait()

            def chunk_rows(stage):
                rows = stage.shape[1]
                return pl.ds(pl.multiple_of(c * rows, rows), rows)

            def cast_cols(t, inner):
                cols = pl.ds(pl.multiple_of(t * CAST_COLS, CAST_COLS), CAST_COLS)
                win_ref[chunk_rows(sin_ref), cols] = sin_ref[slot, :, cols].astype(BF16)
                return inner

            lax.fori_loop(0, IN_W // CAST_COLS, cast_cols, 0)
            for stage, dst in zip(stages[1:], weights[1:]):
                dst[chunk_rows(stage), :] = stage[slot].astype(BF16)
            return carry

        lax.fori_loop(0, CAST_CHUNKS, cast_chunk, 0)

    def tile_rows(t):
        return pl.ds(pl.multiple_of(t * tile, tile), tile)

    def in_copy(b, t, slot):
        return pltpu.make_async_copy(x_hbm.at[b, tile_rows(t)], xin_ref.at[slot], xin_sem.at[slot])

    @pl.when(i == CAST_STEPS)
    def _():
        in_copy(0, 0, 0).start()
        _sample_step(xs_ref, g_ref, *weights, cw_ref, cb_ref,
                     sbias_ref, sbd_ref, ck_ref, cv_ref, cc_ref, cmk_hbm, cmv_hbm, fg_ref,
                     xso_ref, ko_ref, vo_ref, uo_ref, suc_ref, sattn_ref, sconv_ref, sxat_ref, cmem_ref, cmem_sem,
                     layer=layer_ref[0], nb=nb, nt=nt, final=final)

    @pl.when(i > CAST_STEPS)
    def _():
        b = i - (CAST_STEPS + 1)
        n_seq = pl.num_programs(0) - (CAST_STEPS + 1)
        first = b * tiles_per_seq

        def out_copy(g, slot):
            dst = xo_hbm.at[lax.div(g, tiles_per_seq), tile_rows(lax.rem(g, tiles_per_seq))]
            return pltpu.make_async_copy(xout_ref.at[slot], dst, xout_sem.at[slot])

        def tile_step(t, carry):
            slot = lax.rem(first + t, 2)

            @pl.when(t + 1 < tiles_per_seq)
            def _():
                in_copy(b, t + 1, 1 - slot).start()

            @pl.when(jnp.logical_and(t + 1 == tiles_per_seq, b + 1 < n_seq))
            def _():
                in_copy(b + 1, 0, 1 - slot).start()

            in_copy(b, t, slot).wait()

            @pl.when(first + t >= 2)
            def _():
                out_copy(first + t - 2, slot).wait()

            _prompt_tile(t, xin_ref.at[slot], g_ref, *weights, cw_ref, cb_ref,
                         bias_ref, bd_ref, mk_ref, mvx_ref, fg_ref,
                         xout_ref.at[slot], kl_ref, vl_ref, ul_ref, kr_ref, vv_ref, uc_ref, q_ref, attn_ref,
                         tile=tile, final=final)
            out_copy(first + t, slot).start()
            return carry

        lax.fori_loop(0, tiles_per_seq, tile_step, 0)

        @pl.when(b + 1 == n_seq)
        def _():
            for t in range(tiles_per_seq - 2, tiles_per_seq):
                out_copy(first + t, lax.rem(first + t, 2)).wait()


_WEIGHT_SHAPES = ((D_MODEL, IN_W), (ATT_W, D_MODEL), (CONV_W, D_MODEL), (X_W, D_MODEL), (D_MODEL, D_MODEL))


def _layer(layer, x, xs, g, win, wpa, wpb, wpx, wout, cw, cb, fg, bias, bd, mkb, mvx,
           sbias, sbd, ck, cv, cc, cmk, cmv, *, nt, final):
    nb, seq, _ = x.shape
    rows = xs.shape[0]
    nbs = rows // nt
    tile = PROMPT_TILE
    assert seq % tile == 0 and tile % WINDOW == 0
    tps = seq // tile
    assert tps >= 2
    kern = functools.partial(_layer_kernel, tile=tile, tiles_per_seq=tps, nb=nbs, nt=nt, final=final)

    def seq_of(i):
        return jnp.maximum(i - (CAST_STEPS + 1), 0)

    grid_spec = pltpu.PrefetchScalarGridSpec(
        num_scalar_prefetch=1,
        grid=(CAST_STEPS + 1 + nb,),
        in_specs=[
            pl.BlockSpec(memory_space=pl.ANY),
            _layer_spec((1, D_MODEL)),
            *[pl.BlockSpec(memory_space=pl.ANY) for _ in _WEIGHT_SHAPES],
            _layer_spec((CONV_K, CONV_W)),
            _layer_spec((1, CONV_W)),
            _const_spec((1, D_MODEL)),
            _layer_spec((3, N_KV_HEADS, GROUP * CHUNK, KEY_SLOTS)),
            _const_spec((GROUP * CHUNK, GQ_W)),
            pl.BlockSpec((None, 1, N_MEM, X_W), lambda i, l: (l[0], seq_of(i), 0, 0)),
            pl.BlockSpec((None, 1, X_HEADS, N_MEM, 2 * X_HEAD_DIM), lambda i, l: (l[0], seq_of(i), 0, 0, 0)),
            _const_spec((rows, D_MODEL)),
            _layer_spec((N_KV_HEADS, GROUP * nt, KEY_SLOTS)),
            _const_spec((GROUP * nt, GQ_W)),
            _layer_spec((nbs, WINDOW, KV_W)),
            _layer_spec((nbs, WINDOW, KV_W)),
            _layer_spec((nbs, CONV_K - 1, CONV_W)),
            pl.BlockSpec(memory_space=pl.ANY),
            pl.BlockSpec(memory_space=pl.ANY),
        ],
        out_specs=[
            pl.BlockSpec(memory_space=pl.ANY),
            pl.BlockSpec((1, WINDOW, KV_W), lambda i, l: (seq_of(i), 0, 0)),
            pl.BlockSpec((1, WINDOW, KV_W), lambda i, l: (seq_of(i), 0, 0)),
            pl.BlockSpec((1, CONV_K - 1, CONV_W), lambda i, l: (seq_of(i), 0, 0)),
            pl.BlockSpec((rows, D_MODEL), lambda i, l: (0, 0)),
            pl.BlockSpec((rows, KV_W), lambda i, l: (0, 0)),
            pl.BlockSpec((rows, KV_W), lambda i, l: (0, 0)),
            pl.BlockSpec((nbs, CONV_K - 1, CONV_W), lambda i, l: (0, 0, 0)),
        ],
        scratch_shapes=[
            *[pltpu.VMEM(shape, BF16) for shape in _WEIGHT_SHAPES],
            pltpu.VMEM((2, tile, D_MODEL), F32),
            pltpu.VMEM((2, tile, D_MODEL), F32),
            pltpu.SemaphoreType.DMA((2,)),
            pltpu.SemaphoreType.DMA((2,)),
            pltpu.VMEM((N_KV_HEADS, WINDOW + tile, GQ_W), BF16),
            pltpu.VMEM((N_KV_HEADS, WINDOW + tile, 2 * LANES), BF16),
            pltpu.VMEM((CONV_PAD + tile, CONV_W), F32),
            pltpu.VMEM((tile, ATT_W), BF16),
            pltpu.VMEM((tile, ATT_W), F32),
            pltpu.VMEM((nbs // DECODE_PARTS, CONV_PAD + nt, CONV_W), F32),
            pltpu.VMEM((rows // DECODE_PARTS, ATT_W), F32),
            pltpu.VMEM((rows // DECODE_PARTS, CONV_W), F32),
            pltpu.VMEM((rows // DECODE_PARTS, X_W), F32),
            pltpu.VMEM((2, nbs // DECODE_PARTS, N_MEM * X_HEADS, X_HEAD_DIM), F32),
            pltpu.SemaphoreType.DMA((2,)),
            *[pltpu.VMEM((2, r // CAST_CHUNKS, c), F32) for r, c in _WEIGHT_SHAPES],
            pltpu.SemaphoreType.DMA((2, len(_WEIGHT_SHAPES))),
        ],
    )
    return pl.pallas_call(
        kern,
        grid_spec=grid_spec,
        out_shape=[
            jax.ShapeDtypeStruct((nb, seq, D_MODEL), F32),
            jax.ShapeDtypeStruct((nb, WINDOW, KV_W), F32),
            jax.ShapeDtypeStruct((nb, WINDOW, KV_W), F32),
            jax.ShapeDtypeStruct((nb, CONV_K - 1, CONV_W), F32),
            jax.ShapeDtypeStruct((rows, D_MODEL), F32),
            jax.ShapeDtypeStruct((rows, KV_W), F32),
            jax.ShapeDtypeStruct((rows, KV_W), F32),
            jax.ShapeDtypeStruct((nbs, CONV_K - 1, CONV_W), F32),
        ],
        compiler_params=pltpu.CompilerParams(
            dimension_semantics=("arbitrary",), vmem_limit_bytes=VMEM_LIMIT),
        name="layer",
    )(layer, x, g, win, wpa, wpb, wpx, wout, cw, cb, fg, bias, bd, mkb, mvx,
      xs, sbias, sbd, ck, cv, cc, cmk, cmv)


def _alibi_slopes():
    return np.power(2.0, -8.0 * np.arange(1, N_HEADS + 1, dtype=np.float32) / N_HEADS).astype(np.float32)


def _alibi_bias(nq, nk, masked):
    dist = np.abs(np.arange(nq)[:, None] + WINDOW - np.arange(nk)[None, :]).astype(np.float32)
    slope = _alibi_slopes().reshape(N_KV_HEADS, GROUP)
    bias = -(slope[:, :, None, None] * dist[None, None])
    bias = np.where(np.arange(nk)[None, None, None, :] < masked, -np.inf, bias)
    return bias.reshape(N_KV_HEADS, GROUP * nq, nk).astype(np.float32)


def _bias_with_sink(alibi, attn_sink, nq):
    nk = alibi.shape[-1]
    lead = alibi.shape[:-3]
    sink = jnp.repeat(attn_sink.reshape(DEPTH, N_KV_HEADS, GROUP), nq, axis=2)[..., None]
    sink = jnp.broadcast_to(sink.reshape((DEPTH,) + (1,) * len(lead) + sink.shape[1:]),
                            (DEPTH,) + alibi.shape[:-1] + (1,))
    real = jnp.broadcast_to(jnp.asarray(alibi), (DEPTH,) + alibi.shape)
    unused = jnp.full((DEPTH,) + alibi.shape[:-1] + (KEY_SLOTS - nk - 1,), -jnp.inf, F32)
    return jnp.concatenate([real, sink, unused], axis=-1)


def _block_diag_mask(nq):
    rows = np.arange(GROUP * nq)[:, None] // nq
    cols = np.arange(GQ_W)[None, :] // HEAD_DIM
    return (rows == cols).astype(np.float32)


def kernel(x_prompt, x_sample, cache_attn_k, cache_attn_v, cache_conv, cache_mem_k, cache_mem_v,
           mem_prompt, norm_g, w_in, attn_sink, w_pa, conv_w, conv_b, w_pb, mem_norm_g,
           w_mk, w_mv, w_px, w_out, final_g):
    bp, seq, _ = x_prompt.shape
    bs, nt, _ = x_sample.shape
    win_len = cache_attn_k.shape[2]
    assert win_len == WINDOW and nt <= CHUNK and seq >= WINDOW

    bias_p = _bias_with_sink(np.stack([_alibi_bias(CHUNK, BAND, m) for m in (WINDOW, CHUNK, 0)]), attn_sink, CHUNK)
    bias_s = _bias_with_sink(_alibi_bias(nt, WINDOW + nt, 0), attn_sink, nt)
    bd_p = jnp.asarray(_block_diag_mask(CHUNK), BF16)
    bd_s = jnp.asarray(_block_diag_mask(nt), BF16)
    fg = final_g.reshape(1, D_MODEL)

    mk, mv, mkb, mvx = _memory_kv(mem_prompt, mem_norm_g, w_mk, w_mv)

    xp = x_prompt
    xs = x_sample.reshape(bs * nt, D_MODEL)
    ck = cache_attn_k.reshape(DEPTH, bs, WINDOW, KV_W)
    cv = cache_attn_v.reshape(DEPTH, bs, WINDOW, KV_W)
    g = norm_g.reshape(DEPTH, 1, D_MODEL)
    cb = conv_b.reshape(DEPTH, 1, CONV_W)
    cmk = cache_mem_k.reshape(DEPTH, bs, N_MEM * X_HEADS, X_HEAD_DIM)
    cmv = cache_mem_v.reshape(DEPTH, bs, N_MEM * X_HEADS, X_HEAD_DIM)
    pk, pv, pc, sk, sv, sc = [], [], [], [], [], []
    for l in range(DEPTH):
        final = l == DEPTH - 1
        layer = jnp.full((1,), l, jnp.int32)
        xp, kl, vl, ul, xs, ko, vo, uo = _layer(
            layer, xp, xs, g, w_in, w_pa, w_pb, w_px, w_out, conv_w, cb, fg, bias_p, bd_p, mkb, mvx,
            bias_s, bd_s, ck, cv, cache_conv, cmk, cmv, nt=nt, final=final)
        pk.append(kl)
        pv.append(vl)
        pc.append(ul)
        sk.append(ko)
        sv.append(vo)
        sc.append(uo)

    kv_p = (DEPTH, bp, WINDOW, N_KV_HEADS, HEAD_DIM)
    kv_s = (DEPTH, bs, nt, N_KV_HEADS, HEAD_DIM)
    mem_shape = (DEPTH, bp, N_MEM, X_HEADS, X_HEAD_DIM)
    return (xp, xs.reshape(bs, nt, D_MODEL),
            jnp.stack(pk).reshape(kv_p), jnp.stack(pv).reshape(kv_p), jnp.stack(pc),
            mk.reshape(mem_shape), mv.reshape(mem_shape),
            jnp.stack(sk).reshape(kv_s), jnp.stack(sv).reshape(kv_s), jnp.stack(sc))
```
